```python
import jax, jax.numpy as jnp
from jax import lax
import numpy as np

D_MODEL = 1024
BATCH = 8
SEQ = 2048
DEPTH = 4
DEC_BATCH = 128
DEC_SEQ = 4
PAST_LEN = 2048
PAGE_SIZE = 128

N_MIXERS = 2
N_A_LAYERS = (DEPTH + N_MIXERS - 1) // N_MIXERS
N_B_LAYERS = DEPTH // N_MIXERS
CHUNK = 128
D_A = D_MODEL
N_GROUPS_A = 8
GROUP_DIM_A = D_A // N_GROUPS_A
N_HEADS = 8
HEAD_DIM = D_MODEL // N_HEADS
N_KV_HEADS = 2
N_IDX_HEADS = 8
IDX_DIM = 64
TOPK_MAX = 256
QUERY_BLOCK = 128
ROPE_THETA = 500000.0
ROT_DIV = 4
IDX_SCALE = float((IDX_DIM * N_IDX_HEADS) ** -0.5)
ATTN_SCALE = float(HEAD_DIM ** -0.5)
Q_W = N_HEADS * HEAD_DIM
KV_W = N_KV_HEADS * HEAD_DIM
QI_W = N_IDX_HEADS * IDX_DIM
B_PROJ = Q_W + 2 * KV_W + QI_W + IDX_DIM + N_IDX_HEADS
D_FF = 4 * D_MODEL
EPS = 1e-6

kernel_name = 'dsa_chunkmlp_hybrid_step'


def rms_norm(x, g):
    xf = x.astype(jnp.float32)
    y = xf * lax.rsqrt(jnp.mean(xf * xf, axis=-1, keepdims=True) + EPS)
    return y.astype(x.dtype) * g


def rope(x, pos):
    dim = x.shape[-1]
    r = dim // ROT_DIV
    half = r // 2
    inv = ROPE_THETA ** (-jnp.arange(half, dtype=jnp.float32) * 2.0 / r)
    ang = pos.astype(jnp.float32)[:, None] * inv[None, :]
    cos = jnp.cos(ang)[None, :, None, :]
    sin = jnp.sin(ang)[None, :, None, :]
    xr = x[..., :r].astype(jnp.float32)
    x1, x2 = xr[..., :half], xr[..., half:]
    rot = jnp.concatenate([x1 * cos - x2 * sin, x2 * cos + x1 * sin], axis=-1).astype(x.dtype)
    return jnp.concatenate([rot, x[..., r:]], axis=-1)


def sqrelu_mlp(h, w1, w2):
    a = jax.nn.relu(h @ w1)
    return (a * a) @ w2


def chunk_mlp(h, w_in, v_gain, w_s, b_s, w_out):
    N, T, _ = h.shape
    uv = jax.nn.gelu(h @ w_in)
    u, v = uv[..., :D_A], uv[..., D_A:]
    vf = v.astype(jnp.float32)
    mu = jnp.mean(vf, axis=-1, keepdims=True)
    var = jnp.mean(jnp.square(vf - mu), axis=-1, keepdims=True)
    v = ((vf - mu) * lax.rsqrt(var + EPS)).astype(h.dtype) * v_gain
    c = CHUNK if T >= CHUNK else T
    n_chunks = -(-T // c)
    tp = n_chunks * c
    vp = jnp.pad(v, ((0, 0), (0, tp - T), (0, 0))).reshape(N, n_chunks, c, N_GROUPS_A, GROUP_DIM_A)
    mask = jnp.tril(jnp.ones((c, c), dtype=bool))
    ws = jnp.where(mask[None], w_s[:, :c, :c], jnp.zeros((), w_s.dtype))
    mixed = jnp.einsum('gts,ncsgd->nctgd', ws, vp) + b_s[:, :c].T[None, None, :, :, None]
    mixed = mixed.reshape(N, tp, D_A)[:, :T]
    out = (u * mixed) @ w_out
    start = ((T - 1) // CHUNK) * CHUNK
    return out, v[:, start:]


def dsa_project(h, w_in, pos):
    N, T, _ = h.shape
    p = h @ w_in
    o1 = Q_W
    o2 = o1 + KV_W
    o3 = o2 + KV_W
    o4 = o3 + QI_W
    o5 = o4 + IDX_DIM
    q = rope(p[..., :o1].reshape(N, T, N_HEADS, HEAD_DIM), pos)
    k = rope(p[..., o1:o2].reshape(N, T, N_KV_HEADS, HEAD_DIM), pos)
    v = p[..., o2:o3].reshape(N, T, N_KV_HEADS, HEAD_DIM)
    qi = rope(p[..., o3:o4].reshape(N, T, N_IDX_HEADS, IDX_DIM), pos)
    ki = rope(p[..., o4:o5].reshape(N, T, 1, IDX_DIM), pos)[:, :, 0]
    wi = p[..., o5:]
    return q, k, v, qi, ki, wi


def index_scores(qi, wi, ki):
    dots = jnp.einsum('nthd,nsd->nths', qi, ki, preferred_element_type=jnp.float32)
    return jnp.einsum('nths,nth->nts', jax.nn.relu(dots), wi.astype(jnp.float32)) * IDX_SCALE


def attend(q, kg, vg, valid):
    N, T = q.shape[0], q.shape[1]
    qg = q.reshape(N, T, N_KV_HEADS, N_HEADS // N_KV_HEADS, HEAD_DIM)
    s = jnp.einsum('ntkgd,ntjkd->ntkgj', qg, kg, preferred_element_type=jnp.float32) * ATTN_SCALE
    s = jnp.where(valid[:, :, None, None, :], s, -jnp.inf)
    p = jax.nn.softmax(s, axis=-1)
    o = jnp.einsum('ntkgj,ntjkd->ntkgd', p.astype(vg.dtype), vg)
    return o.reshape(N, T, N_HEADS * HEAD_DIM).astype(q.dtype)


def dsa_prompt(h, w_in, w_out):
    N, T, _ = h.shape
    pos = jnp.arange(T)
    q, k, v, qi, ki, wi = dsa_project(h, w_in, pos)
    qb = min(QUERY_BLOCK, T)
    nb = T // qb
    k_top = min(TOPK_MAX, T // 4)
    bi = jnp.arange(N)[:, None, None]

    def block(args):
        q_b, qi_b, wi_b, start = args
        tq = start + jnp.arange(qb)
        sc = index_scores(qi_b, wi_b, ki)
        sc = jnp.where(pos[None, None, :] <= tq[None, :, None], sc, -jnp.inf)
        _, idx = lax.top_k(sc, k_top)
        valid = idx <= tq[None, :, None]
        return attend(q_b, k[bi, idx], v[bi, idx], valid)

    def to_blocks(a):
        return a.reshape(N, nb, qb, *a.shape[2:]).swapaxes(0, 1)

    o = lax.map(block, (to_blocks(q), to_blocks(qi), to_blocks(wi), jnp.arange(nb) * qb))
    o = o.swapaxes(0, 1).reshape(N, T, Q_W)
    return o @ w_out, k, v, ki


def dsa_sample(h, w_in, w_out, cache_k, cache_v, cache_kidx, page_table, layer):
    N, T, _ = h.shape
    n_pages = page_table.shape[1]
    past = n_pages * PAGE_SIZE
    pos = past + jnp.arange(T)
    q, k, v, qi, ki, wi = dsa_project(h, w_in, pos)
    past_ki = cache_kidx[layer, page_table].reshape(N, past, IDX_DIM)
    all_ki = jnp.concatenate([past_ki, ki.astype(past_ki.dtype)], axis=1)
    L = past + T
    k_top = min(TOPK_MAX, L // 4)
    sc = index_scores(qi, wi, all_ki)
    key_pos = jnp.arange(L)
    sc = jnp.where(key_pos[None, None, :] <= pos[None, :, None], sc, -jnp.inf)
    _, idx = lax.top_k(sc, k_top)
    bi = jnp.arange(N)[:, None, None]
    in_past = (idx < past)[..., None, None]
    pidx = jnp.minimum(idx, past - 1)
    page = page_table[bi, pidx // PAGE_SIZE]
    off = pidx % PAGE_SIZE
    nidx = jnp.clip(idx - past, 0, T - 1)
    kg = jnp.where(in_past, cache_k[layer, page, off], k[bi, nidx].astype(cache_k.dtype))
    vg = jnp.where(in_past, cache_v[layer, page, off], v[bi, nidx].astype(cache_v.dtype))
    valid = idx <= pos[None, :, None]
    o = attend(q, kg, vg, valid)
    return o @ w_out, k, v, ki


def setup_inputs(seed: int = 0) -> dict:
    key = jax.random.key(seed)
    ks = jax.random.split(key, 20)
    f32 = jnp.float32
    n_pages = PAST_LEN // PAGE_SIZE
    used = DEC_BATCH * n_pages
    n_pool = used + max(1, used // 4)
    nrm = lambda k, shp, s: jax.random.normal(k, shp, f32) * s
    page_table = jax.random.permutation(ks[0], n_pool)[:used].reshape(DEC_BATCH, n_pages).astype(jnp.int32)
    return {
        'x_prompt': nrm(ks[1], (BATCH, SEQ, D_MODEL), 1.0),
        'x_sample': nrm(ks[2], (DEC_BATCH, DEC_SEQ, D_MODEL), 1.0),
        'cache_k': nrm(ks[3], (N_B_LAYERS, n_pool, PAGE_SIZE, N_KV_HEADS, HEAD_DIM), 1.0),
        'cache_v': nrm(ks[4], (N_B_LAYERS, n_pool, PAGE_SIZE, N_KV_HEADS, HEAD_DIM), 1.0),
        'cache_kidx': nrm(ks[5], (N_B_LAYERS, n_pool, PAGE_SIZE, IDX_DIM), 1.0),
        'page_table': page_table,
        'norm_mix': 1.0 + nrm(ks[6], (DEPTH, D_MODEL), 0.02),
        'norm_ffn': 1.0 + nrm(ks[7], (DEPTH, D_MODEL), 0.02),
        'a_w_in': nrm(ks[8], (N_A_LAYERS, D_MODEL, 2 * D_A), D_MODEL ** -0.5),
        'a_v_gain': 1.0 + nrm(ks[9], (N_A_LAYERS, D_A), 0.02),
        'a_w_s': nrm(ks[10], (N_A_LAYERS, N_GROUPS_A, CHUNK, CHUNK), 0.5 * CHUNK ** -0.5),
        'a_b_s': 1.0 + nrm(ks[11], (N_A_LAYERS, N_GROUPS_A, CHUNK), 0.02),
        'a_w_out': nrm(ks[12], (N_A_LAYERS, D_A, D_MODEL), D_A ** -0.5),
        'b_w_in': nrm(ks[13], (N_B_LAYERS, D_MODEL, B_PROJ), D_MODEL ** -0.5),
        'b_w_out': nrm(ks[14], (N_B_LAYERS, Q_W, D_MODEL), Q_W ** -0.5),
        'ffn_w1': nrm(ks[15], (DEPTH, D_MODEL, D_FF), D_MODEL ** -0.5),
        'ffn_w2': nrm(ks[16], (DEPTH, D_FF, D_MODEL), D_FF ** -0.5),
        'norm_final': 1.0 + nrm(ks[17], (D_MODEL,), 0.02),
    }


def reference(x_prompt, x_sample, cache_k, cache_v, cache_kidx, page_table, norm_mix, norm_ffn,
              a_w_in, a_v_gain, a_w_s, a_b_s, a_w_out, b_w_in, b_w_out, ffn_w1, ffn_w2, norm_final):
    xp, xs = x_prompt, x_sample
    kp_l, vp_l, kip_l, ks_l, vs_l, kis_l, cvp_l, cvs_l = [], [], [], [], [], [], [], []
    for i in range(DEPTH):
        j = i // N_MIXERS
        hp = rms_norm(xp, norm_mix[i])
        hs = rms_norm(xs, norm_mix[i])
        if i % N_MIXERS == 0:
            op, cvp = chunk_mlp(hp, a_w_in[j], a_v_gain[j], a_w_s[j], a_b_s[j], a_w_out[j])
            os_, cvs = chunk_mlp(hs, a_w_in[j], a_v_gain[j], a_w_s[j], a_b_s[j], a_w_out[j])
            cvp_l.append(cvp)
            cvs_l.append(cvs)
        else:
            op, kp, vp, kip = dsa_prompt(hp, b_w_in[j], b_w_out[j])
            os_, ks, vs, kis = dsa_sample(hs, b_w_in[j], b_w_out[j], cache_k, cache_v, cache_kidx, page_table, j)
            kp_l.append(kp)
            vp_l.append(vp)
            kip_l.append(kip)
            ks_l.append(ks)
            vs_l.append(vs)
            kis_l.append(kis)
        xp = xp + op
        xs = xs + os_
        xp = xp + sqrelu_mlp(rms_norm(xp, norm_ffn[i]), ffn_w1[i], ffn_w2[i])
        xs = xs + sqrelu_mlp(rms_norm(xs, norm_ffn[i]), ffn_w1[i], ffn_w2[i])
    y_prompt = rms_norm(xp, norm_final)
    y_sample = rms_norm(xs, norm_final)
    new_k_prompt = jnp.stack(kp_l)
    new_v_prompt = jnp.stack(vp_l)
    new_kidx_prompt = jnp.stack(kip_l)
    new_k_sample = jnp.stack(ks_l)
    new_v_sample = jnp.stack(vs_l)
    new_kidx_sample = jnp.stack(kis_l)
    chunk_v_prompt = jnp.stack(cvp_l)
    chunk_v_sample = jnp.stack(cvs_l)
    return (y_prompt, y_sample, new_k_prompt, new_v_prompt, new_kidx_prompt, new_k_sample, new_v_sample, new_kidx_sample, chunk_v_prompt, chunk_v_sample)
```

```python
import functools

import jax
import jax.numpy as jnp
from jax import lax
from jax.experimental import pallas as pl
from jax.experimental.pallas import tpu as pltpu

D_MODEL = 1024
BATCH = 8
SEQ = 2048
DEPTH = 4
DEC_BATCH = 128
DEC_SEQ = 4
PAST_LEN = 2048
PAGE_SIZE = 128
N_PAGES = PAST_LEN // PAGE_SIZE
CHUNK = 128
D_A = D_MODEL
N_GROUPS_A = 8
GROUP_DIM_A = D_A // N_GROUPS_A
N_HEADS = 8
HEAD_DIM = D_MODEL // N_HEADS
N_KV_HEADS = 2
Q_PER_KV = N_HEADS // N_KV_HEADS
N_IDX_HEADS = 8
IDX_DIM = 64
TOPK_MAX = 256
QUERY_BLOCK = 128
ROPE_THETA = 500000.0
ROT_DIV = 4
IDX_SCALE = float((IDX_DIM * N_IDX_HEADS) ** -0.5)
ATTN_SCALE = float(HEAD_DIM ** -0.5)
Q_W = N_HEADS * HEAD_DIM
KV_W = N_KV_HEADS * HEAD_DIM
QI_W = N_IDX_HEADS * IDX_DIM
B_PROJ = Q_W + 2 * KV_W + QI_W + IDX_DIM + N_IDX_HEADS
D_FF = 4 * D_MODEL
EPS = 1e-6

LANES = 128
SUBLANES = 8
P_ROWS = BATCH * SEQ
S_ROWS = DEC_BATCH * DEC_SEQ
ROWS = P_ROWS + S_ROWS
TM = 512
N_TILES = ROWS // TM
P_TILES = P_ROWS // TM
TILES_PER_SEQ = SEQ // TM
CHUNKS_PER_TILE = TM // CHUNK
FF_TILE = 1024
B_PROJ_PAD = Q_W + 2 * KV_W + QI_W + LANES
KIWI_OFF = Q_W + 2 * KV_W + QI_W
S_KEYS = PAST_LEN + LANES
KV_SPAN = 512
VMEM_LIMIT = 56 * 1024 * 1024
INT_MIN = -(2 ** 31)

f32 = jnp.float32
bf16 = jnp.bfloat16

assert S_ROWS == TM and P_ROWS % TM == 0 and SEQ % TM == 0


def _params(*sem):
    return pltpu.CompilerParams(dimension_semantics=sem, vmem_limit_bytes=VMEM_LIMIT)


def _rms(x, g):
    return x * lax.rsqrt(jnp.mean(x * x, axis=-1, keepdims=True) + EPS) * g


def _gelu(x):
    c = 0.7978845608028654
    return 0.5 * x * (1.0 + jnp.tanh(c * (x + 0.044715 * (x * x * x))))


def _dot(a, b):
    return jnp.dot(a, b, preferred_element_type=f32)


def _dot_nt(a, b):
    return lax.dot_general(a, b, (((1,), (1,)), ((), ())), preferred_element_type=f32)


def _mixa_kernel(x_ref, g_ref, win_ref, gain_ref, ws_ref, bias_ref, scoef_ref, sbias_ref,
                 m_ref, cvp_ref, cvs_ref):
    i = pl.program_id(0)
    h = _rms(x_ref[...], g_ref[...]).astype(bf16)
    uv = _gelu(_dot(h, win_ref[...]))
    u = uv[:, :D_A]
    v = uv[:, D_A:]
    vc = v - jnp.mean(v, axis=-1, keepdims=True)
    vn = vc * lax.rsqrt(jnp.mean(vc * vc, axis=-1, keepdims=True) + EPS) * gain_ref[...]

    @pl.when(i < P_TILES)
    def _prompt():
        t_idx = lax.broadcasted_iota(jnp.int32, (CHUNK, CHUNK), 0)
        s_idx = lax.broadcasted_iota(jnp.int32, (CHUNK, CHUNK), 1)
        causal = s_idx <= t_idx
        ws = [jnp.where(causal, ws_ref[g], 0.0).astype(bf16) for g in range(N_GROUPS_A)]
        for c in range(CHUNKS_PER_TILE):
            rows = slice(c * CHUNK, (c + 1) * CHUNK)
            v_c = vn[rows].astype(bf16)
            mixed = jnp.concatenate(
                [_dot(ws[g], v_c[:, g * GROUP_DIM_A:(g + 1) * GROUP_DIM_A]) for g in range(N_GROUPS_A)],
                axis=1) + bias_ref[...]
            m_ref[rows, :] = (u[rows] * mixed).astype(bf16)

        @pl.when(i % TILES_PER_SEQ == TILES_PER_SEQ - 1)
        def _():
            cvp_ref[0] = vn[TM - CHUNK:]

    @pl.when(i == P_TILES)
    def _sample():
        grp = TM // SUBLANES
        v3 = vn.reshape(grp, SUBLANES, D_A)
        mixed = sbias_ref[...][None] + scoef_ref[0][None] * v3
        for d in range(1, DEC_SEQ):
            mixed = mixed + scoef_ref[d][None] * pltpu.roll(v3, d, 1)
        m_ref[...] = (u.reshape(grp, SUBLANES, D_A) * mixed).reshape(TM, D_A).astype(bf16)
        cvs_ref[...] = vn


def _mixa(x, g, w_in, gain, ws, bias, scoef, sbias):
    const2 = lambda i: (0, 0)
    const3 = lambda i: (0, 0, 0)
    return pl.pallas_call(
        _mixa_kernel,
        grid=(N_TILES,),
        in_specs=[
            pl.BlockSpec((TM, D_MODEL), lambda i: (i, 0)),
            pl.BlockSpec((1, D_MODEL), const2),
            pl.BlockSpec((D_MODEL, 2 * D_A), const2),
            pl.BlockSpec((1, D_A), const2),
            pl.BlockSpec((N_GROUPS_A, CHUNK, CHUNK), const3),
            pl.BlockSpec((CHUNK, D_A), const2),
            pl.BlockSpec((DEC_SEQ, SUBLANES, D_A), const3),
            pl.BlockSpec((SUBLANES, D_A), const2),
        ],
        out_specs=[
            pl.BlockSpec((TM, D_A), lambda i: (i, 0)),
            pl.BlockSpec((1, CHUNK, D_A), lambda i: (jnp.minimum(i // TILES_PER_SEQ, BATCH - 1), 0, 0)),
            pl.BlockSpec((S_ROWS, D_A), const2),
        ],
        out_shape=[
            jax.ShapeDtypeStruct((ROWS, D_A), bf16),
            jax.ShapeDtypeStruct((BATCH, CHUNK, D_A), f32),
            jax.ShapeDtypeStruct((S_ROWS, D_A), f32),
        ],
        compiler_params=_params("arbitrary"),
        name="mixer_a",
    )(x, g, w_in, gain, ws, bias, scoef, sbias)


def _post_kernel(x_ref, m_ref, wout_ref, g_ref, w1_ref, w2_ref, gf_ref, o_ref, x1_s, h_s, acc_s, *, final):
    f = pl.program_id(1)

    @pl.when(f == 0)
    def _():
        x1 = x_ref[...] + _dot(m_ref[...], wout_ref[...])
        x1_s[...] = x1
        h_s[...] = _rms(x1, g_ref[...]).astype(bf16)

    a = jnp.maximum(_dot(h_s[...], w1_ref[...]), 0.0)
    part = _dot((a * a).astype(bf16), w2_ref[...])

    @pl.when(f == 0)
    def _():
        acc_s[...] = part

    @pl.when(f > 0)
    def _():
        acc_s[...] += part

    @pl.when(f == pl.num_programs(1) - 1)
    def _():
        y = x1_s[...] + acc_s[...]
        o_ref[...] = _rms(y, gf_ref[...]) if final else y


def _post(x, m, w_out, g, w1, w2, g_final, final):
    const2 = lambda i, f: (0, 0)
    return pl.pallas_call(
        functools.partial(_post_kernel, final=final),
        grid=(N_TILES, D_FF // FF_TILE),
        in_specs=[
            pl.BlockSpec((TM, D_MODEL), lambda i, f: (i, 0)),
            pl.BlockSpec((TM, D_MODEL), lambda i, f: (i, 0)),
            pl.BlockSpec((D_MODEL, D_MODEL), const2),
            pl.BlockSpec((1, D_MODEL), const2),
            pl.BlockSpec((D_MODEL, FF_TILE), lambda i, f: (0, f)),
            pl.BlockSpec((FF_TILE, D_MODEL), lambda i, f: (f, 0)),
            pl.BlockSpec((1, D_MODEL), const2),
        ],
        out_specs=pl.BlockSpec((TM, D_MODEL), lambda i, f: (i, 0)),
        out_shape=jax.ShapeDtypeStruct((ROWS, D_MODEL), f32),
        scratch_shapes=[
            pltpu.VMEM((TM, D_MODEL), f32),
            pltpu.VMEM((TM, D_MODEL), bf16),
            pltpu.VMEM((TM, D_MODEL), f32),
        ],
        compiler_params=_params("arbitrary", "arbitrary"),
        name="post_ffn",
    )(x, m, w_out, g, w1, w2, g_final)


def _rope(x, cos, sin_lo, sin_hi, half):
    return x * cos + pltpu.roll(x, LANES - half, 1) * sin_lo + pltpu.roll(x, half, 1) * sin_hi


def _proj_kernel(x_ref, g_ref, w_ref, rh_ref, ri_ref, rk_ref,
                 q_ref, k_ref, v_ref, kb_ref, vb_ref, qi_ref, kiwi_ref, kib_ref):
    h = _rms(x_ref[...], g_ref[...]).astype(bf16)
    p = _dot(h, w_ref[...])
    half_h = HEAD_DIM // ROT_DIV // 2
    half_i = IDX_DIM // ROT_DIV // 2
    ch, slh, shh = rh_ref[0], rh_ref[1], rh_ref[2]
    for hd in range(N_HEADS):
        cols = slice(hd * HEAD_DIM, (hd + 1) * HEAD_DIM)
        q_ref[:, cols] = _rope(p[:, cols], ch, slh, shh, half_h).astype(bf16)
    for kh in range(N_KV_HEADS):
        cols = slice(kh * HEAD_DIM, (kh + 1) * HEAD_DIM)
        kr = _rope(p[:, Q_W + kh * HEAD_DIM:Q_W + (kh + 1) * HEAD_DIM], ch, slh, shh, half_h)
        k_ref[:, cols] = kr
        kb_ref[:, cols] = kr.astype(bf16)
    v = p[:, Q_W + KV_W:Q_W + 2 * KV_W]
    v_ref[...] = v
    vb_ref[...] = v.astype(bf16)
    ci, sli, shi = ri_ref[0], ri_ref[1], ri_ref[2]
    off = Q_W + 2 * KV_W
    for j in range(QI_W // LANES):
        cols = slice(j * LANES, (j + 1) * LANES)
        qi_ref[:, cols] = _rope(p[:, off + j * LANES:off + (j + 1) * LANES], ci, sli, shi, half_i).astype(bf16)
    kiwi = _rope(p[:, KIWI_OFF:KIWI_OFF + LANES], rk_ref[0], rk_ref[1], rk_ref[2], half_i)
    kiwi_ref[...] = kiwi
    kib_ref[...] = kiwi[:, :IDX_DIM].astype(bf16)


def _proj(x, g, w, rope_h, rope_i, rope_k):
    const2 = lambda i: (0, 0)
    rope_map = lambda i: (0, jnp.where(i < P_TILES, i % TILES_PER_SEQ, TILES_PER_SEQ), 0)
    row = lambda i: (i, 0)
    widths = [(Q_W, bf16), (KV_W, f32), (KV_W, f32), (KV_W, bf16), (KV_W, bf16), (QI_W, bf16), (LANES, f32),
              (IDX_DIM, bf16)]
    return pl.pallas_call(
        _proj_kernel,
        grid=(N_TILES,),
        in_specs=[
            pl.BlockSpec((TM, D_MODEL), row),
            pl.BlockSpec((1, D_MODEL), const2),
            pl.BlockSpec((D_MODEL, B_PROJ_PAD), const2),
            pl.BlockSpec((3, TM, LANES), rope_map),
            pl.BlockSpec((3, TM, LANES), rope_map),
            pl.BlockSpec((3, TM, LANES), rope_map),
        ],
        out_specs=[pl.BlockSpec((TM, w_), row) for w_, _ in widths],
        out_shape=[jax.ShapeDtypeStruct((ROWS, w_), dt) for w_, dt in widths],
        compiler_params=_params("arbitrary"),
        name="dsa_proj",
    )(x, g, w, rope_h, rope_i, rope_k)


def _count(mask):
    return jnp.sum(jnp.where(mask, 1.0, 0.0), axis=1, keepdims=True)


def _topk_mask(sc, valid, col, k):
    n_keys = sc.shape[1]
    bits = pltpu.bitcast(sc, jnp.int32)
    key = bits ^ ((bits >> 31) & 0x7FFFFFFF)
    key = jnp.where(valid, key, INT_MIN)
    kf = float(k)
    zero = jnp.zeros((sc.shape[0], 1), jnp.int32)
    thr = jnp.where(_count(key >= zero) >= kf, zero, zero + INT_MIN)

    def bit_step(j, thr):
        cand = thr | lax.shift_left(jnp.int32(1), 30 - j)
        return jnp.where(_count(key >= cand) >= kf, cand, thr)

    thr = lax.fori_loop(0, 31, bit_step, thr)
    above = key > thr
    tie = (key == thr) & valid
    need = kf - _count(above)
    excess = jnp.max(_count(tie) - need)
    n_bits = max(1, (n_keys - 1).bit_length())

    def first_ties():
        def step(j, p):
            cand = p + lax.shift_left(jnp.int32(1), n_bits - 1 - j)
            return jnp.where(_count(tie & (col < cand)) < need, cand, p)

        return lax.fori_loop(0, n_bits, step, zero)

    last = lax.cond(excess > 0.0, first_ties, lambda: zero + n_keys)
    return valid & (above | (tie & (col <= last)))


def _masked_attention(q_rows, k, v, sel_rows):
    s = _dot_nt(q_rows, k) * ATTN_SCALE
    s = jnp.where(sel_rows, s, -jnp.inf)
    p = jnp.exp(s - jnp.max(s, axis=1, keepdims=True))
    denom = jnp.sum(p, axis=1, keepdims=True)
    return _dot(p.astype(bf16), v) / denom


def _dsa_prompt_kernel(q_ref, qi_ref, kiwi_ref, k_ref, v_ref, ki_ref, o_ref):
    i = pl.program_id(1)

    def run(n_keys):
        ki = ki_ref[:n_keys, :]
        wi = kiwi_ref[:, IDX_DIM:IDX_DIM + N_IDX_HEADS]
        sc = jnp.zeros((QUERY_BLOCK, n_keys), f32)
        for hd in range(N_IDX_HEADS):
            d = _dot_nt(qi_ref[:, hd * IDX_DIM:(hd + 1) * IDX_DIM], ki)
            sc = sc + jnp.maximum(d, 0.0) * wi[:, hd:hd + 1]
        sc = sc * IDX_SCALE
        row = i * QUERY_BLOCK + lax.broadcasted_iota(jnp.int32, (QUERY_BLOCK, n_keys), 0)
        col = lax.broadcasted_iota(jnp.int32, (QUERY_BLOCK, n_keys), 1)
        valid = col <= row
        sel = _topk_mask(sc, valid, col, min(TOPK_MAX, SEQ // 4))
        sel_rows = jnp.concatenate([sel] * Q_PER_KV, axis=0)
        for kh in range(N_KV_HEADS):
            kv_cols = slice(kh * HEAD_DIM, (kh + 1) * HEAD_DIM)
            q_rows = jnp.concatenate(
                [q_ref[:, (kh * Q_PER_KV + g) * HEAD_DIM:(kh * Q_PER_KV + g + 1) * HEAD_DIM]
                 for g in range(Q_PER_KV)], axis=0)
            o = _masked_attention(q_rows, k_ref[:n_keys, kv_cols], v_ref[:n_keys, kv_cols], sel_rows)
            for g in range(Q_PER_KV):
                hd = kh * Q_PER_KV + g
                o_ref[:, hd * HEAD_DIM:(hd + 1) * HEAD_DIM] = o[g * QUERY_BLOCK:(g + 1) * QUERY_BLOCK].astype(bf16)

    blocks_per_span = KV_SPAN // QUERY_BLOCK
    for span in range(SEQ // KV_SPAN):
        pl.when(i // blocks_per_span == span)(functools.partial(run, (span + 1) * KV_SPAN))


def _dsa_prompt(q, qi, kiwi, kb, vb, kib):
    nq = SEQ // QUERY_BLOCK
    qmap = lambda b, i: (b * nq + i, 0)
    kmap = lambda b, i: (b, 0)
    return pl.pallas_call(
        _dsa_prompt_kernel,
        grid=(BATCH, nq),
        in_specs=[
            pl.BlockSpec((QUERY_BLOCK, Q_W), qmap),
            pl.BlockSpec((QUERY_BLOCK, QI_W), qmap),
            pl.BlockSpec((QUERY_BLOCK, LANES), qmap),
            pl.BlockSpec((SEQ, KV_W), kmap),
            pl.BlockSpec((SEQ, KV_W), kmap),
            pl.BlockSpec((SEQ, IDX_DIM), kmap),
        ],
        out_specs=pl.BlockSpec((QUERY_BLOCK, Q_W), qmap),
        out_shape=jax.ShapeDtypeStruct((P_ROWS, Q_W), bf16),
        compiler_params=_params("arbitrary", "arbitrary"),
        name="dsa_prompt",
    )(q, qi, kiwi, kb, vb, kib)


def _sample_valid(shape, row_offset=0):
    t = (lax.broadcasted_iota(jnp.int32, shape, 0) + row_offset) % DEC_SEQ
    col = lax.broadcasted_iota(jnp.int32, shape, 1)
    return col <= t + PAST_LEN, col


def _sidx_kernel(pt_ref, qi_ref, w_ref, kin_ref, *rest):
    pages, sc_ref = rest[:N_PAGES], rest[N_PAGES]
    qi = qi_ref[0]
    d = jnp.concatenate([_dot_nt(qi, pg[0].astype(bf16)) for pg in pages] + [_dot_nt(qi, kin_ref[0])], axis=1)
    d = jnp.maximum(d, 0.0) * w_ref[0]
    sc = jnp.sum(d.reshape(DEC_SEQ, N_IDX_HEADS, S_KEYS), axis=1) * IDX_SCALE
    sc_ref[0] = sc


def _sample_index_scores(page_table, qi32, w32, ki_new, cache_kidx, layer):
    n_pool = cache_kidx.shape[0] // (DEPTH // 2)
    seq3 = lambda n, pt: (n, 0, 0)

    def page_map(p):
        return lambda n, pt: (layer * n_pool + pt[n, p], 0, 0)

    grid_spec = pltpu.PrefetchScalarGridSpec(
        num_scalar_prefetch=1,
        grid=(DEC_BATCH,),
        in_specs=[
            pl.BlockSpec((1, DEC_SEQ * N_IDX_HEADS, IDX_DIM), seq3),
            pl.BlockSpec((1, DEC_SEQ * N_IDX_HEADS, 1), seq3),
            pl.BlockSpec((1, LANES, IDX_DIM), seq3),
        ] + [pl.BlockSpec((1, PAGE_SIZE, IDX_DIM), page_map(p)) for p in range(N_PAGES)],
        out_specs=pl.BlockSpec((1, DEC_SEQ, S_KEYS), seq3),
    )
    return pl.pallas_call(
        _sidx_kernel,
        grid_spec=grid_spec,
        out_shape=jax.ShapeDtypeStruct((DEC_BATCH, DEC_SEQ, S_KEYS), f32),
        compiler_params=_params("arbitrary"),
        name="dsa_sample_index",
    )(page_table, qi32, w32, ki_new, *([cache_kidx] * N_PAGES))


def _stopk_kernel(sc_ref, bias_ref):
    r0 = pl.program_id(0) * sc_ref.shape[0]
    valid, col = _sample_valid(sc_ref.shape, r0)
    sel = _topk_mask(sc_ref[...], valid, col, min(TOPK_MAX, (PAST_LEN + DEC_SEQ) // 4))
    bias_ref[...] = jnp.where(sel, 0.0, -jnp.inf)


def _sample_topk_bias(sc):
    tr = 128
    return pl.pallas_call(
        _stopk_kernel,
        grid=(S_ROWS // tr,),
        in_specs=[pl.BlockSpec((tr, S_KEYS), lambda i: (i, 0))],
        out_specs=pl.BlockSpec((tr, S_KEYS), lambda i: (i, 0)),
        out_shape=jax.ShapeDtypeStruct((S_ROWS, S_KEYS), f32),
        compiler_params=_params("arbitrary"),
        name="dsa_sample_topk",
    )(sc)


def _satt_kernel(pt_ref, q_ref, bias_ref, kn_ref, vn_ref, *rest):
    k_pages, v_pages, o_ref = rest[:N_PAGES], rest[N_PAGES:2 * N_PAGES], rest[2 * N_PAGES]
    bias = bias_ref[0]
    for kh in range(N_KV_HEADS):
        cols = slice(kh * HEAD_DIM, (kh + 1) * HEAD_DIM)
        q = q_ref[0, kh]
        s = jnp.concatenate([_dot_nt(q, pg[0, :, cols].astype(bf16)) for pg in k_pages]
                            + [_dot_nt(q, kn_ref[0, :, cols])], axis=1)
        s = s * ATTN_SCALE + bias
        p = jnp.exp(s - jnp.max(s, axis=1, keepdims=True))
        denom = jnp.sum(p, axis=1, keepdims=True)
        p = p.astype(bf16)
        o = _dot(p[:, PAST_LEN:], vn_ref[0, :, cols])
        for j, pg in enumerate(v_pages):
            o = o + _dot(p[:, j * PAGE_SIZE:(j + 1) * PAGE_SIZE], pg[0, :, cols].astype(bf16))
        o_ref[0, kh] = (o / denom).astype(bf16)


def _sample_attention(page_table, q16, bias16, k_new, v_new, cache_k, cache_v, layer):
    n_pool = cache_k.shape[0] // (DEPTH // 2)
    rows = DEC_SEQ * Q_PER_KV
    seq3 = lambda n, pt: (n, 0, 0)
    seq4 = lambda n, pt: (n, 0, 0, 0)

    def page_map(p):
        return lambda n, pt: (layer * n_pool + pt[n, p], 0, 0)

    page_specs = [pl.BlockSpec((1, PAGE_SIZE, KV_W), page_map(p)) for p in range(N_PAGES)]
    grid_spec = pltpu.PrefetchScalarGridSpec(
        num_scalar_prefetch=1,
        grid=(DEC_BATCH,),
        in_specs=[
            pl.BlockSpec((1, N_KV_HEADS, rows, HEAD_DIM), seq4),
            pl.BlockSpec((1, rows, S_KEYS), seq3),
            pl.BlockSpec((1, LANES, KV_W), seq3),
            pl.BlockSpec((1, LANES, KV_W), seq3),
        ] + page_specs + page_specs,
        out_specs=pl.BlockSpec((1, N_KV_HEADS, rows, HEAD_DIM), seq4),
    )
    return pl.pallas_call(
        _satt_kernel,
        grid_spec=grid_spec,
        out_shape=jax.ShapeDtypeStruct((DEC_BATCH, N_KV_HEADS, rows, HEAD_DIM), bf16),
        compiler_params=_params("arbitrary"),
        name="dsa_sample_attn",
    )(page_table, q16, bias16, k_new, v_new, *([cache_k] * N_PAGES), *([cache_v] * N_PAGES))


def _rope_tables():
    pos = jnp.concatenate([jnp.arange(SEQ), PAST_LEN + jnp.arange(TM) % DEC_SEQ]).astype(f32)

    def pattern(dim, n_rot_groups):
        r = dim // ROT_DIV
        half = r // 2
        inv = ROPE_THETA ** (-jnp.arange(half, dtype=f32) * 2.0 / r)
        ang = pos[:, None] * inv[None, :]
        cos, sin = jnp.cos(ang), jnp.sin(ang)
        ones = jnp.ones((pos.shape[0], dim - r), f32)
        zeros = jnp.zeros((pos.shape[0], dim - r), f32)
        zh = jnp.zeros_like(sin)
        c = jnp.concatenate([cos, cos, ones], axis=1)
        lo = jnp.concatenate([-sin, zh, zeros], axis=1)
        hi = jnp.concatenate([zh, sin, zeros], axis=1)
        ident = (jnp.ones((pos.shape[0], dim), f32), jnp.zeros((pos.shape[0], dim), f32),
                 jnp.zeros((pos.shape[0], dim), f32))
        reps = LANES // dim
        parts = [(c, lo, hi) if j < n_rot_groups else ident for j in range(reps)]
        return jnp.stack([jnp.concatenate([p_[t] for p_ in parts], axis=1) for t in range(3)])

    return pattern(HEAD_DIM, 1), pattern(IDX_DIM, 2), pattern(IDX_DIM, 1)


def _sample_mix_tables(w_s, b_s):
    t = jnp.arange(SUBLANES) % DEC_SEQ
    coefs = []
    for d in range(DEC_SEQ):
        c = jnp.where((t - d >= 0)[None, :], w_s[:, t, jnp.maximum(t - d, 0)], 0.0)
        coefs.append(jnp.repeat(c.T, GROUP_DIM_A, axis=1))
    bias = jnp.repeat(b_s[:, t].T, GROUP_DIM_A, axis=1)
    return jnp.stack(coefs), bias


def kernel(x_prompt, x_sample, cache_k, cache_v, cache_kidx, page_table, norm_mix, norm_ffn,
           a_w_in, a_v_gain, a_w_s, a_b_s, a_w_out, b_w_in, b_w_out, ffn_w1, ffn_w2, norm_final):
    x = jnp.concatenate([x_prompt.reshape(P_ROWS, D_MODEL), x_sample.reshape(S_ROWS, D_MODEL)], axis=0)
    rope_h, rope_i, rope_k = _rope_tables()
    n_pool = cache_k.shape[1]
    ck = cache_k.reshape(-1, PAGE_SIZE, KV_W)
    cv = cache_v.reshape(-1, PAGE_SIZE, KV_W)
    cki = cache_kidx.reshape(-1, PAGE_SIZE, IDX_DIM)
    g_final = norm_final.reshape(1, D_MODEL)
    outs = {name: [] for name in ("kp", "vp", "kip", "ks", "vs", "kis", "cvp", "cvs")}

    for i in range(DEPTH):
        j = i // 2
        g_mix = norm_mix[i].reshape(1, D_MODEL)
        if i % 2 == 0:
            scoef, sbias = _sample_mix_tables(a_w_s[j], a_b_s[j])
            bias = jnp.repeat(a_b_s[j].T, GROUP_DIM_A, axis=1)
            m, cvp, cvs = _mixa(x, g_mix, a_w_in[j].astype(bf16), a_v_gain[j].reshape(1, D_A), a_w_s[j],
                                bias, scoef, sbias)
            outs["cvp"].append(cvp)
            outs["cvs"].append(cvs.reshape(DEC_BATCH, DEC_SEQ, D_A))
            w_out = a_w_out[j]
        else:
            w = jnp.pad(b_w_in[j], ((0, 0), (0, B_PROJ_PAD - B_PROJ))).astype(bf16)
            q, k, v, kb, vb, qi, kiwi, kib = _proj(x, g_mix, w, rope_h, rope_i, rope_k)
            o_p = _dsa_prompt(q, qi, kiwi, kb, vb, kib)
            qi32 = qi[P_ROWS:].reshape(DEC_BATCH, DEC_SEQ * N_IDX_HEADS, IDX_DIM)
            w32 = kiwi[P_ROWS:, IDX_DIM:IDX_DIM + N_IDX_HEADS].reshape(DEC_BATCH, DEC_SEQ * N_IDX_HEADS, 1)
            pad_rows = lambda a: jnp.pad(a.reshape(DEC_BATCH, DEC_SEQ, -1), ((0, 0), (0, LANES - DEC_SEQ), (0, 0)))
            sc = _sample_index_scores(page_table, qi32, w32, pad_rows(kib[P_ROWS:]), cki, j)
            bias = _sample_topk_bias(sc.reshape(S_ROWS, S_KEYS))
            bias16 = jnp.tile(bias.reshape(DEC_BATCH, DEC_SEQ, S_KEYS), (1, Q_PER_KV, 1))
            q16 = q[P_ROWS:].reshape(DEC_BATCH, DEC_SEQ, N_KV_HEADS, Q_PER_KV, HEAD_DIM)
            q16 = q16.transpose(0, 2, 3, 1, 4).reshape(DEC_BATCH, N_KV_HEADS, Q_PER_KV * DEC_SEQ, HEAD_DIM)
            o16 = _sample_attention(page_table, q16, bias16, pad_rows(kb[P_ROWS:]), pad_rows(vb[P_ROWS:]), ck, cv, j)
            o_s = o16.reshape(DEC_BATCH, N_KV_HEADS, Q_PER_KV, DEC_SEQ, HEAD_DIM).transpose(0, 3, 1, 2, 4)
            m = jnp.concatenate([o_p, o_s.reshape(S_ROWS, Q_W)], axis=0)
            outs["kp"].append(k[:P_ROWS].reshape(BATCH, SEQ, N_KV_HEADS, HEAD_DIM))
            outs["vp"].append(v[:P_ROWS].reshape(BATCH, SEQ, N_KV_HEADS, HEAD_DIM))
            outs["kip"].append(kiwi[:P_ROWS, :IDX_DIM].reshape(BATCH, SEQ, IDX_DIM))
            outs["ks"].append(k[P_ROWS:].reshape(DEC_BATCH, DEC_SEQ, N_KV_HEADS, HEAD_DIM))
            outs["vs"].append(v[P_ROWS:].reshape(DEC_BATCH, DEC_SEQ, N_KV_HEADS, HEAD_DIM))
            outs["kis"].append(kiwi[P_ROWS:, :IDX_DIM].reshape(DEC_BATCH, DEC_SEQ, IDX_DIM))
            w_out = b_w_out[j]
        x = _post(x, m, w_out.astype(bf16), norm_ffn[i].reshape(1, D_MODEL), ffn_w1[i].astype(bf16),
                  ffn_w2[i].astype(bf16), g_final, final=(i == DEPTH - 1))

    y_prompt = x[:P_ROWS].reshape(BATCH, SEQ, D_MODEL)
    y_sample = x[P_ROWS:].reshape(DEC_BATCH, DEC_SEQ, D_MODEL)
    st = jnp.stack
    return (y_prompt, y_sample, st(outs["kp"]), st(outs["vp"]), st(outs["kip"]), st(outs["ks"]), st(outs["vs"]),
            st(outs["kis"]), st(outs["cvp"]), st(outs["cvs"]))
```

```python
import functools

import jax
import jax.numpy as jnp
from jax import lax
from jax.experimental import pallas as pl
from jax.experimental.pallas import tpu as pltpu

D_MODEL = 1024
BATCH = 8
SEQ = 2048
DEPTH = 4
DEC_BATCH = 128
DEC_SEQ = 4
PAST_LEN = 2048
PAGE_SIZE = 128
N_PAGES = PAST_LEN // PAGE_SIZE
CHUNK = 128
D_A = D_MODEL
N_GROUPS_A = 8
GROUP_DIM_A = D_A // N_GROUPS_A
N_HEADS = 8
HEAD_DIM = D_MODEL // N_HEADS
N_KV_HEADS = 2
Q_PER_KV = N_HEADS // N_KV_HEADS
N_IDX_HEADS = 8
IDX_DIM = 64
TOPK_MAX = 256
QUERY_BLOCK = 128
ROPE_THETA = 500000.0
ROT_DIV = 4
IDX_SCALE = float((IDX_DIM * N_IDX_HEADS) ** -0.5)
ATTN_SCALE = float(HEAD_DIM ** -0.5)
Q_W = N_HEADS * HEAD_DIM
KV_W = N_KV_HEADS * HEAD_DIM
QI_W = N_IDX_HEADS * IDX_DIM
B_PROJ = Q_W + 2 * KV_W + QI_W + IDX_DIM + N_IDX_HEADS
D_FF = 4 * D_MODEL
EPS = 1e-6

LANES = 128
SUBLANES = 8
P_ROWS = BATCH * SEQ
S_ROWS = DEC_BATCH * DEC_SEQ
ROWS = P_ROWS + S_ROWS
TM = 512
N_TILES = ROWS // TM
P_TILES = P_ROWS // TM
TILES_PER_SEQ = SEQ // TM
CHUNKS_PER_TILE = TM // CHUNK
FF_TILE = 1024
B_PROJ_PAD = Q_W + 2 * KV_W + QI_W + LANES
KIWI_OFF = Q_W + 2 * KV_W + QI_W
S_KEYS = PAST_LEN + LANES
KV_SPAN = 512
VMEM_LIMIT = 56 * 1024 * 1024
INT_MIN = -(2 ** 31)
LOG2_E = 1.4426950408889634
BISECT_ROWS = 64

f32 = jnp.float32
bf16 = jnp.bfloat16

assert S_ROWS == TM and P_ROWS % TM == 0 and SEQ % TM == 0


def _params(*sem):
    return pltpu.CompilerParams(dimension_semantics=sem, vmem_limit_bytes=VMEM_LIMIT)


def _rms(x, g):
    return x * lax.rsqrt(jnp.mean(x * x, axis=-1, keepdims=True) + EPS) * g


def _gelu(x):
    c = 0.7978845608028654
    return 0.5 * x * (1.0 + jnp.tanh(c * (x + 0.044715 * (x * x * x))))


def _dot(a, b):
    return jnp.dot(a, b, preferred_element_type=f32)


def _dot_nt(a, b):
    return lax.dot_general(a, b, (((1,), (1,)), ((), ())), preferred_element_type=f32)


def _mixa_kernel(x_ref, g_ref, win_ref, gain_ref, ws_ref, bias_ref, scoef_ref, sbias_ref,
                 mp_ref, ms_ref, cvp_ref, cvs_ref):
    i = pl.program_id(0)
    h = _rms(x_ref[...], g_ref[...]).astype(bf16)
    uv = _gelu(_dot(h, win_ref[...]))
    u = uv[:, :D_A]
    v = uv[:, D_A:]
    vc = v - jnp.mean(v, axis=-1, keepdims=True)
    vn = vc * lax.rsqrt(jnp.mean(vc * vc, axis=-1, keepdims=True) + EPS) * gain_ref[...]

    @pl.when(i < P_TILES)
    def _prompt():
        t_idx = lax.broadcasted_iota(jnp.int32, (CHUNK, CHUNK), 0)
        s_idx = lax.broadcasted_iota(jnp.int32, (CHUNK, CHUNK), 1)
        causal = s_idx <= t_idx
        ws = [jnp.where(causal, ws_ref[g], 0.0).astype(bf16) for g in range(N_GROUPS_A)]
        for c in range(CHUNKS_PER_TILE):
            rows = slice(c * CHUNK, (c + 1) * CHUNK)
            v_c = vn[rows].astype(bf16)
            mixed = jnp.concatenate(
                [_dot(ws[g], v_c[:, g * GROUP_DIM_A:(g + 1) * GROUP_DIM_A]) for g in range(N_GROUPS_A)],
                axis=1) + bias_ref[...]
            mp_ref[rows, :] = (u[rows] * mixed).astype(bf16)

        @pl.when(i % TILES_PER_SEQ == TILES_PER_SEQ - 1)
        def _():
            cvp_ref[0] = vn[TM - CHUNK:]

    @pl.when(i == P_TILES)
    def _sample():
        grp = TM // SUBLANES
        v3 = vn.reshape(grp, SUBLANES, D_A)
        mixed = sbias_ref[...][None] + scoef_ref[0][None] * v3
        for d in range(1, DEC_SEQ):
            mixed = mixed + scoef_ref[d][None] * pltpu.roll(v3, d, 1)
        ms_ref[...] = (u.reshape(grp, SUBLANES, D_A) * mixed).reshape(TM, D_A).astype(bf16)
        cvs_ref[...] = vn


def _mixa(x, g, w_in, gain, ws, bias, scoef, sbias):
    const2 = lambda i: (0, 0)
    const3 = lambda i: (0, 0, 0)
    return pl.pallas_call(
        _mixa_kernel,
        grid=(N_TILES,),
        in_specs=[
            pl.BlockSpec((TM, D_MODEL), lambda i: (i, 0)),
            pl.BlockSpec((1, D_MODEL), const2),
            pl.BlockSpec((D_MODEL, 2 * D_A), const2),
            pl.BlockSpec((1, D_A), const2),
            pl.BlockSpec((N_GROUPS_A, CHUNK, CHUNK), const3),
            pl.BlockSpec((CHUNK, D_A), const2),
            pl.BlockSpec((DEC_SEQ, SUBLANES, D_A), const3),
            pl.BlockSpec((SUBLANES, D_A), const2),
        ],
        out_specs=[
            pl.BlockSpec((TM, D_A), lambda i: (jnp.minimum(i, P_TILES - 1), 0)),
            pl.BlockSpec((S_ROWS, D_A), const2),
            pl.BlockSpec((1, CHUNK, D_A), lambda i: (jnp.minimum(i // TILES_PER_SEQ, BATCH - 1), 0, 0)),
            pl.BlockSpec((S_ROWS, D_A), const2),
        ],
        out_shape=[
            jax.ShapeDtypeStruct((P_ROWS, D_A), bf16),
            jax.ShapeDtypeStruct((S_ROWS, D_A), bf16),
            jax.ShapeDtypeStruct((BATCH, CHUNK, D_A), f32),
            jax.ShapeDtypeStruct((S_ROWS, D_A), f32),
        ],
        compiler_params=_params("arbitrary"),
        name="mixer_a",
    )(x, g, w_in, gain, ws, bias, scoef, sbias)


def _post_kernel(x_ref, mp_ref, ms_ref, wout_ref, g_ref, w1_ref, w2_ref, gf_ref, *outs, final):
    i = pl.program_id(0)
    m = jnp.where(i == P_TILES, ms_ref[...], mp_ref[...])
    x1 = x_ref[...] + _dot(m, wout_ref[...])
    h = _rms(x1, g_ref[...]).astype(bf16)
    y = x1
    for c in range(D_FF // FF_TILE):
        a = jnp.maximum(_dot(h, w1_ref[:, c * FF_TILE:(c + 1) * FF_TILE]), 0.0)
        y = y + _dot((a * a).astype(bf16), w2_ref[c * FF_TILE:(c + 1) * FF_TILE, :])
    if not final:
        outs[0][...] = y
    else:
        y = _rms(y, gf_ref[...])
        yp_ref, ys_ref = outs

        @pl.when(i < P_TILES)
        def _():
            yp_ref[...] = y

        @pl.when(i == P_TILES)
        def _():
            ys_ref[...] = y


def _post(x, m_p, m_s, w_out, g, w1, w2, g_final, final):
    const2 = lambda i: (0, 0)
    resident = dict(pipeline_mode=pl.Buffered(1))
    row = lambda i: (i, 0)
    prow = lambda i: (jnp.minimum(i, P_TILES - 1), 0)
    if final:
        out_specs = [pl.BlockSpec((TM, D_MODEL), prow), pl.BlockSpec((S_ROWS, D_MODEL), const2)]
        out_shape = [jax.ShapeDtypeStruct((P_ROWS, D_MODEL), f32), jax.ShapeDtypeStruct((S_ROWS, D_MODEL), f32)]
    else:
        out_specs = pl.BlockSpec((TM, D_MODEL), row)
        out_shape = jax.ShapeDtypeStruct((ROWS, D_MODEL), f32)
    return pl.pallas_call(
        functools.partial(_post_kernel, final=final),
        grid=(N_TILES,),
        in_specs=[
            pl.BlockSpec((TM, D_MODEL), row),
            pl.BlockSpec((TM, D_MODEL), prow),
            pl.BlockSpec((S_ROWS, D_MODEL), const2),
            pl.BlockSpec((D_MODEL, D_MODEL), const2, **resident),
            pl.BlockSpec((1, D_MODEL), const2),
            pl.BlockSpec((D_MODEL, D_FF), const2, **resident),
            pl.BlockSpec((D_FF, D_MODEL), const2, **resident),
            pl.BlockSpec((1, D_MODEL), const2),
        ],
        out_specs=out_specs,
        out_shape=out_shape,
        compiler_params=_params("arbitrary"),
        name="post_ffn",
    )(x, m_p, m_s, w_out, g, w1, w2, g_final)


def _rope(x, cos, sin_lo, sin_hi, half):
    return x * cos + pltpu.roll(x, LANES - half, 1) * sin_lo + pltpu.roll(x, half, 1) * sin_hi


def _proj_kernel(x_ref, g_ref, w_ref, rh_ref, ri_ref, rk_ref,
                 q_ref, k_ref, v_ref, kb_ref, vb_ref, qi_ref, kiwi_ref, kib_ref, wit_ref):
    h = _rms(x_ref[...], g_ref[...]).astype(bf16)
    p = _dot(h, w_ref[...])
    half_h = HEAD_DIM // ROT_DIV // 2
    half_i = IDX_DIM // ROT_DIV // 2
    ch, slh, shh = rh_ref[0], rh_ref[1], rh_ref[2]
    for hd in range(N_HEADS):
        cols = slice(hd * HEAD_DIM, (hd + 1) * HEAD_DIM)
        q_ref[:, cols] = _rope(p[:, cols], ch, slh, shh, half_h).astype(bf16)
    for kh in range(N_KV_HEADS):
        cols = slice(kh * HEAD_DIM, (kh + 1) * HEAD_DIM)
        kr = _rope(p[:, Q_W + kh * HEAD_DIM:Q_W + (kh + 1) * HEAD_DIM], ch, slh, shh, half_h)
        k_ref[:, cols] = kr
        kb_ref[:, cols] = kr.astype(bf16)
    v = p[:, Q_W + KV_W:Q_W + 2 * KV_W]
    v_ref[...] = v
    one_hot = jnp.where(lax.broadcasted_iota(jnp.int32, (TM, HEAD_DIM), 1) == 0, 1.0, 0.0).astype(bf16)
    for kh in range(N_KV_HEADS):
        vb_ref[:, 2 * kh * HEAD_DIM:(2 * kh + 1) * HEAD_DIM] = v[:, kh * HEAD_DIM:(kh + 1) * HEAD_DIM].astype(bf16)
        vb_ref[:, (2 * kh + 1) * HEAD_DIM:(2 * kh + 2) * HEAD_DIM] = one_hot
    ci, sli, shi = ri_ref[0], ri_ref[1], ri_ref[2]
    off = Q_W + 2 * KV_W
    for j in range(QI_W // LANES):
        cols = slice(j * LANES, (j + 1) * LANES)
        qi_ref[:, cols] = _rope(p[:, off + j * LANES:off + (j + 1) * LANES], ci, sli, shi, half_i).astype(bf16)
    kiwi = _rope(p[:, KIWI_OFF:KIWI_OFF + LANES], rk_ref[0], rk_ref[1], rk_ref[2], half_i)
    kiwi_ref[...] = kiwi
    kib_ref[...] = kiwi[:, :IDX_DIM].astype(bf16)
    wit_ref[...] = kiwi.T[IDX_DIM:IDX_DIM + N_IDX_HEADS, :]


def _proj(x, g, w, rope_h, rope_i, rope_k):
    const2 = lambda i: (0, 0)
    rope_map = lambda i: (0, jnp.where(i < P_TILES, i % TILES_PER_SEQ, TILES_PER_SEQ), 0)
    row = lambda i: (i, 0)
    widths = [(Q_W, bf16), (KV_W, f32), (KV_W, f32), (KV_W, bf16), (2 * KV_W, bf16), (QI_W, bf16), (LANES, f32),
              (IDX_DIM, bf16)]
    return pl.pallas_call(
        _proj_kernel,
        grid=(N_TILES,),
        in_specs=[
            pl.BlockSpec((TM, D_MODEL), row),
            pl.BlockSpec((1, D_MODEL), const2),
            pl.BlockSpec((D_MODEL, B_PROJ_PAD), const2),
            pl.BlockSpec((3, TM, LANES), rope_map),
            pl.BlockSpec((3, TM, LANES), rope_map),
            pl.BlockSpec((3, TM, LANES), rope_map),
        ],
        out_specs=[pl.BlockSpec((TM, w_), row) for w_, _ in widths]
        + [pl.BlockSpec((N_IDX_HEADS, TM), lambda i: (0, i))],
        out_shape=[jax.ShapeDtypeStruct((ROWS, w_), dt) for w_, dt in widths]
        + [jax.ShapeDtypeStruct((N_IDX_HEADS, ROWS), f32)],
        compiler_params=_params("arbitrary"),
        name="dsa_proj",
    )(x, g, w, rope_h, rope_i, rope_k)


def _order_key(sc, valid):
    bits = pltpu.bitcast(sc, jnp.int32)
    return jnp.where(valid, bits ^ ((bits >> 31) & 0x7FFFFFFF), INT_MIN)


def _count0(mask):
    return jnp.sum(jnp.where(mask, 1.0, 0.0), axis=0, keepdims=True)


def _kth_largest(key_ref, n_keys, k):
    kf = float(k)

    def count_ge(cand):
        acc = jnp.zeros((BISECT_ROWS, LANES), f32)
        for c in range(n_keys // BISECT_ROWS):
            acc = jnp.where(key_ref[c * BISECT_ROWS:(c + 1) * BISECT_ROWS, :] >= cand, acc + 1.0, acc)
        return jnp.sum(acc, axis=0, keepdims=True)

    zero = jnp.zeros((1, LANES), jnp.int32)
    thr = jnp.where(count_ge(zero) >= kf, zero, zero + INT_MIN)

    def bit_step(j, thr):
        cand = thr | lax.shift_left(jnp.int32(1), 30 - j)
        return jnp.where(count_ge(cand) >= kf, cand, thr)

    return lax.fori_loop(0, 31, bit_step, thr)


def _topk_bias_t(key, valid, pos, thr, k):
    n_keys = key.shape[0]
    kf = float(k)
    above = key > thr
    tie = (key == thr) & valid
    need = kf - _count0(above)
    excess = jnp.max(_count0(tie) - need)
    n_bits = max(1, (n_keys - 1).bit_length())
    zero = jnp.zeros_like(thr)

    def first_ties():
        def step(j, p):
            cand = p + lax.shift_left(jnp.int32(1), n_bits - 1 - j)
            return jnp.where(_count0(tie & (pos < cand)) < need, cand, p)

        return lax.fori_loop(0, n_bits, step, zero)

    last = lax.cond(excess > 0.0, first_ties, lambda: zero + n_keys)
    sel = valid & (above | (tie & (pos <= last)))
    return jnp.where(sel, 0.0, -jnp.inf)


def _dsa_prompt_kernel(q_ref, qi_ref, wt_ref, k_ref, v_ref, ki_ref, o_ref, key_s, thr_s, bias_s):
    i = pl.program_id(1)
    k_top = min(TOPK_MAX, SEQ // 4)
    blocks_per_span = KV_SPAN // QUERY_BLOCK

    def run(span):
        n_keys = (span + 1) * KV_SPAN
        ki = ki_ref[:n_keys, :]
        sc = jnp.zeros((n_keys, QUERY_BLOCK), f32)
        for hd in range(N_IDX_HEADS):
            d = _dot_nt(ki, qi_ref[:, hd * IDX_DIM:(hd + 1) * IDX_DIM])
            sc = sc + jnp.maximum(d, 0.0) * wt_ref[hd:hd + 1, :]
        sc = sc * IDX_SCALE
        pos = lax.broadcasted_iota(jnp.int32, (n_keys, QUERY_BLOCK), 0)
        query = i * QUERY_BLOCK + lax.broadcasted_iota(jnp.int32, (n_keys, QUERY_BLOCK), 1)
        valid = pos <= query
        key_s[:n_keys, :] = _order_key(sc, valid)

        thr_s[...] = jnp.full(thr_s.shape, INT_MIN, jnp.int32)
        for blk in range(span * blocks_per_span, (span + 1) * blocks_per_span):
            if (blk + 1) * QUERY_BLOCK > k_top:
                @pl.when(i == blk)
                def _(blk=blk):
                    thr_s[0:1, :] = _kth_largest(key_s, (blk + 1) * QUERY_BLOCK, k_top)

        bias_t = _topk_bias_t(key_s[:n_keys, :], valid, pos, thr_s[0:1, :], k_top)
        bias_s[:, :n_keys] = bias_t.T
        bias = bias_s[:, :n_keys][None]
        for kh in range(N_KV_HEADS):
            q_rows = jnp.concatenate(
                [q_ref[:, (kh * Q_PER_KV + g) * HEAD_DIM:(kh * Q_PER_KV + g + 1) * HEAD_DIM]
                 for g in range(Q_PER_KV)], axis=0)
            s = _dot_nt(q_rows, k_ref[:n_keys, kh * HEAD_DIM:(kh + 1) * HEAD_DIM])
            s = (s.reshape(Q_PER_KV, QUERY_BLOCK, n_keys) + bias).reshape(Q_PER_KV * QUERY_BLOCK, n_keys)
            p = jnp.exp2((s - jnp.max(s, axis=1, keepdims=True)) * (ATTN_SCALE * LOG2_E))
            o = _dot(p.astype(bf16), v_ref[:n_keys, 2 * kh * HEAD_DIM:(2 * kh + 2) * HEAD_DIM])
            o = o[:, :HEAD_DIM] / o[:, HEAD_DIM:HEAD_DIM + 1]
            for g in range(Q_PER_KV):
                hd = kh * Q_PER_KV + g
                o_ref[:, hd * HEAD_DIM:(hd + 1) * HEAD_DIM] = o[g * QUERY_BLOCK:(g + 1) * QUERY_BLOCK].astype(bf16)

    for span in range(SEQ // KV_SPAN):
        pl.when(i // blocks_per_span == span)(functools.partial(run, span))


def _dsa_prompt(q, qi, wi_t, kb, vb, kib):
    nq = SEQ // QUERY_BLOCK
    qmap = lambda b, i: (b * nq + i, 0)
    kmap = lambda b, i: (b, 0)
    return pl.pallas_call(
        _dsa_prompt_kernel,
        grid=(BATCH, nq),
        in_specs=[
            pl.BlockSpec((QUERY_BLOCK, Q_W), qmap),
            pl.BlockSpec((QUERY_BLOCK, QI_W), qmap),
            pl.BlockSpec((N_IDX_HEADS, QUERY_BLOCK), lambda b, i: (0, b * nq + i)),
            pl.BlockSpec((SEQ, KV_W), kmap),
            pl.BlockSpec((SEQ, 2 * KV_W), kmap),
            pl.BlockSpec((SEQ, IDX_DIM), kmap),
        ],
        out_specs=pl.BlockSpec((QUERY_BLOCK, Q_W), qmap),
        out_shape=jax.ShapeDtypeStruct((P_ROWS, Q_W), bf16),
        scratch_shapes=[
            pltpu.VMEM((SEQ, QUERY_BLOCK), jnp.int32),
            pltpu.VMEM((SUBLANES, QUERY_BLOCK), jnp.int32),
            pltpu.VMEM((QUERY_BLOCK, SEQ), f32),
        ],
        compiler_params=_params("arbitrary", "arbitrary"),
        name="dsa_prompt",
    )(q, qi, wi_t, kb, vb, kib)


def _sidx_kernel(pt_ref, qi_ref, w_ref, kin_ref, *rest):
    pages, sc_ref = rest[:N_PAGES], rest[N_PAGES]
    qi = qi_ref[0]
    d = jnp.concatenate([_dot(qi, pg[0].astype(bf16)) for pg in pages] + [_dot_nt(qi, kin_ref[0])], axis=1)
    d = jnp.maximum(d, 0.0) * w_ref[0]
    sc = jnp.sum(d.reshape(DEC_SEQ, N_IDX_HEADS, S_KEYS), axis=1) * IDX_SCALE
    sc_ref[0] = sc


def _sample_index_scores(page_table, qi32, w32, ki_new, cache_kidx, layer):
    n_pool = cache_kidx.shape[0] // (DEPTH // 2)
    seq3 = lambda n, pt: (n, 0, 0)

    def page_map(p):
        return lambda n, pt: (layer * n_pool + pt[n, p], 0, 0)

    grid_spec = pltpu.PrefetchScalarGridSpec(
        num_scalar_prefetch=1,
        grid=(DEC_BATCH,),
        in_specs=[
            pl.BlockSpec((1, DEC_SEQ * N_IDX_HEADS, IDX_DIM), seq3),
            pl.BlockSpec((1, DEC_SEQ * N_IDX_HEADS, 1), seq3),
            pl.BlockSpec((1, LANES, IDX_DIM), seq3),
        ] + [pl.BlockSpec((1, IDX_DIM, PAGE_SIZE), page_map(p)) for p in range(N_PAGES)],
        out_specs=pl.BlockSpec((1, DEC_SEQ, S_KEYS), seq3),
    )
    return pl.pallas_call(
        _sidx_kernel,
        grid_spec=grid_spec,
        out_shape=jax.ShapeDtypeStruct((DEC_BATCH, DEC_SEQ, S_KEYS), f32),
        compiler_params=_params("arbitrary"),
        name="dsa_sample_index",
    )(page_table, qi32, w32, ki_new, *([cache_kidx] * N_PAGES))


def _stopk_kernel(sc_ref, bias_ref, key_s):
    shape = sc_ref.shape
    pos = lax.broadcasted_iota(jnp.int32, shape, 0)
    t = (pl.program_id(0) * LANES + lax.broadcasted_iota(jnp.int32, shape, 1)) % DEC_SEQ
    valid = pos <= t + PAST_LEN
    k_top = min(TOPK_MAX, (PAST_LEN + DEC_SEQ) // 4)
    key_s[...] = _order_key(sc_ref[...], valid)
    thr = _kth_largest(key_s, S_KEYS, k_top)
    bias_ref[...] = _topk_bias_t(key_s[...], valid, pos, thr, k_top)


def _sample_topk_bias(sc_t):
    spec = pl.BlockSpec((S_KEYS, LANES), lambda i: (0, i))
    return pl.pallas_call(
        _stopk_kernel,
        grid=(S_ROWS // LANES,),
        in_specs=[spec],
        out_specs=spec,
        out_shape=jax.ShapeDtypeStruct((S_KEYS, S_ROWS), f32),
        scratch_shapes=[pltpu.VMEM((S_KEYS, LANES), jnp.int32)],
        compiler_params=_params("arbitrary"),
        name="dsa_sample_topk",
    )(sc_t)


def _satt_kernel(pt_ref, q_ref, bias_ref, kn_ref, vn_ref, *rest):
    k_pages, v_pages, o_ref = rest[:N_PAGES], rest[N_PAGES:2 * N_PAGES], rest[2 * N_PAGES]
    bias = bias_ref[0]
    for kh in range(N_KV_HEADS):
        cols = slice(kh * HEAD_DIM, (kh + 1) * HEAD_DIM)
        head_rows = pl.ds(kh, PAGE_SIZE, stride=N_KV_HEADS)
        q = q_ref[0, kh]
        s = jnp.concatenate([_dot_nt(q, pg[0, head_rows, :].astype(bf16)) for pg in k_pages]
                            + [_dot_nt(q, kn_ref[0, :, cols])], axis=1)
        s = s * ATTN_SCALE + bias
        p = jnp.exp(s - jnp.max(s, axis=1, keepdims=True))
        denom = jnp.sum(p, axis=1, keepdims=True)
        p = p.astype(bf16)
        o = _dot(p[:, PAST_LEN:], vn_ref[0, :, cols])
        for j, pg in enumerate(v_pages):
            o = o + _dot(p[:, j * PAGE_SIZE:(j + 1) * PAGE_SIZE], pg[0, head_rows, :].astype(bf16))
        o_ref[0, kh] = (o / denom).astype(bf16)


def _sample_attention(page_table, q16, bias16, k_new, v_new, cache_k, cache_v, layer):
    n_pool = cache_k.shape[0] // (DEPTH // 2)
    rows = DEC_SEQ * Q_PER_KV
    seq3 = lambda n, pt: (n, 0, 0)
    seq4 = lambda n, pt: (n, 0, 0, 0)

    def page_map(p):
        return lambda n, pt: (layer * n_pool + pt[n, p], 0, 0)

    page_specs = [pl.BlockSpec((1, PAGE_SIZE * N_KV_HEADS, HEAD_DIM), page_map(p)) for p in range(N_PAGES)]
    grid_spec = pltpu.PrefetchScalarGridSpec(
        num_scalar_prefetch=1,
        grid=(DEC_BATCH,),
        in_specs=[
            pl.BlockSpec((1, N_KV_HEADS, rows, HEAD_DIM), seq4),
            pl.BlockSpec((1, rows, S_KEYS), seq3),
            pl.BlockSpec((1, LANES, KV_W), seq3),
            pl.BlockSpec((1, LANES, KV_W), seq3),
        ] + page_specs + page_specs,
        out_specs=pl.BlockSpec((1, N_KV_HEADS, rows, HEAD_DIM), seq4),
    )
    return pl.pallas_call(
        _satt_kernel,
        grid_spec=grid_spec,
        out_shape=jax.ShapeDtypeStruct((DEC_BATCH, N_KV_HEADS, rows, HEAD_DIM), bf16),
        compiler_params=_params("arbitrary"),
        name="dsa_sample_attn",
    )(page_table, q16, bias16, k_new, v_new, *([cache_k] * N_PAGES), *([cache_v] * N_PAGES))


def _rope_tables():
    pos = jnp.concatenate([jnp.arange(SEQ), PAST_LEN + jnp.arange(TM) % DEC_SEQ]).astype(f32)

    def pattern(dim, n_rot_groups):
        r = dim // ROT_DIV
        half = r // 2
        inv = ROPE_THETA ** (-jnp.arange(half, dtype=f32) * 2.0 / r)
        ang = pos[:, None] * inv[None, :]
        cos, sin = jnp.cos(ang), jnp.sin(ang)
        ones = jnp.ones((pos.shape[0], dim - r), f32)
        zeros = jnp.zeros((pos.shape[0], dim - r), f32)
        zh = jnp.zeros_like(sin)
        c = jnp.concatenate([cos, cos, ones], axis=1)
        lo = jnp.concatenate([-sin, zh, zeros], axis=1)
        hi = jnp.concatenate([zh, sin, zeros], axis=1)
        ident = (jnp.ones((pos.shape[0], dim), f32), jnp.zeros((pos.shape[0], dim), f32),
                 jnp.zeros((pos.shape[0], dim), f32))
        reps = LANES // dim
        parts = [(c, lo, hi) if j < n_rot_groups else ident for j in range(reps)]
        return jnp.stack([jnp.concatenate([p_[t] for p_ in parts], axis=1) for t in range(3)])

    return pattern(HEAD_DIM, 1), pattern(IDX_DIM, 2), pattern(IDX_DIM, 1)


def _sample_mix_tables(w_s, b_s):
    t = jnp.arange(SUBLANES) % DEC_SEQ
    coefs = []
    for d in range(DEC_SEQ):
        c = jnp.where((t - d >= 0)[None, :], w_s[:, t, jnp.maximum(t - d, 0)], 0.0)
        coefs.append(jnp.repeat(c.T, GROUP_DIM_A, axis=1))
    bias = jnp.repeat(b_s[:, t].T, GROUP_DIM_A, axis=1)
    return jnp.stack(coefs), bias


def kernel(x_prompt, x_sample, cache_k, cache_v, cache_kidx, page_table, norm_mix, norm_ffn,
           a_w_in, a_v_gain, a_w_s, a_b_s, a_w_out, b_w_in, b_w_out, ffn_w1, ffn_w2, norm_final):
    x = jnp.concatenate([x_prompt.reshape(P_ROWS, D_MODEL), x_sample.reshape(S_ROWS, D_MODEL)], axis=0)
    rope_h, rope_i, rope_k = _rope_tables()
    n_pool = cache_k.shape[1]
    ck = cache_k.reshape(-1, PAGE_SIZE * N_KV_HEADS, HEAD_DIM)
    cv = cache_v.reshape(-1, PAGE_SIZE * N_KV_HEADS, HEAD_DIM)
    cki = jnp.swapaxes(cache_kidx, 2, 3).reshape(-1, IDX_DIM, PAGE_SIZE)
    g_final = norm_final.reshape(1, D_MODEL)
    outs = {name: [] for name in ("kp", "vp", "kip", "ks", "vs", "kis", "cvp", "cvs")}

    for i in range(DEPTH):
        j = i // 2
        g_mix = norm_mix[i].reshape(1, D_MODEL)
        if i % 2 == 0:
            scoef, sbias = _sample_mix_tables(a_w_s[j], a_b_s[j])
            bias = jnp.repeat(a_b_s[j].T, GROUP_DIM_A, axis=1)
            m_p, m_s, cvp, cvs = _mixa(x, g_mix, a_w_in[j].astype(bf16), a_v_gain[j].reshape(1, D_A), a_w_s[j],
                                       bias, scoef, sbias)
            outs["cvp"].append(cvp)
            outs["cvs"].append(cvs.reshape(DEC_BATCH, DEC_SEQ, D_A))
            w_out = a_w_out[j]
        else:
            w = jnp.pad(b_w_in[j], ((0, 0), (0, B_PROJ_PAD - B_PROJ))).astype(bf16)
            q, k, v, kb, vb, qi, kiwi, kib, wi_t = _proj(x, g_mix, w, rope_h, rope_i, rope_k)
            m_p = _dsa_prompt(q, qi, wi_t, kb, vb, kib)
            qi32 = qi[P_ROWS:].reshape(DEC_BATCH, DEC_SEQ * N_IDX_HEADS, IDX_DIM)
            w32 = kiwi[P_ROWS:, IDX_DIM:IDX_DIM + N_IDX_HEADS].reshape(DEC_BATCH, DEC_SEQ * N_IDX_HEADS, 1)
            pad_rows = lambda a: jnp.pad(a.reshape(DEC_BATCH, DEC_SEQ, -1), ((0, 0), (0, LANES - DEC_SEQ), (0, 0)))
            sc = _sample_index_scores(page_table, qi32, w32, pad_rows(kib[P_ROWS:]), cki, j)
            bias = _sample_topk_bias(sc.reshape(S_ROWS, S_KEYS).T).T
            bias16 = jnp.tile(bias.reshape(DEC_BATCH, DEC_SEQ, S_KEYS), (1, Q_PER_KV, 1))
            q16 = q[P_ROWS:].reshape(DEC_BATCH, DEC_SEQ, N_KV_HEADS, Q_PER_KV, HEAD_DIM)
            q16 = q16.transpose(0, 2, 3, 1, 4).reshape(DEC_BATCH, N_KV_HEADS, Q_PER_KV * DEC_SEQ, HEAD_DIM)
            v_new = vb[P_ROWS:].reshape(S_ROWS, N_KV_HEADS, 2, HEAD_DIM)[:, :, 0]
            o16 = _sample_attention(page_table, q16, bias16, pad_rows(kb[P_ROWS:]), pad_rows(v_new), ck, cv, j)
            m_s = o16.reshape(DEC_BATCH, N_KV_HEADS, Q_PER_KV, DEC_SEQ, HEAD_DIM).transpose(0, 3, 1, 2, 4)
            m_s = m_s.reshape(S_ROWS, Q_W)
            outs["kp"].append(k[:P_ROWS].reshape(BATCH, SEQ, N_KV_HEADS, HEAD_DIM))
            outs["vp"].append(v[:P_ROWS].reshape(BATCH, SEQ, N_KV_HEADS, HEAD_DIM))
            outs["kip"].append(kiwi[:P_ROWS, :IDX_DIM].reshape(BATCH, SEQ, IDX_DIM))
            outs["ks"].append(k[P_ROWS:].reshape(DEC_BATCH, DEC_SEQ, N_KV_HEADS, HEAD_DIM))
            outs["vs"].append(v[P_ROWS:].reshape(DEC_BATCH, DEC_SEQ, N_KV_HEADS, HEAD_DIM))
            outs["kis"].append(kiwi[P_ROWS:, :IDX_DIM].reshape(DEC_BATCH, DEC_SEQ, IDX_DIM))
            w_out = b_w_out[j]
        x = _post(x, m_p, m_s, w_out.astype(bf16), norm_ffn[i].reshape(1, D_MODEL), ffn_w1[i].astype(bf16),
                  ffn_w2[i].astype(bf16), g_final, final=(i == DEPTH - 1))

    y_prompt = x[0].reshape(BATCH, SEQ, D_MODEL)
    y_sample = x[1].reshape(DEC_BATCH, DEC_SEQ, D_MODEL)
    st = jnp.stack
    return (y_prompt, y_sample, st(outs["kp"]), st(outs["vp"]), st(outs["kip"]), st(outs["ks"]), st(outs["vs"]),
            st(outs["kis"]), st(outs["cvp"]), st(outs["cvs"]))
```

```python
import functools

import jax
import jax.numpy as jnp
from jax import lax
from jax.experimental import pallas as pl
from jax.experimental.pallas import tpu as pltpu

D_MODEL = 1024
BATCH = 8
SEQ = 2048
DEPTH = 4
DEC_BATCH = 128
DEC_SEQ = 4
PAST_LEN = 2048
PAGE_SIZE = 128
N_PAGES = PAST_LEN // PAGE_SIZE
CHUNK = 128
D_A = D_MODEL
N_GROUPS_A = 8
GROUP_DIM_A = D_A // N_GROUPS_A
N_HEADS = 8
HEAD_DIM = D_MODEL // N_HEADS
N_KV_HEADS = 2
Q_PER_KV = N_HEADS // N_KV_HEADS
N_IDX_HEADS = 8
IDX_DIM = 64
TOPK_MAX = 256
QUERY_BLOCK = 128
ROPE_THETA = 500000.0
ROT_DIV = 4
IDX_SCALE = float((IDX_DIM * N_IDX_HEADS) ** -0.5)
ATTN_SCALE = float(HEAD_DIM ** -0.5)
Q_W = N_HEADS * HEAD_DIM
KV_W = N_KV_HEADS * HEAD_DIM
QI_W = N_IDX_HEADS * IDX_DIM
B_PROJ = Q_W + 2 * KV_W + QI_W + IDX_DIM + N_IDX_HEADS
D_FF = 4 * D_MODEL
EPS = 1e-6

LANES = 128
SUBLANES = 8
P_ROWS = BATCH * SEQ
S_ROWS = DEC_BATCH * DEC_SEQ
ROWS = P_ROWS + S_ROWS
TM = 512
N_TILES = ROWS // TM
P_TILES = P_ROWS // TM
TILES_PER_SEQ = SEQ // TM
CHUNKS_PER_TILE = TM // CHUNK
FF_TILE = 1024
B_PROJ_PAD = Q_W + 2 * KV_W + QI_W + LANES
KIWI_OFF = Q_W + 2 * KV_W + QI_W
S_KEYS = PAST_LEN + LANES
KV_SPAN = 256
VMEM_LIMIT = 56 * 1024 * 1024
INT_MIN = -(2 ** 31)
LOG2_E = 1.4426950408889634
BISECT_ROWS = 64

f32 = jnp.float32
bf16 = jnp.bfloat16

assert S_ROWS == TM and P_ROWS % TM == 0 and SEQ % TM == 0


def _params(*sem):
    return pltpu.CompilerParams(dimension_semantics=sem, vmem_limit_bytes=VMEM_LIMIT)


def _rms(x, g):
    return x * lax.rsqrt(jnp.mean(x * x, axis=-1, keepdims=True) + EPS) * g


def _gelu(x):
    a = -2.0 * LOG2_E * 0.7978845608028654
    return x / (1.0 + jnp.exp2(x * (a + (a * 0.044715) * (x * x))))


def _dot(a, b):
    return jnp.dot(a, b, preferred_element_type=f32)


def _dot_nt(a, b):
    return lax.dot_general(a, b, (((1,), (1,)), ((), ())), preferred_element_type=f32)


def _prompt_tile(i):
    return (jnp.minimum(i, P_TILES - 1), 0)


def _first_tile(i):
    return (0, 0)


def _pair_specs(width):
    return [pl.BlockSpec((TM, width), _prompt_tile), pl.BlockSpec((S_ROWS, width), _first_tile)]


def _pair_tile(i, p_ref, s_ref):
    return jnp.where(i == P_TILES, s_ref[...], p_ref[...])


def _mixa_kernel(xp_ref, xs_ref, g_ref, win_ref, gain_ref, ws_ref, bias_ref, scoef_ref, sbias_ref,
                 mp_ref, ms_ref, cvp_ref, cvs_ref):
    i = pl.program_id(0)
    h = _rms(_pair_tile(i, xp_ref, xs_ref), g_ref[...]).astype(bf16)
    uv = _gelu(_dot(h, win_ref[...]))
    u = uv[:, :D_A]
    v = uv[:, D_A:]
    vc = v - jnp.mean(v, axis=-1, keepdims=True)
    vn = vc * lax.rsqrt(jnp.mean(vc * vc, axis=-1, keepdims=True) + EPS) * gain_ref[...]

    @pl.when(i < P_TILES)
    def _prompt():
        t_idx = lax.broadcasted_iota(jnp.int32, (CHUNK, CHUNK), 0)
        s_idx = lax.broadcasted_iota(jnp.int32, (CHUNK, CHUNK), 1)
        causal = s_idx <= t_idx
        ws = [jnp.where(causal, ws_ref[g], 0.0).astype(bf16) for g in range(N_GROUPS_A)]
        for c in range(CHUNKS_PER_TILE):
            rows = slice(c * CHUNK, (c + 1) * CHUNK)
            v_c = vn[rows].astype(bf16)
            mixed = jnp.concatenate(
                [_dot(ws[g], v_c[:, g * GROUP_DIM_A:(g + 1) * GROUP_DIM_A]) for g in range(N_GROUPS_A)],
                axis=1) + bias_ref[...]
            mp_ref[rows, :] = (u[rows] * mixed).astype(bf16)

        @pl.when(i % TILES_PER_SEQ == TILES_PER_SEQ - 1)
        def _():
            cvp_ref[0] = vn[TM - CHUNK:]

    @pl.when(i == P_TILES)
    def _sample():
        grp = TM // SUBLANES
        v3 = vn.reshape(grp, SUBLANES, D_A)
        mixed = sbias_ref[...][None] + scoef_ref[0][None] * v3
        for d in range(1, DEC_SEQ):
            mixed = mixed + scoef_ref[d][None] * pltpu.roll(v3, d, 1)
        ms_ref[...] = (u.reshape(grp, SUBLANES, D_A) * mixed).reshape(TM, D_A).astype(bf16)
        cvs_ref[...] = vn


def _mixa(x, g, w_in, gain, ws, bias, scoef, sbias):
    const2 = lambda i: (0, 0)
    const3 = lambda i: (0, 0, 0)
    return pl.pallas_call(
        _mixa_kernel,
        grid=(N_TILES,),
        in_specs=_pair_specs(D_MODEL) + [
            pl.BlockSpec((1, D_MODEL), const2),
            pl.BlockSpec((D_MODEL, 2 * D_A), const2),
            pl.BlockSpec((1, D_A), const2),
            pl.BlockSpec((N_GROUPS_A, CHUNK, CHUNK), const3),
            pl.BlockSpec((CHUNK, D_A), const2),
            pl.BlockSpec((DEC_SEQ, SUBLANES, D_A), const3),
            pl.BlockSpec((SUBLANES, D_A), const2),
        ],
        out_specs=[
            pl.BlockSpec((TM, D_A), lambda i: (jnp.minimum(i, P_TILES - 1), 0)),
            pl.BlockSpec((S_ROWS, D_A), const2),
            pl.BlockSpec((1, CHUNK, D_A), lambda i: (jnp.minimum(i // TILES_PER_SEQ, BATCH - 1), 0, 0)),
            pl.BlockSpec((S_ROWS, D_A), const2),
        ],
        out_shape=[
            jax.ShapeDtypeStruct((P_ROWS, D_A), bf16),
            jax.ShapeDtypeStruct((S_ROWS, D_A), bf16),
            jax.ShapeDtypeStruct((BATCH, CHUNK, D_A), f32),
            jax.ShapeDtypeStruct((S_ROWS, D_A), f32),
        ],
        compiler_params=_params("arbitrary"),
        name="mixer_a",
    )(*x, g, w_in, gain, ws, bias, scoef, sbias)


def _post_kernel(xp_ref, xs_ref, mp_ref, ms_ref, wout_ref, g_ref, w1_ref, w2_ref, gf_ref, yp_ref, ys_ref, *, final):
    i = pl.program_id(0)
    x1 = _pair_tile(i, xp_ref, xs_ref) + _dot(_pair_tile(i, mp_ref, ms_ref), wout_ref[...])
    h = _rms(x1, g_ref[...]).astype(bf16)
    y = x1
    for c in range(D_FF // FF_TILE):
        a = jnp.maximum(_dot(h, w1_ref[:, c * FF_TILE:(c + 1) * FF_TILE]), 0.0)
        y = y + _dot((a * a).astype(bf16), w2_ref[c * FF_TILE:(c + 1) * FF_TILE, :])
    if final:
        y = _rms(y, gf_ref[...])

    @pl.when(i < P_TILES)
    def _():
        yp_ref[...] = y

    @pl.when(i == P_TILES)
    def _():
        ys_ref[...] = y


def _post(x, m, w_out, g, w1, w2, g_final, final):
    const2 = lambda i: (0, 0)
    resident = dict(pipeline_mode=pl.Buffered(1))
    return pl.pallas_call(
        functools.partial(_post_kernel, final=final),
        grid=(N_TILES,),
        in_specs=_pair_specs(D_MODEL) + _pair_specs(D_MODEL) + [
            pl.BlockSpec((D_MODEL, D_MODEL), const2, **resident),
            pl.BlockSpec((1, D_MODEL), const2),
            pl.BlockSpec((D_MODEL, D_FF), const2, **resident),
            pl.BlockSpec((D_FF, D_MODEL), const2, **resident),
            pl.BlockSpec((1, D_MODEL), const2),
        ],
        out_specs=_pair_specs(D_MODEL),
        out_shape=[jax.ShapeDtypeStruct((P_ROWS, D_MODEL), f32), jax.ShapeDtypeStruct((S_ROWS, D_MODEL), f32)],
        compiler_params=_params("arbitrary"),
        name="post_ffn",
    )(*x, *m, w_out, g, w1, w2, g_final)


def _rope(x, cos, sin_lo, sin_hi, half):
    return x * cos + pltpu.roll(x, LANES - half, 1) * sin_lo + pltpu.roll(x, half, 1) * sin_hi


def _proj_kernel(*refs, n_alias):
    xp_ref, xs_ref, g_ref, w_ref, rh_ref, ri_ref, rk_ref = refs[:7]
    (q_ref, kb_ref, vb_ref, qi_ref, kiwi_ref, kib_ref, wit_ref,
     kp_ref, ks_ref, vp_ref, vs_ref) = refs[7 + n_alias:]
    i = pl.program_id(0)
    h = _rms(_pair_tile(i, xp_ref, xs_ref), g_ref[...]).astype(bf16)
    p = _dot(h, w_ref[...])
    half_h = HEAD_DIM // ROT_DIV // 2
    half_i = IDX_DIM // ROT_DIV // 2
    ch, slh, shh = rh_ref[0], rh_ref[1], rh_ref[2]
    for hd in range(N_HEADS):
        cols = slice(hd * HEAD_DIM, (hd + 1) * HEAD_DIM)
        q_ref[:, cols] = _rope(p[:, cols], ch, slh, shh, half_h).astype(bf16)
    one_hot = jnp.where(lax.broadcasted_iota(jnp.int32, (TM, HEAD_DIM), 1) == 0, 1.0, 0.0).astype(bf16)
    for kh in range(N_KV_HEADS):
        cols = slice(kh * HEAD_DIM, (kh + 1) * HEAD_DIM)
        kr = _rope(p[:, Q_W + kh * HEAD_DIM:Q_W + (kh + 1) * HEAD_DIM], ch, slh, shh, half_h)
        v = p[:, Q_W + KV_W + kh * HEAD_DIM:Q_W + KV_W + (kh + 1) * HEAD_DIM]
        kb_ref[:, cols] = kr.astype(bf16)
        vb_ref[:, 2 * kh * HEAD_DIM:(2 * kh + 1) * HEAD_DIM] = v.astype(bf16)
        vb_ref[:, (2 * kh + 1) * HEAD_DIM:(2 * kh + 2) * HEAD_DIM] = one_hot
        head_rows = pl.ds(kh, TM, stride=N_KV_HEADS)

        @pl.when(i < P_TILES)
        def _():
            kp_ref[0, head_rows, :] = kr
            vp_ref[0, head_rows, :] = v

        @pl.when(i == P_TILES)
        def _():
            ks_ref[0, head_rows, :] = kr
            vs_ref[0, head_rows, :] = v
    ci, sli, shi = ri_ref[0], ri_ref[1], ri_ref[2]
    off = Q_W + 2 * KV_W
    for j in range(QI_W // LANES):
        cols = slice(j * LANES, (j + 1) * LANES)
        qi_ref[:, cols] = _rope(p[:, off + j * LANES:off + (j + 1) * LANES], ci, sli, shi, half_i).astype(bf16)
    kiwi = _rope(p[:, KIWI_OFF:KIWI_OFF + LANES], rk_ref[0], rk_ref[1], rk_ref[2], half_i)
    kiwi_ref[...] = kiwi
    kib_ref[...] = kiwi[:, :IDX_DIM].astype(bf16)
    wit_ref[...] = kiwi.T[IDX_DIM:IDX_DIM + N_IDX_HEADS, :]


def _proj(x, g, w, rope_h, rope_i, rope_k, layer, kv_prev):
    const2 = lambda i: (0, 0)
    rope_map = lambda i: (0, jnp.where(i < P_TILES, i % TILES_PER_SEQ, TILES_PER_SEQ), 0)
    row = lambda i: (i, 0)
    widths = [(Q_W, bf16), (KV_W, bf16), (2 * KV_W, bf16), (QI_W, bf16), (LANES, f32), (IDX_DIM, bf16)]
    n_layers = DEPTH // 2
    kv_rows = TM * N_KV_HEADS
    kv_specs = [pl.BlockSpec((1, kv_rows, HEAD_DIM), lambda i: (layer, jnp.minimum(i, P_TILES - 1), 0)),
                pl.BlockSpec((1, kv_rows, HEAD_DIM), lambda i: (layer, 0, 0))] * 2
    kv_shapes = [jax.ShapeDtypeStruct((n_layers, P_ROWS * N_KV_HEADS, HEAD_DIM), f32),
                 jax.ShapeDtypeStruct((n_layers, S_ROWS * N_KV_HEADS, HEAD_DIM), f32)] * 2
    n_in = 7
    n_out = len(widths) + 1
    alias_in = [] if kv_prev is None else list(kv_prev)
    return pl.pallas_call(
        functools.partial(_proj_kernel, n_alias=len(alias_in)),
        grid=(N_TILES,),
        in_specs=_pair_specs(D_MODEL) + [
            pl.BlockSpec((1, D_MODEL), const2),
            pl.BlockSpec((D_MODEL, B_PROJ_PAD), const2),
            pl.BlockSpec((3, TM, LANES), rope_map),
            pl.BlockSpec((3, TM, LANES), rope_map),
            pl.BlockSpec((3, TM, LANES), rope_map),
        ] + [pl.BlockSpec(memory_space=pl.ANY)] * len(alias_in),
        out_specs=[pl.BlockSpec((TM, w_), row) for w_, _ in widths]
        + [pl.BlockSpec((N_IDX_HEADS, TM), lambda i: (0, i))] + kv_specs,
        out_shape=[jax.ShapeDtypeStruct((ROWS, w_), dt) for w_, dt in widths]
        + [jax.ShapeDtypeStruct((N_IDX_HEADS, ROWS), f32)] + kv_shapes,
        input_output_aliases={n_in + a: n_out + a for a in range(len(alias_in))},
        compiler_params=_params("arbitrary"),
        name="dsa_proj",
    )(*x, g, w, rope_h, rope_i, rope_k, *alias_in)


def _order_key(sc, valid):
    bits = pltpu.bitcast(sc, jnp.int32)
    return jnp.where(valid, bits ^ ((bits >> 31) & 0x7FFFFFFF), INT_MIN)


def _count0(mask):
    return jnp.sum(jnp.where(mask, 1.0, 0.0), axis=0, keepdims=True)


def _kth_largest(key_ref, n_keys, k):
    kf = float(k)

    def count_ge(cand):
        acc = jnp.zeros((BISECT_ROWS, LANES), f32)
        for c in range(n_keys // BISECT_ROWS):
            acc = jnp.where(key_ref[c * BISECT_ROWS:(c + 1) * BISECT_ROWS, :] >= cand, acc + 1.0, acc)
        return jnp.sum(acc, axis=0, keepdims=True)

    zero = jnp.zeros((1, LANES), jnp.int32)
    thr = jnp.where(count_ge(zero) >= kf, zero, zero + INT_MIN)

    def bit_step(j, thr):
        cand = thr | lax.shift_left(jnp.int32(1), 30 - j)
        return jnp.where(count_ge(cand) >= kf, cand, thr)

    return lax.fori_loop(0, 31, bit_step, thr)


def _topk_bias_t(key, thr, k, out_ref):
    n_keys = key.shape[0]
    kf = float(k)
    thr = jnp.maximum(thr, INT_MIN + 1)
    ge = key >= thr
    excess = jnp.max(_count0(ge)) - kf

    @pl.when(excess <= 0.0)
    def _():
        out_ref[:n_keys, :] = jnp.where(ge, 0.0, -jnp.inf)

    @pl.when(excess > 0.0)
    def _():
        pos = lax.broadcasted_iota(jnp.int32, key.shape, 0)
        above = key > thr
        tie = key == thr
        need = kf - _count0(above)
        n_bits = max(1, (n_keys - 1).bit_length())

        def step(j, p):
            cand = p + lax.shift_left(jnp.int32(1), n_bits - 1 - j)
            return jnp.where(_count0(tie & (pos < cand)) < need, cand, p)

        last = lax.fori_loop(0, n_bits, step, jnp.zeros_like(thr))
        out_ref[:n_keys, :] = jnp.where(above | (tie & (pos <= last)), 0.0, -jnp.inf)


def _dsa_prompt_kernel(q_ref, qi_ref, wt_ref, k_ref, v_ref, ki_ref, o_ref, key_s, thr_s, biast_s, bias_s):
    i = pl.program_id(1)
    k_top = min(TOPK_MAX, SEQ // 4)
    blocks_per_span = KV_SPAN // QUERY_BLOCK

    def run(span):
        n_keys = (span + 1) * KV_SPAN
        ki = ki_ref[:n_keys, :]
        sc = jnp.zeros((n_keys, QUERY_BLOCK), f32)
        for hd in range(N_IDX_HEADS):
            d = _dot_nt(ki, qi_ref[:, hd * IDX_DIM:(hd + 1) * IDX_DIM])
            sc = sc + jnp.maximum(d, 0.0) * wt_ref[hd:hd + 1, :]
        sc = sc * IDX_SCALE
        pos = lax.broadcasted_iota(jnp.int32, (n_keys, QUERY_BLOCK), 0)
        query = i * QUERY_BLOCK + lax.broadcasted_iota(jnp.int32, (n_keys, QUERY_BLOCK), 1)
        valid = pos <= query
        key_s[:n_keys, :] = _order_key(sc, valid)

        thr_s[...] = jnp.full(thr_s.shape, INT_MIN, jnp.int32)
        for blk in range(span * blocks_per_span, (span + 1) * blocks_per_span):
            if (blk + 1) * QUERY_BLOCK > k_top:
                @pl.when(i == blk)
                def _(blk=blk):
                    thr_s[0:1, :] = _kth_largest(key_s, (blk + 1) * QUERY_BLOCK, k_top)

        _topk_bias_t(key_s[:n_keys, :], thr_s[0:1, :], k_top, biast_s)
        bias_s[:, :n_keys] = biast_s[:n_keys, :].T
        bias = bias_s[:, :n_keys][None]
        for kh in range(N_KV_HEADS):
            q_rows = jnp.concatenate(
                [q_ref[:, (kh * Q_PER_KV + g) * HEAD_DIM:(kh * Q_PER_KV + g + 1) * HEAD_DIM]
                 for g in range(Q_PER_KV)], axis=0)
            s = _dot_nt(q_rows, k_ref[:n_keys, kh * HEAD_DIM:(kh + 1) * HEAD_DIM])
            s = (s.reshape(Q_PER_KV, QUERY_BLOCK, n_keys) + bias).reshape(Q_PER_KV * QUERY_BLOCK, n_keys)
            p = jnp.exp2((s - jnp.max(s, axis=1, keepdims=True)) * (ATTN_SCALE * LOG2_E))
            o = _dot(p.astype(bf16), v_ref[:n_keys, 2 * kh * HEAD_DIM:(2 * kh + 2) * HEAD_DIM])
            o = o[:, :HEAD_DIM] / o[:, HEAD_DIM:HEAD_DIM + 1]
            for g in range(Q_PER_KV):
                hd = kh * Q_PER_KV + g
                o_ref[:, hd * HEAD_DIM:(hd + 1) * HEAD_DIM] = o[g * QUERY_BLOCK:(g + 1) * QUERY_BLOCK].astype(bf16)

    for span in range(SEQ // KV_SPAN):
        pl.when(i // blocks_per_span == span)(functools.partial(run, span))


def _dsa_prompt(q, qi, wi_t, kb, vb, kib):
    nq = SEQ // QUERY_BLOCK
    qmap = lambda b, i: (b * nq + i, 0)
    kmap = lambda b, i: (b, 0)
    return pl.pallas_call(
        _dsa_prompt_kernel,
        grid=(BATCH, nq),
        in_specs=[
            pl.BlockSpec((QUERY_BLOCK, Q_W), qmap),
            pl.BlockSpec((QUERY_BLOCK, QI_W), qmap),
            pl.BlockSpec((N_IDX_HEADS, QUERY_BLOCK), lambda b, i: (0, b * nq + i)),
            pl.BlockSpec((SEQ, KV_W), kmap),
            pl.BlockSpec((SEQ, 2 * KV_W), kmap),
            pl.BlockSpec((SEQ, IDX_DIM), kmap),
        ],
        out_specs=pl.BlockSpec((QUERY_BLOCK, Q_W), qmap),
        out_shape=jax.ShapeDtypeStruct((P_ROWS, Q_W), bf16),
        scratch_shapes=[
            pltpu.VMEM((SEQ, QUERY_BLOCK), jnp.int32),
            pltpu.VMEM((SUBLANES, QUERY_BLOCK), jnp.int32),
            pltpu.VMEM((SEQ, QUERY_BLOCK), f32),
            pltpu.VMEM((QUERY_BLOCK, SEQ), f32),
        ],
        compiler_params=_params("arbitrary", "arbitrary"),
        name="dsa_prompt",
    )(q, qi, wi_t, kb, vb, kib)


def _sidx_kernel(pt_ref, qi_ref, w_ref, kin_ref, *rest):
    pages, sc_ref = rest[:N_PAGES], rest[N_PAGES]
    qi = qi_ref[0]
    d = jnp.concatenate([_dot(qi, pg[0].astype(bf16)) for pg in pages] + [_dot_nt(qi, kin_ref[0])], axis=1)
    d = jnp.maximum(d, 0.0) * w_ref[0]
    sc = jnp.sum(d.reshape(DEC_SEQ, N_IDX_HEADS, S_KEYS), axis=1) * IDX_SCALE
    sc_ref[0] = sc


def _sample_index_scores(page_table, qi32, w32, ki_new, cache_kidx, layer):
    n_pool = cache_kidx.shape[0] // (DEPTH // 2)
    seq3 = lambda n, pt: (n, 0, 0)

    def page_map(p):
        return lambda n, pt: (layer * n_pool + pt[n, p], 0, 0)

    grid_spec = pltpu.PrefetchScalarGridSpec(
        num_scalar_prefetch=1,
        grid=(DEC_BATCH,),
        in_specs=[
            pl.BlockSpec((1, DEC_SEQ * N_IDX_HEADS, IDX_DIM), seq3),
            pl.BlockSpec((1, DEC_SEQ * N_IDX_HEADS, 1), seq3),
            pl.BlockSpec((1, LANES, IDX_DIM), seq3),
        ] + [pl.BlockSpec((1, IDX_DIM, PAGE_SIZE), page_map(p)) for p in range(N_PAGES)],
        out_specs=pl.BlockSpec((1, DEC_SEQ, S_KEYS), seq3),
    )
    return pl.pallas_call(
        _sidx_kernel,
        grid_spec=grid_spec,
        out_shape=jax.ShapeDtypeStruct((DEC_BATCH, DEC_SEQ, S_KEYS), f32),
        compiler_params=_params("arbitrary"),
        name="dsa_sample_index",
    )(page_table, qi32, w32, ki_new, *([cache_kidx] * N_PAGES))


def _stopk_kernel(sc_ref, bias_ref, key_s):
    shape = sc_ref.shape
    pos = lax.broadcasted_iota(jnp.int32, shape, 0)
    t = (pl.program_id(0) * LANES + lax.broadcasted_iota(jnp.int32, shape, 1)) % DEC_SEQ
    valid = pos <= t + PAST_LEN
    k_top = min(TOPK_MAX, (PAST_LEN + DEC_SEQ) // 4)
    key_s[...] = _order_key(sc_ref[...], valid)
    thr = _kth_largest(key_s, S_KEYS, k_top)
    _topk_bias_t(key_s[...], thr, k_top, bias_ref)


def _sample_topk_bias(sc_t):
    spec = pl.BlockSpec((S_KEYS, LANES), lambda i: (0, i))
    return pl.pallas_call(
        _stopk_kernel,
        grid=(S_ROWS // LANES,),
        in_specs=[spec],
        out_specs=spec,
        out_shape=jax.ShapeDtypeStruct((S_KEYS, S_ROWS), f32),
        scratch_shapes=[pltpu.VMEM((S_KEYS, LANES), jnp.int32)],
        compiler_params=_params("arbitrary"),
        name="dsa_sample_topk",
    )(sc_t)


def _satt_kernel(pt_ref, q_ref, bias_ref, kn_ref, vn_ref, *rest):
    k_pages, v_pages = rest[:N_PAGES], rest[N_PAGES:2 * N_PAGES]
    o_ref, kcat_s, vcat_s = rest[2 * N_PAGES:]
    bias = bias_ref[0]
    for kh in range(N_KV_HEADS):
        cols = slice(kh * HEAD_DIM, (kh + 1) * HEAD_DIM)
        head_rows = pl.ds(kh, PAGE_SIZE, stride=N_KV_HEADS)
        for j in range(N_PAGES):
            rows = slice(j * PAGE_SIZE, (j + 1) * PAGE_SIZE)
            kcat_s[kh, rows, :] = k_pages[j][0, head_rows, :].astype(bf16)
            vcat_s[kh, rows, :] = v_pages[j][0, head_rows, :].astype(bf16)
        kcat_s[kh, PAST_LEN:, :] = kn_ref[0, :, cols]
        vcat_s[kh, PAST_LEN:, :] = vn_ref[0, :, cols]
    for kh in range(N_KV_HEADS):
        s = _dot_nt(q_ref[0, kh], kcat_s[kh]) + bias
        p = jnp.exp2((s - jnp.max(s, axis=1, keepdims=True)) * (ATTN_SCALE * LOG2_E))
        denom = jnp.sum(p, axis=1, keepdims=True)
        o_ref[0, kh] = (_dot(p.astype(bf16), vcat_s[kh]) / denom).astype(bf16)


def _sample_attention(page_table, q16, bias16, k_new, v_new, cache_k, cache_v, layer):
    n_pool = cache_k.shape[0] // (DEPTH // 2)
    rows = DEC_SEQ * Q_PER_KV
    seq3 = lambda n, pt: (n, 0, 0)
    seq4 = lambda n, pt: (n, 0, 0, 0)

    def page_map(p):
        return lambda n, pt: (layer * n_pool + pt[n, p], 0, 0)

    page_specs = [pl.BlockSpec((1, PAGE_SIZE * N_KV_HEADS, HEAD_DIM), page_map(p)) for p in range(N_PAGES)]
    grid_spec = pltpu.PrefetchScalarGridSpec(
        num_scalar_prefetch=1,
        grid=(DEC_BATCH,),
        in_specs=[
            pl.BlockSpec((1, N_KV_HEADS, rows, HEAD_DIM), seq4),
            pl.BlockSpec((1, rows, S_KEYS), seq3),
            pl.BlockSpec((1, LANES, KV_W), seq3),
            pl.BlockSpec((1, LANES, KV_W), seq3),
        ] + page_specs + page_specs,
        out_specs=pl.BlockSpec((1, N_KV_HEADS, rows, HEAD_DIM), seq4),
        scratch_shapes=[pltpu.VMEM((N_KV_HEADS, S_KEYS, HEAD_DIM), bf16)] * 2,
    )
    return pl.pallas_call(
        _satt_kernel,
        grid_spec=grid_spec,
        out_shape=jax.ShapeDtypeStruct((DEC_BATCH, N_KV_HEADS, rows, HEAD_DIM), bf16),
        compiler_params=_params("arbitrary"),
        name="dsa_sample_attn",
    )(page_table, q16, bias16, k_new, v_new, *([cache_k] * N_PAGES), *([cache_v] * N_PAGES))


def _rope_tables():
    pos = jnp.concatenate([jnp.arange(SEQ), PAST_LEN + jnp.arange(TM) % DEC_SEQ]).astype(f32)

    def pattern(dim, n_rot_groups):
        r = dim // ROT_DIV
        half = r // 2
        inv = ROPE_THETA ** (-jnp.arange(half, dtype=f32) * 2.0 / r)
        ang = pos[:, None] * inv[None, :]
        cos, sin = jnp.cos(ang), jnp.sin(ang)
        ones = jnp.ones((pos.shape[0], dim - r), f32)
        zeros = jnp.zeros((pos.shape[0], dim - r), f32)
        zh = jnp.zeros_like(sin)
        c = jnp.concatenate([cos, cos, ones], axis=1)
        lo = jnp.concatenate([-sin, zh, zeros], axis=1)
        hi = jnp.concatenate([zh, sin, zeros], axis=1)
        ident = (jnp.ones((pos.shape[0], dim), f32), jnp.zeros((pos.shape[0], dim), f32),
                 jnp.zeros((pos.shape[0], dim), f32))
        reps = LANES // dim
        parts = [(c, lo, hi) if j < n_rot_groups else ident for j in range(reps)]
        return jnp.stack([jnp.concatenate([p_[t] for p_ in parts], axis=1) for t in range(3)])

    return pattern(HEAD_DIM, 1), pattern(IDX_DIM, 2), pattern(IDX_DIM, 1)


def _sample_mix_tables(w_s, b_s):
    t = jnp.arange(SUBLANES) % DEC_SEQ
    coefs = []
    for d in range(DEC_SEQ):
        c = jnp.where((t - d >= 0)[None, :], w_s[:, t, jnp.maximum(t - d, 0)], 0.0)
        coefs.append(jnp.repeat(c.T, GROUP_DIM_A, axis=1))
    bias = jnp.repeat(b_s[:, t].T, GROUP_DIM_A, axis=1)
    return jnp.stack(coefs), bias


def kernel(x_prompt, x_sample, cache_k, cache_v, cache_kidx, page_table, norm_mix, norm_ffn,
           a_w_in, a_v_gain, a_w_s, a_b_s, a_w_out, b_w_in, b_w_out, ffn_w1, ffn_w2, norm_final):
    x = (x_prompt.reshape(P_ROWS, D_MODEL), x_sample.reshape(S_ROWS, D_MODEL))
    kv_new = None
    rope_h, rope_i, rope_k = _rope_tables()
    n_pool = cache_k.shape[1]
    ck = cache_k.reshape(-1, PAGE_SIZE * N_KV_HEADS, HEAD_DIM)
    cv = cache_v.reshape(-1, PAGE_SIZE * N_KV_HEADS, HEAD_DIM)
    cki = jnp.swapaxes(cache_kidx, 2, 3).reshape(-1, IDX_DIM, PAGE_SIZE)
    g_final = norm_final.reshape(1, D_MODEL)
    outs = {name: [] for name in ("kip", "kis", "cvp", "cvs")}

    for i in range(DEPTH):
        j = i // 2
        g_mix = norm_mix[i].reshape(1, D_MODEL)
        if i % 2 == 0:
            scoef, sbias = _sample_mix_tables(a_w_s[j], a_b_s[j])
            bias = jnp.repeat(a_b_s[j].T, GROUP_DIM_A, axis=1)
            *m, cvp, cvs = _mixa(x, g_mix, a_w_in[j].astype(bf16), a_v_gain[j].reshape(1, D_A), a_w_s[j],
                                 bias, scoef, sbias)
            outs["cvp"].append(cvp)
            outs["cvs"].append(cvs.reshape(DEC_BATCH, DEC_SEQ, D_A))
            w_out = a_w_out[j]
        else:
            w = jnp.pad(b_w_in[j], ((0, 0), (0, B_PROJ_PAD - B_PROJ))).astype(bf16)
            q, kb, vb, qi, kiwi, kib, wi_t, *kv_new = _proj(x, g_mix, w, rope_h, rope_i, rope_k, j, kv_new)
            m_p = _dsa_prompt(q, qi, wi_t, kb, vb, kib)
            qi32 = qi[P_ROWS:].reshape(DEC_BATCH, DEC_SEQ * N_IDX_HEADS, IDX_DIM)
            w32 = kiwi[P_ROWS:, IDX_DIM:IDX_DIM + N_IDX_HEADS].reshape(DEC_BATCH, DEC_SEQ * N_IDX_HEADS, 1)
            pad_rows = lambda a: jnp.pad(a.reshape(DEC_BATCH, DEC_SEQ, -1), ((0, 0), (0, LANES - DEC_SEQ), (0, 0)))
            sc = _sample_index_scores(page_table, qi32, w32, pad_rows(kib[P_ROWS:]), cki, j)
            bias = _sample_topk_bias(sc.reshape(S_ROWS, S_KEYS).T).T
            bias16 = jnp.tile(bias.reshape(DEC_BATCH, DEC_SEQ, S_KEYS), (1, Q_PER_KV, 1))
            q16 = q[P_ROWS:].reshape(DEC_BATCH, DEC_SEQ, N_KV_HEADS, Q_PER_KV, HEAD_DIM)
            q16 = q16.transpose(0, 2, 3, 1, 4).reshape(DEC_BATCH, N_KV_HEADS, Q_PER_KV * DEC_SEQ, HEAD_DIM)
            v_new = vb[P_ROWS:].reshape(S_ROWS, N_KV_HEADS, 2, HEAD_DIM)[:, :, 0]
            o16 = _sample_attention(page_table, q16, bias16, pad_rows(kb[P_ROWS:]), pad_rows(v_new), ck, cv, j)
            m_s = o16.reshape(DEC_BATCH, N_KV_HEADS, Q_PER_KV, DEC_SEQ, HEAD_DIM).transpose(0, 3, 1, 2, 4)
            m = (m_p, m_s.reshape(S_ROWS, Q_W))
            outs["kip"].append(kiwi[:P_ROWS, :IDX_DIM].reshape(BATCH, SEQ, IDX_DIM))
            outs["kis"].append(kiwi[P_ROWS:, :IDX_DIM].reshape(DEC_BATCH, DEC_SEQ, IDX_DIM))
            w_out = b_w_out[j]
        x = _post(x, m, w_out.astype(bf16), norm_ffn[i].reshape(1, D_MODEL), ffn_w1[i].astype(bf16),
                  ffn_w2[i].astype(bf16), g_final, final=(i == DEPTH - 1))

    y_prompt = x[0].reshape(BATCH, SEQ, D_MODEL)
    y_sample = x[1].reshape(DEC_BATCH, DEC_SEQ, D_MODEL)
    n_layers = DEPTH // 2
    kp, ks, vp, vs = kv_new
    prompt_kv = lambda a: a.reshape(n_layers, BATCH, SEQ, N_KV_HEADS, HEAD_DIM)
    sample_kv = lambda a: a.reshape(n_layers, DEC_BATCH, DEC_SEQ, N_KV_HEADS, HEAD_DIM)
    st = jnp.stack
    return (y_prompt, y_sample, prompt_kv(kp), prompt_kv(vp), st(outs["kip"]), sample_kv(ks), sample_kv(vs),
            st(outs["kis"]), st(outs["cvp"]), st(outs["cvs"]))
```

```python
import functools

import jax
import jax.numpy as jnp
from jax import lax
from jax.experimental import pallas as pl
from jax.experimental.pallas import tpu as pltpu

D_MODEL = 1024
BATCH = 8
SEQ = 2048
DEPTH = 4
DEC_BATCH = 128
DEC_SEQ = 4
PAST_LEN = 2048
PAGE_SIZE = 128
N_PAGES = PAST_LEN // PAGE_SIZE
CHUNK = 128
D_A = D_MODEL
N_GROUPS_A = 8
GROUP_DIM_A = D_A // N_GROUPS_A
N_HEADS = 8
HEAD_DIM = D_MODEL // N_HEADS
N_KV_HEADS = 2
Q_PER_KV = N_HEADS // N_KV_HEADS
N_IDX_HEADS = 8
IDX_DIM = 64
TOPK_MAX = 256
QUERY_BLOCK = 128
ROPE_THETA = 500000.0
ROT_DIV = 4
IDX_SCALE = float((IDX_DIM * N_IDX_HEADS) ** -0.5)
ATTN_SCALE = float(HEAD_DIM ** -0.5)
Q_W = N_HEADS * HEAD_DIM
KV_W = N_KV_HEADS * HEAD_DIM
QI_W = N_IDX_HEADS * IDX_DIM
B_PROJ = Q_W + 2 * KV_W + QI_W + IDX_DIM + N_IDX_HEADS
D_FF = 4 * D_MODEL
EPS = 1e-6

LANES = 128
SUBLANES = 8
P_ROWS = BATCH * SEQ
S_ROWS = DEC_BATCH * DEC_SEQ
ROWS = P_ROWS + S_ROWS
TM = 512
N_TILES = ROWS // TM
P_TILES = P_ROWS // TM
TILES_PER_SEQ = SEQ // TM
CHUNKS_PER_TILE = TM // CHUNK
FF_TILE = 1024
B_PROJ_PAD = Q_W + 2 * KV_W + QI_W + LANES
KIWI_OFF = Q_W + 2 * KV_W + QI_W
S_KEYS = PAST_LEN + LANES
KV_SPAN = 256
VMEM_LIMIT = 56 * 1024 * 1024
INT_MIN = -(2 ** 31)
LOG2_E = 1.4426950408889634
SATT_SEQS = 2
SIDX_SEQS = 4
BISECT_ROWS = 64

f32 = jnp.float32
bf16 = jnp.bfloat16

assert S_ROWS == TM and P_ROWS % TM == 0 and SEQ % TM == 0


def _params(*sem):
    return pltpu.CompilerParams(dimension_semantics=sem, vmem_limit_bytes=VMEM_LIMIT)


def _rms(x, g):
    return x * lax.rsqrt(jnp.mean(x * x, axis=-1, keepdims=True) + EPS) * g


def _gelu(x):
    a = -2.0 * LOG2_E * 0.7978845608028654
    return x / (1.0 + jnp.exp2(x * (a + (a * 0.044715) * (x * x))))


def _dot(a, b):
    return jnp.dot(a, b, preferred_element_type=f32)


def _dot_nt(a, b):
    return lax.dot_general(a, b, (((1,), (1,)), ((), ())), preferred_element_type=f32)


def _prompt_tile(i):
    return (jnp.minimum(i, P_TILES - 1), 0)


def _first_tile(i):
    return (0, 0)


def _pair_specs(width):
    return [pl.BlockSpec((TM, width), _prompt_tile), pl.BlockSpec((S_ROWS, width), _first_tile)]


def _rows_specs(n_arrays, width):
    return [pl.BlockSpec((TM, width), lambda i: (i, 0))] if n_arrays == 1 else _pair_specs(width)


def _rows_tile(i, refs):
    if len(refs) == 1:
        return refs[0][...]
    p_ref, s_ref = refs
    return jnp.where(i == P_TILES, s_ref[...], p_ref[...])


def _store_rows_tile(i, refs, value):
    if len(refs) == 1:
        refs[0][...] = value
        return
    p_ref, s_ref = refs

    @pl.when(i < P_TILES)
    def _():
        p_ref[...] = value

    @pl.when(i == P_TILES)
    def _():
        s_ref[...] = value


def _mixa_kernel(*refs, n_x):
    x_refs = refs[:n_x]
    g_ref, win_ref, gain_ref, ws_ref, bias_ref, scoef_ref, sbias_ref, m_ref, cvp_ref, cvs_ref = refs[n_x:]
    i = pl.program_id(0)
    h = _rms(_rows_tile(i, x_refs), g_ref[...]).astype(bf16)
    uv = _gelu(_dot(h, win_ref[...]))
    u = uv[:, :D_A]
    v = uv[:, D_A:]
    vc = v - jnp.mean(v, axis=-1, keepdims=True)
    vn = vc * lax.rsqrt(jnp.mean(vc * vc, axis=-1, keepdims=True) + EPS) * gain_ref[...]

    @pl.when(i < P_TILES)
    def _prompt():
        t_idx = lax.broadcasted_iota(jnp.int32, (CHUNK, CHUNK), 0)
        s_idx = lax.broadcasted_iota(jnp.int32, (CHUNK, CHUNK), 1)
        causal = s_idx <= t_idx
        ws = [jnp.where(causal, ws_ref[g], 0.0).astype(bf16) for g in range(N_GROUPS_A)]
        for c in range(CHUNKS_PER_TILE):
            rows = slice(c * CHUNK, (c + 1) * CHUNK)
            v_c = vn[rows].astype(bf16)
            mixed = jnp.concatenate(
                [_dot(ws[g], v_c[:, g * GROUP_DIM_A:(g + 1) * GROUP_DIM_A]) for g in range(N_GROUPS_A)],
                axis=1) + bias_ref[...]
            m_ref[rows, :] = (u[rows] * mixed).astype(bf16)

        @pl.when(i % TILES_PER_SEQ == TILES_PER_SEQ - 1)
        def _():
            cvp_ref[0] = vn[TM - CHUNK:]

    @pl.when(i == P_TILES)
    def _sample():
        grp = TM // SUBLANES
        v3 = vn.reshape(grp, SUBLANES, D_A)
        mixed = sbias_ref[...][None] + scoef_ref[0][None] * v3
        for d in range(1, DEC_SEQ):
            mixed = mixed + scoef_ref[d][None] * pltpu.roll(v3, d, 1)
        m_ref[...] = (u.reshape(grp, SUBLANES, D_A) * mixed).reshape(TM, D_A).astype(bf16)
        cvs_ref[...] = vn


def _mixa(x, g, w_in, gain, ws, bias, scoef, sbias):
    const2 = lambda i: (0, 0)
    const3 = lambda i: (0, 0, 0)
    return pl.pallas_call(
        functools.partial(_mixa_kernel, n_x=len(x)),
        grid=(N_TILES,),
        in_specs=_rows_specs(len(x), D_MODEL) + [
            pl.BlockSpec((1, D_MODEL), const2),
            pl.BlockSpec((D_MODEL, 2 * D_A), const2),
            pl.BlockSpec((1, D_A), const2),
            pl.BlockSpec((N_GROUPS_A, CHUNK, CHUNK), const3),
            pl.BlockSpec((CHUNK, D_A), const2),
            pl.BlockSpec((DEC_SEQ, SUBLANES, D_A), const3),
            pl.BlockSpec((SUBLANES, D_A), const2),
        ],
        out_specs=[
            pl.BlockSpec((TM, D_A), lambda i: (i, 0)),
            pl.BlockSpec((1, CHUNK, D_A), lambda i: (jnp.minimum(i // TILES_PER_SEQ, BATCH - 1), 0, 0)),
            pl.BlockSpec((S_ROWS, D_A), const2),
        ],
        out_shape=[
            jax.ShapeDtypeStruct((ROWS, D_A), bf16),
            jax.ShapeDtypeStruct((BATCH, CHUNK, D_A), f32),
            jax.ShapeDtypeStruct((S_ROWS, D_A), f32),
        ],
        compiler_params=_params("arbitrary"),
        name="mixer_a",
    )(*x, g, w_in, gain, ws, bias, scoef, sbias)


def _post_kernel(*refs, n_x, n_m, final):
    x_refs, m_refs = refs[:n_x], refs[n_x:n_x + n_m]
    wout_ref, g_ref, w1_ref, w2_ref, gf_ref = refs[n_x + n_m:n_x + n_m + 5]
    y_refs = refs[n_x + n_m + 5:]
    i = pl.program_id(0)
    x1 = _rows_tile(i, x_refs) + _dot(_rows_tile(i, m_refs), wout_ref[...])
    h = _rms(x1, g_ref[...]).astype(bf16)
    y = x1
    for c in range(D_FF // FF_TILE):
        a = jnp.maximum(_dot(h, w1_ref[:, c * FF_TILE:(c + 1) * FF_TILE]), 0.0)
        y = y + _dot((a * a).astype(bf16), w2_ref[c * FF_TILE:(c + 1) * FF_TILE, :])
    if final:
        y = _rms(y, gf_ref[...])
    _store_rows_tile(i, y_refs, y)


def _post(x, m, w_out, g, w1, w2, g_final, final):
    const2 = lambda i: (0, 0)
    resident = dict(pipeline_mode=pl.Buffered(1))
    if final:
        out_specs = _pair_specs(D_MODEL)
        out_shape = [jax.ShapeDtypeStruct((P_ROWS, D_MODEL), f32), jax.ShapeDtypeStruct((S_ROWS, D_MODEL), f32)]
    else:
        out_specs = _rows_specs(1, D_MODEL)
        out_shape = [jax.ShapeDtypeStruct((ROWS, D_MODEL), f32)]
    return pl.pallas_call(
        functools.partial(_post_kernel, n_x=len(x), n_m=len(m), final=final),
        grid=(N_TILES,),
        in_specs=_rows_specs(len(x), D_MODEL) + _rows_specs(len(m), D_MODEL) + [
            pl.BlockSpec((D_MODEL, D_MODEL), const2, **resident),
            pl.BlockSpec((1, D_MODEL), const2),
            pl.BlockSpec((D_MODEL, D_FF), const2, **resident),
            pl.BlockSpec((D_FF, D_MODEL), const2, **resident),
            pl.BlockSpec((1, D_MODEL), const2),
        ],
        out_specs=out_specs,
        out_shape=out_shape,
        compiler_params=_params("arbitrary"),
        name="post_ffn",
    )(*x, *m, w_out, g, w1, w2, g_final)


def _rope(x, cos, sin_lo, sin_hi, half):
    return x * cos + pltpu.roll(x, LANES - half, 1) * sin_lo + pltpu.roll(x, half, 1) * sin_hi


def _proj_kernel(*refs, n_alias):
    x_ref, g_ref, w_ref, rh_ref, ri_ref, rk_ref = refs[:6]
    (q_ref, kb_ref, vb_ref, qi_ref, kiwi_ref, kib_ref, wit_ref,
     kp_ref, ks_ref, vp_ref, vs_ref) = refs[6 + n_alias:]
    i = pl.program_id(0)
    h = _rms(x_ref[...], g_ref[...]).astype(bf16)
    p = _dot(h, w_ref[...])
    half_h = HEAD_DIM // ROT_DIV // 2
    half_i = IDX_DIM // ROT_DIV // 2
    ch, slh, shh = rh_ref[0], rh_ref[1], rh_ref[2]
    for hd in range(N_HEADS):
        cols = slice(hd * HEAD_DIM, (hd + 1) * HEAD_DIM)
        q_ref[:, cols] = _rope(p[:, cols], ch, slh, shh, half_h).astype(bf16)
    one_hot = jnp.where(lax.broadcasted_iota(jnp.int32, (TM, HEAD_DIM), 1) == 0, 1.0, 0.0).astype(bf16)
    for kh in range(N_KV_HEADS):
        cols = slice(kh * HEAD_DIM, (kh + 1) * HEAD_DIM)
        kr = _rope(p[:, Q_W + kh * HEAD_DIM:Q_W + (kh + 1) * HEAD_DIM], ch, slh, shh, half_h)
        v = p[:, Q_W + KV_W + kh * HEAD_DIM:Q_W + KV_W + (kh + 1) * HEAD_DIM]
        kb_ref[:, cols] = kr.astype(bf16)
        vb_ref[:, 2 * kh * HEAD_DIM:(2 * kh + 1) * HEAD_DIM] = v.astype(bf16)
        vb_ref[:, (2 * kh + 1) * HEAD_DIM:(2 * kh + 2) * HEAD_DIM] = one_hot
        head_rows = pl.ds(kh, TM, stride=N_KV_HEADS)

        for slot in range(kp_ref.shape[0]):
            @pl.when(i < P_TILES)
            def _(slot=slot):
                kp_ref[slot, head_rows, :] = kr
                vp_ref[slot, head_rows, :] = v

            @pl.when(i == P_TILES)
            def _(slot=slot):
                ks_ref[slot, head_rows, :] = kr
                vs_ref[slot, head_rows, :] = v
    ci, sli, shi = ri_ref[0], ri_ref[1], ri_ref[2]
    off = Q_W + 2 * KV_W
    for j in range(QI_W // LANES):
        cols = slice(j * LANES, (j + 1) * LANES)
        qi_ref[:, cols] = _rope(p[:, off + j * LANES:off + (j + 1) * LANES], ci, sli, shi, half_i).astype(bf16)
    kiwi = _rope(p[:, KIWI_OFF:KIWI_OFF + LANES], rk_ref[0], rk_ref[1], rk_ref[2], half_i)
    kiwi_ref[...] = kiwi
    kib_ref[...] = kiwi[:, :IDX_DIM].astype(bf16)
    wit_ref[...] = kiwi.T[IDX_DIM:IDX_DIM + N_IDX_HEADS, :]


def _proj(x, g, w, rope_h, rope_i, rope_k, layer, kv_prev):
    const2 = lambda i: (0, 0)
    rope_map = lambda i: (0, jnp.where(i < P_TILES, i % TILES_PER_SEQ, TILES_PER_SEQ), 0)
    row = lambda i: (i, 0)
    widths = [(Q_W, bf16), (KV_W, bf16), (2 * KV_W, bf16), (QI_W, bf16), (LANES, f32), (IDX_DIM, bf16)]
    n_layers = DEPTH // 2
    kv_rows = TM * N_KV_HEADS
    slots, first = (n_layers, 0) if kv_prev is None else (1, layer)
    kv_specs = [pl.BlockSpec((slots, kv_rows, HEAD_DIM), lambda i: (first, jnp.minimum(i, P_TILES - 1), 0)),
                pl.BlockSpec((slots, kv_rows, HEAD_DIM), lambda i: (first, 0, 0))] * 2
    kv_shapes = [jax.ShapeDtypeStruct((n_layers, P_ROWS * N_KV_HEADS, HEAD_DIM), f32),
                 jax.ShapeDtypeStruct((n_layers, S_ROWS * N_KV_HEADS, HEAD_DIM), f32)] * 2
    n_in = 6
    n_out = len(widths) + 1
    alias_in = [] if kv_prev is None else list(kv_prev)
    return pl.pallas_call(
        functools.partial(_proj_kernel, n_alias=len(alias_in)),
        grid=(N_TILES,),
        in_specs=_rows_specs(1, D_MODEL) + [
            pl.BlockSpec((1, D_MODEL), const2),
            pl.BlockSpec((D_MODEL, B_PROJ_PAD), const2),
            pl.BlockSpec((3, TM, LANES), rope_map),
            pl.BlockSpec((3, TM, LANES), rope_map),
            pl.BlockSpec((3, TM, LANES), rope_map),
        ] + [pl.BlockSpec(memory_space=pl.ANY)] * len(alias_in),
        out_specs=[pl.BlockSpec((TM, w_), row) for w_, _ in widths]
        + [pl.BlockSpec((N_IDX_HEADS, TM), lambda i: (0, i))] + kv_specs,
        out_shape=[jax.ShapeDtypeStruct((ROWS, w_), dt) for w_, dt in widths]
        + [jax.ShapeDtypeStruct((N_IDX_HEADS, ROWS), f32)] + kv_shapes,
        input_output_aliases={n_in + a: n_out + a for a in range(len(alias_in))},
        compiler_params=_params("arbitrary"),
        name="dsa_proj",
    )(x, g, w, rope_h, rope_i, rope_k, *alias_in)


def _order_key(sc, valid):
    bits = pltpu.bitcast(sc, jnp.int32)
    return jnp.where(valid, bits ^ ((bits >> 31) & 0x7FFFFFFF), INT_MIN)


def _count0(mask):
    return jnp.sum(jnp.where(mask, 1.0, 0.0), axis=0, keepdims=True)


def _kth_largest(key_ref, n_keys, k):
    kf = float(k)

    def count_ge(cand):
        acc = jnp.zeros((BISECT_ROWS, LANES), f32)
        for c in range(n_keys // BISECT_ROWS):
            acc = jnp.where(key_ref[c * BISECT_ROWS:(c + 1) * BISECT_ROWS, :] >= cand, acc + 1.0, acc)
        return jnp.sum(acc, axis=0, keepdims=True)

    zero = jnp.zeros((1, LANES), jnp.int32)
    thr = jnp.where(count_ge(zero) >= kf, zero, zero + INT_MIN)

    def bit_step(j, thr):
        cand = thr | lax.shift_left(jnp.int32(1), 30 - j)
        return jnp.where(count_ge(cand) >= kf, cand, thr)

    return lax.fori_loop(0, 31, bit_step, thr)


def _topk_bias_t(key, thr, k, out_ref):
    n_keys = key.shape[0]
    kf = float(k)
    thr = jnp.maximum(thr, INT_MIN + 1)
    ge = key >= thr
    excess = jnp.max(_count0(ge)) - kf

    @pl.when(excess <= 0.0)
    def _():
        out_ref[:n_keys, :] = jnp.where(ge, 0.0, -jnp.inf)

    @pl.when(excess > 0.0)
    def _():
        pos = lax.broadcasted_iota(jnp.int32, key.shape, 0)
        above = key > thr
        tie = key == thr
        need = kf - _count0(above)
        n_bits = max(1, (n_keys - 1).bit_length())

        def step(j, p):
            cand = p + lax.shift_left(jnp.int32(1), n_bits - 1 - j)
            return jnp.where(_count0(tie & (pos < cand)) < need, cand, p)

        last = lax.fori_loop(0, n_bits, step, jnp.zeros_like(thr))
        out_ref[:n_keys, :] = jnp.where(above | (tie & (pos <= last)), 0.0, -jnp.inf)


def _dsa_prompt_kernel(q_ref, qi_ref, wt_ref, k_ref, v_ref, ki_ref, o_ref, key_s, thr_s, biast_s, bias_s):
    i = pl.program_id(1)
    k_top = min(TOPK_MAX, SEQ // 4)
    blocks_per_span = KV_SPAN // QUERY_BLOCK

    def run(span):
        n_keys = (span + 1) * KV_SPAN
        ki = ki_ref[:n_keys, :]
        sc = jnp.zeros((n_keys, QUERY_BLOCK), f32)
        for hd in range(N_IDX_HEADS):
            d = _dot_nt(ki, qi_ref[:, hd * IDX_DIM:(hd + 1) * IDX_DIM])
            sc = sc + jnp.maximum(d, 0.0) * wt_ref[hd:hd + 1, :]
        sc = sc * IDX_SCALE
        pos = lax.broadcasted_iota(jnp.int32, (n_keys, QUERY_BLOCK), 0)
        query = i * QUERY_BLOCK + lax.broadcasted_iota(jnp.int32, (n_keys, QUERY_BLOCK), 1)
        valid = pos <= query
        key_s[:n_keys, :] = _order_key(sc, valid)

        thr_s[...] = jnp.full(thr_s.shape, INT_MIN, jnp.int32)
        for blk in range(span * blocks_per_span, (span + 1) * blocks_per_span):
            if (blk + 1) * QUERY_BLOCK > k_top:
                @pl.when(i == blk)
                def _(blk=blk):
                    thr_s[0:1, :] = _kth_largest(key_s, (blk + 1) * QUERY_BLOCK, k_top)

        _topk_bias_t(key_s[:n_keys, :], thr_s[0:1, :], k_top, biast_s)
        bias_s[:, :n_keys] = biast_s[:n_keys, :].T
        bias = bias_s[:, :n_keys][None]
        for kh in range(N_KV_HEADS):
            q_rows = jnp.concatenate(
                [q_ref[:, (kh * Q_PER_KV + g) * HEAD_DIM:(kh * Q_PER_KV + g + 1) * HEAD_DIM]
                 for g in range(Q_PER_KV)], axis=0)
            s = _dot_nt(q_rows, k_ref[:n_keys, kh * HEAD_DIM:(kh + 1) * HEAD_DIM])
            s = (s.reshape(Q_PER_KV, QUERY_BLOCK, n_keys) + bias).reshape(Q_PER_KV * QUERY_BLOCK, n_keys)
            p = jnp.exp2((s - jnp.max(s, axis=1, keepdims=True)) * (ATTN_SCALE * LOG2_E))
            o = _dot(p.astype(bf16), v_ref[:n_keys, 2 * kh * HEAD_DIM:(2 * kh + 2) * HEAD_DIM])
            o = o[:, :HEAD_DIM] / o[:, HEAD_DIM:HEAD_DIM + 1]
            for g in range(Q_PER_KV):
                hd = kh * Q_PER_KV + g
                o_ref[:, hd * HEAD_DIM:(hd + 1) * HEAD_DIM] = o[g * QUERY_BLOCK:(g + 1) * QUERY_BLOCK].astype(bf16)

    for span in range(SEQ // KV_SPAN):
        pl.when(i // blocks_per_span == span)(functools.partial(run, span))


def _dsa_prompt(q, qi, wi_t, kb, vb, kib):
    nq = SEQ // QUERY_BLOCK
    qmap = lambda b, i: (b * nq + i, 0)
    kmap = lambda b, i: (b, 0)
    return pl.pallas_call(
        _dsa_prompt_kernel,
        grid=(BATCH, nq),
        in_specs=[
            pl.BlockSpec((QUERY_BLOCK, Q_W), qmap),
            pl.BlockSpec((QUERY_BLOCK, QI_W), qmap),
            pl.BlockSpec((N_IDX_HEADS, QUERY_BLOCK), lambda b, i: (0, b * nq + i)),
            pl.BlockSpec((SEQ, KV_W), kmap),
            pl.BlockSpec((SEQ, 2 * KV_W), kmap),
            pl.BlockSpec((SEQ, IDX_DIM), kmap),
        ],
        out_specs=pl.BlockSpec((QUERY_BLOCK, Q_W), qmap),
        out_shape=jax.ShapeDtypeStruct((P_ROWS, Q_W), bf16),
        scratch_shapes=[
            pltpu.VMEM((SEQ, QUERY_BLOCK), jnp.int32),
            pltpu.VMEM((SUBLANES, QUERY_BLOCK), jnp.int32),
            pltpu.VMEM((SEQ, QUERY_BLOCK), f32),
            pltpu.VMEM((QUERY_BLOCK, SEQ), f32),
        ],
        compiler_params=_params("arbitrary", "arbitrary"),
        name="dsa_prompt",
    )(q, qi, wi_t, kb, vb, kib)


def _sidx_kernel(pt_ref, qi_ref, w_ref, kin_ref, cki_hbm, sc_ref, pbuf, sem, *, first_page):
    step = pl.program_id(0)
    slot = step % 2

    def page_copies(st, sl):
        return [pltpu.make_async_copy(cki_hbm.at[first_page + pt_ref[st * SIDX_SEQS + g, j]], pbuf.at[sl, g, j],
                                      sem.at[sl])
                for g in range(SIDX_SEQS) for j in range(N_PAGES)]

    @pl.when(step == 0)
    def _():
        for c in page_copies(0, 0):
            c.start()

    @pl.when(step + 1 < pl.num_programs(0))
    def _():
        for c in page_copies(step + 1, 1 - slot):
            c.start()

    for c in page_copies(step, slot):
        c.wait()

    for g in range(SIDX_SEQS):
        qi = qi_ref[g]
        d = jnp.concatenate([_dot(qi, pbuf[slot, g, j].astype(bf16)) for j in range(N_PAGES)]
                            + [_dot_nt(qi, kin_ref[g])], axis=1)
        d = jnp.maximum(d, 0.0) * w_ref[g]
        sc_ref[g] = jnp.sum(d.reshape(DEC_SEQ, N_IDX_HEADS, S_KEYS), axis=1) * IDX_SCALE


def _sample_index_scores(page_table, qi32, w32, ki_new, cache_kidx, layer):
    n_pool = cache_kidx.shape[0] // (DEPTH // 2)
    seq3 = lambda n, pt: (n, 0, 0)
    grid_spec = pltpu.PrefetchScalarGridSpec(
        num_scalar_prefetch=1,
        grid=(DEC_BATCH // SIDX_SEQS,),
        in_specs=[
            pl.BlockSpec((SIDX_SEQS, DEC_SEQ * N_IDX_HEADS, IDX_DIM), seq3),
            pl.BlockSpec((SIDX_SEQS, DEC_SEQ * N_IDX_HEADS, 1), seq3),
            pl.BlockSpec((SIDX_SEQS, LANES, IDX_DIM), seq3),
            pl.BlockSpec(memory_space=pl.ANY),
        ],
        out_specs=pl.BlockSpec((SIDX_SEQS, DEC_SEQ, S_KEYS), seq3),
        scratch_shapes=[
            pltpu.VMEM((2, SIDX_SEQS, N_PAGES, IDX_DIM, PAGE_SIZE), f32),
            pltpu.SemaphoreType.DMA((2,)),
        ],
    )
    return pl.pallas_call(
        functools.partial(_sidx_kernel, first_page=layer * n_pool),
        grid_spec=grid_spec,
        out_shape=jax.ShapeDtypeStruct((DEC_BATCH, DEC_SEQ, S_KEYS), f32),
        compiler_params=_params("arbitrary"),
        name="dsa_sample_index",
    )(page_table, qi32, w32, ki_new, cache_kidx)


def _stopk_kernel(sc_ref, bias_ref, key_s):
    shape = sc_ref.shape
    pos = lax.broadcasted_iota(jnp.int32, shape, 0)
    t = (pl.program_id(0) * LANES + lax.broadcasted_iota(jnp.int32, shape, 1)) % DEC_SEQ
    valid = pos <= t + PAST_LEN
    k_top = min(TOPK_MAX, (PAST_LEN + DEC_SEQ) // 4)
    key_s[...] = _order_key(sc_ref[...], valid)
    thr = _kth_largest(key_s, S_KEYS, k_top)
    _topk_bias_t(key_s[...], thr, k_top, bias_ref)


def _sample_topk_bias(sc_t):
    spec = pl.BlockSpec((S_KEYS, LANES), lambda i: (0, i))
    return pl.pallas_call(
        _stopk_kernel,
        grid=(S_ROWS // LANES,),
        in_specs=[spec],
        out_specs=spec,
        out_shape=jax.ShapeDtypeStruct((S_KEYS, S_ROWS), f32),
        scratch_shapes=[pltpu.VMEM((S_KEYS, LANES), jnp.int32)],
        compiler_params=_params("arbitrary"),
        name="dsa_sample_topk",
    )(sc_t)


def _satt_kernel(pt_ref, q_ref, bias_ref, kn_ref, vn_ref, ck_hbm, cv_hbm, o_ref,
                 kbuf, vbuf, sem, *, first_page):
    step = pl.program_id(0)
    slot = step % 2

    def page_copies(st, sl):
        copies = []
        for g in range(SATT_SEQS):
            for j in range(N_PAGES):
                page = first_page + pt_ref[st * SATT_SEQS + g, j]
                copies.append(pltpu.make_async_copy(ck_hbm.at[page], kbuf.at[sl, g, j], sem.at[sl, 0]))
                copies.append(pltpu.make_async_copy(cv_hbm.at[page], vbuf.at[sl, g, j], sem.at[sl, 1]))
        return copies

    @pl.when(step == 0)
    def _():
        for c in page_copies(0, 0):
            c.start()

    @pl.when(step + 1 < pl.num_programs(0))
    def _():
        for c in page_copies(step + 1, 1 - slot):
            c.start()

    for c in page_copies(step, slot):
        c.wait()

    for g in range(SATT_SEQS):
        for kh in range(N_KV_HEADS):
            cols = slice(kh * HEAD_DIM, (kh + 1) * HEAD_DIM)
            head_rows = pl.ds(kh, PAGE_SIZE, stride=N_KV_HEADS)
            k_all = jnp.concatenate([kbuf[slot, g, j, head_rows, :].astype(bf16) for j in range(N_PAGES)]
                                    + [kn_ref[g, :, cols]], axis=0)
            v_all = jnp.concatenate([vbuf[slot, g, j, head_rows, :].astype(bf16) for j in range(N_PAGES)]
                                    + [vn_ref[g, :, cols]], axis=0)
            s = _dot_nt(q_ref[g, kh], k_all) + bias_ref[g]
            p = jnp.exp2((s - jnp.max(s, axis=1, keepdims=True)) * (ATTN_SCALE * LOG2_E))
            denom = jnp.sum(p, axis=1, keepdims=True)
            o_ref[g, kh] = (_dot(p.astype(bf16), v_all) / denom).astype(bf16)


def _sample_attention(page_table, q16, bias16, k_new, v_new, cache_k, cache_v, layer):
    n_pool = cache_k.shape[0] // (DEPTH // 2)
    rows = DEC_SEQ * Q_PER_KV
    seq3 = lambda n, pt: (n, 0, 0)
    seq4 = lambda n, pt: (n, 0, 0, 0)
    page_shape = (PAGE_SIZE * N_KV_HEADS, HEAD_DIM)
    grid_spec = pltpu.PrefetchScalarGridSpec(
        num_scalar_prefetch=1,
        grid=(DEC_BATCH // SATT_SEQS,),
        in_specs=[
            pl.BlockSpec((SATT_SEQS, N_KV_HEADS, rows, HEAD_DIM), seq4),
            pl.BlockSpec((SATT_SEQS, rows, S_KEYS), seq3),
            pl.BlockSpec((SATT_SEQS, LANES, KV_W), seq3),
            pl.BlockSpec((SATT_SEQS, LANES, KV_W), seq3),
            pl.BlockSpec(memory_space=pl.ANY),
            pl.BlockSpec(memory_space=pl.ANY),
        ],
        out_specs=pl.BlockSpec((SATT_SEQS, N_KV_HEADS, rows, HEAD_DIM), seq4),
        scratch_shapes=[
            pltpu.VMEM((2, SATT_SEQS, N_PAGES) + page_shape, f32),
            pltpu.VMEM((2, SATT_SEQS, N_PAGES) + page_shape, f32),
            pltpu.SemaphoreType.DMA((2, 2)),
        ],
    )
    return pl.pallas_call(
        functools.partial(_satt_kernel, first_page=layer * n_pool),
        grid_spec=grid_spec,
        out_shape=jax.ShapeDtypeStruct((DEC_BATCH, N_KV_HEADS, rows, HEAD_DIM), bf16),
        compiler_params=_params("arbitrary"),
        name="dsa_sample_attn",
    )(page_table, q16, bias16, k_new, v_new, cache_k, cache_v)


def _rope_tables():
    pos = jnp.concatenate([jnp.arange(SEQ), PAST_LEN + jnp.arange(TM) % DEC_SEQ]).astype(f32)

    def pattern(dim, n_rot_groups):
        r = dim // ROT_DIV
        half = r // 2
        inv = ROPE_THETA ** (-jnp.arange(half, dtype=f32) * 2.0 / r)
        ang = pos[:, None] * inv[None, :]
        cos, sin = jnp.cos(ang), jnp.sin(ang)
        ones = jnp.ones((pos.shape[0], dim - r), f32)
        zeros = jnp.zeros((pos.shape[0], dim - r), f32)
        zh = jnp.zeros_like(sin)
        c = jnp.concatenate([cos, cos, ones], axis=1)
        lo = jnp.concatenate([-sin, zh, zeros], axis=1)
        hi = jnp.concatenate([zh, sin, zeros], axis=1)
        ident = (jnp.ones((pos.shape[0], dim), f32), jnp.zeros((pos.shape[0], dim), f32),
                 jnp.zeros((pos.shape[0], dim), f32))
        reps = LANES // dim
        parts = [(c, lo, hi) if j < n_rot_groups else ident for j in range(reps)]
        return jnp.stack([jnp.concatenate([p_[t] for p_ in parts], axis=1) for t in range(3)])

    return pattern(HEAD_DIM, 1), pattern(IDX_DIM, 2), pattern(IDX_DIM, 1)


def _sample_mix_tables(w_s, b_s):
    t = jnp.arange(SUBLANES) % DEC_SEQ
    coefs = []
    for d in range(DEC_SEQ):
        c = jnp.where((t - d >= 0)[None, :], w_s[:, t, jnp.maximum(t - d, 0)], 0.0)
        coefs.append(jnp.repeat(c.T, GROUP_DIM_A, axis=1))
    bias = jnp.repeat(b_s[:, t].T, GROUP_DIM_A, axis=1)
    return jnp.stack(coefs), bias


def kernel(x_prompt, x_sample, cache_k, cache_v, cache_kidx, page_table, norm_mix, norm_ffn,
           a_w_in, a_v_gain, a_w_s, a_b_s, a_w_out, b_w_in, b_w_out, ffn_w1, ffn_w2, norm_final):
    x = (x_prompt.reshape(P_ROWS, D_MODEL), x_sample.reshape(S_ROWS, D_MODEL))
    kv_new = None
    rope_h, rope_i, rope_k = _rope_tables()
    n_pool = cache_k.shape[1]
    ck = cache_k.reshape(-1, PAGE_SIZE * N_KV_HEADS, HEAD_DIM)
    cv = cache_v.reshape(-1, PAGE_SIZE * N_KV_HEADS, HEAD_DIM)
    cki = jnp.swapaxes(cache_kidx, 2, 3).reshape(-1, IDX_DIM, PAGE_SIZE)
    g_final = norm_final.reshape(1, D_MODEL)
    outs = {name: [] for name in ("kip", "kis", "cvp", "cvs")}

    for i in range(DEPTH):
        j = i // 2
        g_mix = norm_mix[i].reshape(1, D_MODEL)
        if i % 2 == 0:
            scoef, sbias = _sample_mix_tables(a_w_s[j], a_b_s[j])
            bias = jnp.repeat(a_b_s[j].T, GROUP_DIM_A, axis=1)
            *m, cvp, cvs = _mixa(x, g_mix, a_w_in[j].astype(bf16), a_v_gain[j].reshape(1, D_A), a_w_s[j],
                                 bias, scoef, sbias)
            outs["cvp"].append(cvp)
            outs["cvs"].append(cvs.reshape(DEC_BATCH, DEC_SEQ, D_A))
            w_out = a_w_out[j]
        else:
            w = jnp.pad(b_w_in[j], ((0, 0), (0, B_PROJ_PAD - B_PROJ))).astype(bf16)
            q, kb, vb, qi, kiwi, kib, wi_t, *kv_new = _proj(x[0], g_mix, w, rope_h, rope_i, rope_k, j, kv_new)
            m_p = _dsa_prompt(q, qi, wi_t, kb, vb, kib)
            qi32 = qi[P_ROWS:].reshape(DEC_BATCH, DEC_SEQ * N_IDX_HEADS, IDX_DIM)
            w32 = kiwi[P_ROWS:, IDX_DIM:IDX_DIM + N_IDX_HEADS].reshape(DEC_BATCH, DEC_SEQ * N_IDX_HEADS, 1)
            pad_rows = lambda a: jnp.pad(a.reshape(DEC_BATCH, DEC_SEQ, -1), ((0, 0), (0, LANES - DEC_SEQ), (0, 0)))
            sc = _sample_index_scores(page_table, qi32, w32, pad_rows(kib[P_ROWS:]), cki, j)
            bias = _sample_topk_bias(sc.reshape(S_ROWS, S_KEYS).T).T
            bias16 = jnp.tile(bias.reshape(DEC_BATCH, DEC_SEQ, S_KEYS), (1, Q_PER_KV, 1))
            q16 = q[P_ROWS:].reshape(DEC_BATCH, DEC_SEQ, N_KV_HEADS, Q_PER_KV, HEAD_DIM)
            q16 = q16.transpose(0, 2, 3, 1, 4).reshape(DEC_BATCH, N_KV_HEADS, Q_PER_KV * DEC_SEQ, HEAD_DIM)
            v_new = vb[P_ROWS:].reshape(S_ROWS, N_KV_HEADS, 2, HEAD_DIM)[:, :, 0]
            o16 = _sample_attention(page_table, q16, bias16, pad_rows(kb[P_ROWS:]), pad_rows(v_new), ck, cv, j)
            m_s = o16.reshape(DEC_BATCH, N_KV_HEADS, Q_PER_KV, DEC_SEQ, HEAD_DIM).transpose(0, 3, 1, 2, 4)
            m = (m_p, m_s.reshape(S_ROWS, Q_W))
            outs["kip"].append(kiwi[:P_ROWS, :IDX_DIM].reshape(BATCH, SEQ, IDX_DIM))
            outs["kis"].append(kiwi[P_ROWS:, :IDX_DIM].reshape(DEC_BATCH, DEC_SEQ, IDX_DIM))
            w_out = b_w_out[j]
        x = tuple(_post(x, tuple(m), w_out.astype(bf16), norm_ffn[i].reshape(1, D_MODEL), ffn_w1[i].astype(bf16),
                        ffn_w2[i].astype(bf16), g_final, final=(i == DEPTH - 1)))

    y_prompt = x[0].reshape(BATCH, SEQ, D_MODEL)
    y_sample = x[1].reshape(DEC_BATCH, DEC_SEQ, D_MODEL)
    n_layers = DEPTH // 2
    kp, ks, vp, vs = kv_new
    prompt_kv = lambda a: a.reshape(n_layers, BATCH, SEQ, N_KV_HEADS, HEAD_DIM)
    sample_kv = lambda a: a.reshape(n_layers, DEC_BATCH, DEC_SEQ, N_KV_HEADS, HEAD_DIM)
    st = jnp.stack
    return (y_prompt, y_sample, prompt_kv(kp), prompt_kv(vp), st(outs["kip"]), sample_kv(ks), sample_kv(vs),
            st(outs["kis"]), st(outs["cvp"]), st(outs["cvs"]))
```

```python
import functools

import jax
import jax.numpy as jnp
from jax import lax
from jax.experimental import pallas as pl
from jax.experimental.pallas import tpu as pltpu

D_MODEL = 1024
BATCH = 8
SEQ = 2048
DEPTH = 4
DEC_BATCH = 128
DEC_SEQ = 4
PAST_LEN = 2048
PAGE_SIZE = 128
N_PAGES = PAST_LEN // PAGE_SIZE
CHUNK = 128
D_A = D_MODEL
N_GROUPS_A = 8
GROUP_DIM_A = D_A // N_GROUPS_A
N_HEADS = 8
HEAD_DIM = D_MODEL // N_HEADS
N_KV_HEADS = 2
Q_PER_KV = N_HEADS // N_KV_HEADS
N_IDX_HEADS = 8
IDX_DIM = 64
TOPK_MAX = 256
QUERY_BLOCK = 128
ROPE_THETA = 500000.0
ROT_DIV = 4
IDX_SCALE = float((IDX_DIM * N_IDX_HEADS) ** -0.5)
ATTN_SCALE = float(HEAD_DIM ** -0.5)
Q_W = N_HEADS * HEAD_DIM
KV_W = N_KV_HEADS * HEAD_DIM
QI_W = N_IDX_HEADS * IDX_DIM
B_PROJ = Q_W + 2 * KV_W + QI_W + IDX_DIM + N_IDX_HEADS
D_FF = 4 * D_MODEL
EPS = 1e-6

LANES = 128
SUBLANES = 8
P_ROWS = BATCH * SEQ
S_ROWS = DEC_BATCH * DEC_SEQ
ROWS = P_ROWS + S_ROWS
TM = 512
N_TILES = ROWS // TM
P_TILES = P_ROWS // TM
TILES_PER_SEQ = SEQ // TM
CHUNKS_PER_TILE = TM // CHUNK
FF_TILE = 1024
B_PROJ_PAD = Q_W + 2 * KV_W + QI_W + LANES
KIWI_OFF = Q_W + 2 * KV_W + QI_W
S_KEYS = PAST_LEN + LANES
KV_SPAN = 256
VMEM_LIMIT = 56 * 1024 * 1024
INT_MIN = -(2 ** 31)
LOG2_E = 1.4426950408889634
SATT_SEQS = 2
SIDX_SEQS = 4
BISECT_ROWS = 128

f32 = jnp.float32
bf16 = jnp.bfloat16

assert S_ROWS == TM and P_ROWS % TM == 0 and SEQ % TM == 0


def _params(*sem):
    return pltpu.CompilerParams(dimension_semantics=sem, vmem_limit_bytes=VMEM_LIMIT)


def _rms(x, g):
    return x * lax.rsqrt(jnp.mean(x * x, axis=-1, keepdims=True) + EPS) * g


def _gelu(x):
    a = -2.0 * LOG2_E * 0.7978845608028654
    return x / (1.0 + jnp.exp2(x * (a + (a * 0.044715) * (x * x))))


def _dot(a, b):
    return jnp.dot(a, b, preferred_element_type=f32)


def _dot_nt(a, b):
    return lax.dot_general(a, b, (((1,), (1,)), ((), ())), preferred_element_type=f32)


def _prompt_tile(i):
    return (jnp.minimum(i, P_TILES - 1), 0)


def _first_tile(i):
    return (0, 0)


def _pair_specs(width):
    return [pl.BlockSpec((TM, width), _prompt_tile), pl.BlockSpec((S_ROWS, width), _first_tile)]


def _rows_specs(n_arrays, width):
    return [pl.BlockSpec((TM, width), lambda i: (i, 0))] if n_arrays == 1 else _pair_specs(width)


def _rows_tile(i, refs):
    if len(refs) == 1:
        return refs[0][...]
    p_ref, s_ref = refs
    return jnp.where(i == P_TILES, s_ref[...], p_ref[...])


def _store_rows_tile(i, refs, value):
    if len(refs) == 1:
        refs[0][...] = value
        return
    p_ref, s_ref = refs

    @pl.when(i < P_TILES)
    def _():
        p_ref[...] = value

    @pl.when(i == P_TILES)
    def _():
        s_ref[...] = value


def _mixa_kernel(*refs, n_x):
    x_refs = refs[:n_x]
    g_ref, win_ref, gain_ref, ws_ref, bias_ref, scoef_ref, sbias_ref, m_ref, cvp_ref, cvs_ref = refs[n_x:]
    i = pl.program_id(0)
    h = _rms(_rows_tile(i, x_refs), g_ref[...]).astype(bf16)
    uv = _gelu(_dot(h, win_ref[...]))
    u = uv[:, :D_A]
    v = uv[:, D_A:]
    vc = v - jnp.mean(v, axis=-1, keepdims=True)
    vn = vc * lax.rsqrt(jnp.mean(vc * vc, axis=-1, keepdims=True) + EPS) * gain_ref[...]

    @pl.when(i < P_TILES)
    def _prompt():
        t_idx = lax.broadcasted_iota(jnp.int32, (CHUNK, CHUNK), 0)
        s_idx = lax.broadcasted_iota(jnp.int32, (CHUNK, CHUNK), 1)
        causal = s_idx <= t_idx
        ws = [jnp.where(causal, ws_ref[g], 0.0).astype(bf16) for g in range(N_GROUPS_A)]
        for c in range(CHUNKS_PER_TILE):
            rows = slice(c * CHUNK, (c + 1) * CHUNK)
            v_c = vn[rows].astype(bf16)
            mixed = jnp.concatenate(
                [_dot(ws[g], v_c[:, g * GROUP_DIM_A:(g + 1) * GROUP_DIM_A]) for g in range(N_GROUPS_A)],
                axis=1) + bias_ref[...]
            m_ref[rows, :] = (u[rows] * mixed).astype(bf16)

        @pl.when(i % TILES_PER_SEQ == TILES_PER_SEQ - 1)
        def _():
            cvp_ref[0] = vn[TM - CHUNK:]

    @pl.when(i == P_TILES)
    def _sample():
        grp = TM // SUBLANES
        v3 = vn.reshape(grp, SUBLANES, D_A)
        mixed = sbias_ref[...][None] + scoef_ref[0][None] * v3
        for d in range(1, DEC_SEQ):
            mixed = mixed + scoef_ref[d][None] * pltpu.roll(v3, d, 1)
        m_ref[...] = (u.reshape(grp, SUBLANES, D_A) * mixed).reshape(TM, D_A).astype(bf16)
        cvs_ref[...] = vn


def _mixa(x, g, w_in, gain, ws, bias, scoef, sbias):
    const2 = lambda i: (0, 0)
    const3 = lambda i: (0, 0, 0)
    return pl.pallas_call(
        functools.partial(_mixa_kernel, n_x=len(x)),
        grid=(N_TILES,),
        in_specs=_rows_specs(len(x), D_MODEL) + [
            pl.BlockSpec((1, D_MODEL), const2),
            pl.BlockSpec((D_MODEL, 2 * D_A), const2),
            pl.BlockSpec((1, D_A), const2),
            pl.BlockSpec((N_GROUPS_A, CHUNK, CHUNK), const3),
            pl.BlockSpec((CHUNK, D_A), const2),
            pl.BlockSpec((DEC_SEQ, SUBLANES, D_A), const3),
            pl.BlockSpec((SUBLANES, D_A), const2),
        ],
        out_specs=[
            pl.BlockSpec((TM, D_A), lambda i: (i, 0)),
            pl.BlockSpec((1, CHUNK, D_A), lambda i: (jnp.minimum(i // TILES_PER_SEQ, BATCH - 1), 0, 0)),
            pl.BlockSpec((S_ROWS, D_A), const2),
        ],
        out_shape=[
            jax.ShapeDtypeStruct((ROWS, D_A), bf16),
            jax.ShapeDtypeStruct((BATCH, CHUNK, D_A), f32),
            jax.ShapeDtypeStruct((S_ROWS, D_A), f32),
        ],
        compiler_params=_params("arbitrary"),
        name="mixer_a",
    )(*x, g, w_in, gain, ws, bias, scoef, sbias)


def _post_kernel(*refs, n_x, n_m, final):
    x_refs, m_refs = refs[:n_x], refs[n_x:n_x + n_m]
    wout_ref, g_ref, w1_ref, w2_ref, gf_ref = refs[n_x + n_m:n_x + n_m + 5]
    y_refs = refs[n_x + n_m + 5:]
    i = pl.program_id(0)
    x1 = _rows_tile(i, x_refs) + _dot(_rows_tile(i, m_refs), wout_ref[...])
    h = _rms(x1, g_ref[...]).astype(bf16)
    y = x1
    for c in range(D_FF // FF_TILE):
        a = jnp.maximum(_dot(h, w1_ref[:, c * FF_TILE:(c + 1) * FF_TILE]), 0.0)
        y = y + _dot((a * a).astype(bf16), w2_ref[c * FF_TILE:(c + 1) * FF_TILE, :])
    if final:
        y = _rms(y, gf_ref[...])
    _store_rows_tile(i, y_refs, y)


def _post(x, m, w_out, g, w1, w2, g_final, final):
    const2 = lambda i: (0, 0)
    resident = dict(pipeline_mode=pl.Buffered(1))
    if final:
        out_specs = _pair_specs(D_MODEL)
        out_shape = [jax.ShapeDtypeStruct((P_ROWS, D_MODEL), f32), jax.ShapeDtypeStruct((S_ROWS, D_MODEL), f32)]
    else:
        out_specs = _rows_specs(1, D_MODEL)
        out_shape = [jax.ShapeDtypeStruct((ROWS, D_MODEL), f32)]
    return pl.pallas_call(
        functools.partial(_post_kernel, n_x=len(x), n_m=len(m), final=final),
        grid=(N_TILES,),
        in_specs=_rows_specs(len(x), D_MODEL) + _rows_specs(len(m), D_MODEL) + [
            pl.BlockSpec((D_MODEL, D_MODEL), const2, **resident),
            pl.BlockSpec((1, D_MODEL), const2),
            pl.BlockSpec((D_MODEL, D_FF), const2, **resident),
            pl.BlockSpec((D_FF, D_MODEL), const2, **resident),
            pl.BlockSpec((1, D_MODEL), const2),
        ],
        out_specs=out_specs,
        out_shape=out_shape,
        compiler_params=_params("arbitrary"),
        name="post_ffn",
    )(*x, *m, w_out, g, w1, w2, g_final)


def _rope(x, cos, sin_lo, sin_hi, half):
    return x * cos + pltpu.roll(x, LANES - half, 1) * sin_lo + pltpu.roll(x, half, 1) * sin_hi


def _proj_kernel(*refs, n_alias):
    x_ref, g_ref, w_ref, rh_ref, ri_ref, rk_ref = refs[:6]
    (q_ref, kb_ref, vb_ref, qi_ref, kiwi_ref, kib_ref, wit_ref,
     kp_ref, ks_ref, vp_ref, vs_ref) = refs[6 + n_alias:]
    i = pl.program_id(0)
    h = _rms(x_ref[...], g_ref[...]).astype(bf16)
    p = _dot(h, w_ref[...])
    half_h = HEAD_DIM // ROT_DIV // 2
    half_i = IDX_DIM // ROT_DIV // 2
    ch, slh, shh = rh_ref[0], rh_ref[1], rh_ref[2]
    for hd in range(N_HEADS):
        cols = slice(hd * HEAD_DIM, (hd + 1) * HEAD_DIM)
        q_ref[:, cols] = _rope(p[:, cols], ch, slh, shh, half_h).astype(bf16)
    one_hot = jnp.where(lax.broadcasted_iota(jnp.int32, (TM, HEAD_DIM), 1) == 0, 1.0, 0.0).astype(bf16)
    for kh in range(N_KV_HEADS):
        cols = slice(kh * HEAD_DIM, (kh + 1) * HEAD_DIM)
        kr = _rope(p[:, Q_W + kh * HEAD_DIM:Q_W + (kh + 1) * HEAD_DIM], ch, slh, shh, half_h)
        v = p[:, Q_W + KV_W + kh * HEAD_DIM:Q_W + KV_W + (kh + 1) * HEAD_DIM]
        kb_ref[:, cols] = kr.astype(bf16)
        vb_ref[:, 2 * kh * HEAD_DIM:(2 * kh + 1) * HEAD_DIM] = v.astype(bf16)
        vb_ref[:, (2 * kh + 1) * HEAD_DIM:(2 * kh + 2) * HEAD_DIM] = one_hot
        head_rows = pl.ds(kh, TM, stride=N_KV_HEADS)

        for slot in range(kp_ref.shape[0]):
            @pl.when(i < P_TILES)
            def _(slot=slot):
                kp_ref[slot, head_rows, :] = kr
                vp_ref[slot, head_rows, :] = v

            @pl.when(i == P_TILES)
            def _(slot=slot):
                ks_ref[slot, head_rows, :] = kr
                vs_ref[slot, head_rows, :] = v
    ci, sli, shi = ri_ref[0], ri_ref[1], ri_ref[2]
    off = Q_W + 2 * KV_W
    for j in range(QI_W // LANES):
        cols = slice(j * LANES, (j + 1) * LANES)
        qi_ref[:, cols] = _rope(p[:, off + j * LANES:off + (j + 1) * LANES], ci, sli, shi, half_i).astype(bf16)
    kiwi = _rope(p[:, KIWI_OFF:KIWI_OFF + LANES], rk_ref[0], rk_ref[1], rk_ref[2], half_i)
    kiwi_ref[...] = kiwi
    kib_ref[...] = kiwi[:, :IDX_DIM].astype(bf16)
    wit_ref[...] = kiwi.T[IDX_DIM:IDX_DIM + N_IDX_HEADS, :]


def _proj(x, g, w, rope_h, rope_i, rope_k, layer, kv_prev):
    const2 = lambda i: (0, 0)
    rope_map = lambda i: (0, jnp.where(i < P_TILES, i % TILES_PER_SEQ, TILES_PER_SEQ), 0)
    row = lambda i: (i, 0)
    widths = [(Q_W, bf16), (KV_W, bf16), (2 * KV_W, bf16), (QI_W, bf16), (LANES, f32), (IDX_DIM, bf16)]
    n_layers = DEPTH // 2
    kv_rows = TM * N_KV_HEADS
    slots, first = (n_layers, 0) if kv_prev is None else (1, layer)
    kv_specs = [pl.BlockSpec((slots, kv_rows, HEAD_DIM), lambda i: (first, jnp.minimum(i, P_TILES - 1), 0)),
                pl.BlockSpec((slots, kv_rows, HEAD_DIM), lambda i: (first, 0, 0))] * 2
    kv_shapes = [jax.ShapeDtypeStruct((n_layers, P_ROWS * N_KV_HEADS, HEAD_DIM), f32),
                 jax.ShapeDtypeStruct((n_layers, S_ROWS * N_KV_HEADS, HEAD_DIM), f32)] * 2
    n_in = 6
    n_out = len(widths) + 1
    alias_in = [] if kv_prev is None else list(kv_prev)
    return pl.pallas_call(
        functools.partial(_proj_kernel, n_alias=len(alias_in)),
        grid=(N_TILES,),
        in_specs=_rows_specs(1, D_MODEL) + [
            pl.BlockSpec((1, D_MODEL), const2),
            pl.BlockSpec((D_MODEL, B_PROJ_PAD), const2),
            pl.BlockSpec((3, TM, LANES), rope_map),
            pl.BlockSpec((3, TM, LANES), rope_map),
            pl.BlockSpec((3, TM, LANES), rope_map),
        ] + [pl.BlockSpec(memory_space=pl.ANY)] * len(alias_in),
        out_specs=[pl.BlockSpec((TM, w_), row) for w_, _ in widths]
        + [pl.BlockSpec((N_IDX_HEADS, TM), lambda i: (0, i))] + kv_specs,
        out_shape=[jax.ShapeDtypeStruct((ROWS, w_), dt) for w_, dt in widths]
        + [jax.ShapeDtypeStruct((N_IDX_HEADS, ROWS), f32)] + kv_shapes,
        input_output_aliases={n_in + a: n_out + a for a in range(len(alias_in))},
        compiler_params=_params("arbitrary"),
        name="dsa_proj",
    )(x, g, w, rope_h, rope_i, rope_k, *alias_in)


def _order_key(sc, valid):
    bits = pltpu.bitcast(sc, jnp.int32)
    return jnp.where(valid, bits ^ ((bits >> 31) & 0x7FFFFFFF), INT_MIN)


def _count0(mask):
    return jnp.sum(jnp.where(mask, 1.0, 0.0), axis=0, keepdims=True)


def _kth_largest(key_ref, n_keys, k, hi_s, lo_s):
    i16 = jnp.int16
    half_min = -(2 ** 15)
    n_chunks = n_keys // BISECT_ROWS

    def chunk(c):
        return slice(c * BISECT_ROWS, (c + 1) * BISECT_ROWS)

    def count(ref, pred):
        acc = jnp.zeros((BISECT_ROWS, LANES), i16)
        for c in range(n_chunks):
            acc = jnp.where(pred(ref[chunk(c), :]), acc + jnp.ones((), i16), acc)
        return jnp.sum(acc.astype(jnp.int32), axis=0, keepdims=True)

    def bisect(ref, thr, n_bits, need):
        def step(j, thr):
            cand = thr + lax.shift_left(jnp.int32(1), n_bits - 1 - j)
            cand16 = cand.astype(i16)
            return jnp.where(count(ref, lambda blk: blk >= cand16) >= need, cand, thr)

        return lax.fori_loop(0, n_bits, step, thr)

    for c in range(n_chunks):
        key = key_ref[chunk(c), :]
        hi_s[chunk(c), :] = (key >> 16).astype(i16)
        lo_s[chunk(c), :] = ((key & 0xFFFF) + half_min).astype(i16)
    zero = jnp.zeros((1, LANES), jnp.int32)
    zero16 = zero.astype(i16)
    base = jnp.where(count(hi_s, lambda blk: blk >= zero16) >= k, zero, zero + half_min)
    t_hi = bisect(hi_s, base, 15, k)
    t_hi16 = t_hi.astype(i16)
    need_lo = k - count(hi_s, lambda blk: blk > t_hi16)
    for c in range(n_chunks):
        lo_s[chunk(c), :] = jnp.where(hi_s[chunk(c), :] == t_hi16, lo_s[chunk(c), :], jnp.full((), half_min, i16))
    t_lo = bisect(lo_s, zero + half_min, 16, need_lo)
    return t_hi * (2 ** 16) + (t_lo - half_min)


def _topk_bias_t(key, thr, k, out_ref):
    n_keys = key.shape[0]
    kf = float(k)
    thr = jnp.maximum(thr, INT_MIN + 1)
    ge = key >= thr
    excess = jnp.max(_count0(ge)) - kf

    @pl.when(excess <= 0.0)
    def _():
        out_ref[:n_keys, :] = jnp.where(ge, 0.0, -jnp.inf)

    @pl.when(excess > 0.0)
    def _():
        pos = lax.broadcasted_iota(jnp.int32, key.shape, 0)
        above = key > thr
        tie = key == thr
        need = kf - _count0(above)
        n_bits = max(1, (n_keys - 1).bit_length())

        def step(j, p):
            cand = p + lax.shift_left(jnp.int32(1), n_bits - 1 - j)
            return jnp.where(_count0(tie & (pos < cand)) < need, cand, p)

        last = lax.fori_loop(0, n_bits, step, jnp.zeros_like(thr))
        out_ref[:n_keys, :] = jnp.where(above | (tie & (pos <= last)), 0.0, -jnp.inf)


def _dsa_prompt_kernel(q_ref, qi_ref, wt_ref, k_ref, v_ref, ki_ref, o_ref,
                       key_s, hi_s, lo_s, thr_s, biast_s, bias_s):
    i = pl.program_id(1)
    k_top = min(TOPK_MAX, SEQ // 4)
    blocks_per_span = KV_SPAN // QUERY_BLOCK

    def run(span):
        n_keys = (span + 1) * KV_SPAN
        ki = ki_ref[:n_keys, :]
        sc = jnp.zeros((n_keys, QUERY_BLOCK), f32)
        for hd in range(N_IDX_HEADS):
            d = _dot_nt(ki, qi_ref[:, hd * IDX_DIM:(hd + 1) * IDX_DIM])
            sc = sc + jnp.maximum(d, 0.0) * wt_ref[hd:hd + 1, :]
        sc = sc * IDX_SCALE
        pos = lax.broadcasted_iota(jnp.int32, (n_keys, QUERY_BLOCK), 0)
        query = i * QUERY_BLOCK + lax.broadcasted_iota(jnp.int32, (n_keys, QUERY_BLOCK), 1)
        valid = pos <= query
        key_s[:n_keys, :] = _order_key(sc, valid)

        thr_s[...] = jnp.full(thr_s.shape, INT_MIN, jnp.int32)
        for blk in range(span * blocks_per_span, (span + 1) * blocks_per_span):
            if (blk + 1) * QUERY_BLOCK > k_top:
                @pl.when(i == blk)
                def _(blk=blk):
                    thr_s[0:1, :] = _kth_largest(key_s, (blk + 1) * QUERY_BLOCK, k_top, hi_s, lo_s)

        _topk_bias_t(key_s[:n_keys, :], thr_s[0:1, :], k_top, biast_s)
        bias_s[:, :n_keys] = biast_s[:n_keys, :].T
        bias = bias_s[:, :n_keys][None]
        for kh in range(N_KV_HEADS):
            q_rows = jnp.concatenate(
                [q_ref[:, (kh * Q_PER_KV + g) * HEAD_DIM:(kh * Q_PER_KV + g + 1) * HEAD_DIM]
                 for g in range(Q_PER_KV)], axis=0)
            s = _dot_nt(q_rows, k_ref[:n_keys, kh * HEAD_DIM:(kh + 1) * HEAD_DIM])
            s = (s.reshape(Q_PER_KV, QUERY_BLOCK, n_keys) + bias).reshape(Q_PER_KV * QUERY_BLOCK, n_keys)
            p = jnp.exp2((s - jnp.max(s, axis=1, keepdims=True)) * (ATTN_SCALE * LOG2_E))
            o = _dot(p.astype(bf16), v_ref[:n_keys, 2 * kh * HEAD_DIM:(2 * kh + 2) * HEAD_DIM])
            o = o[:, :HEAD_DIM] / o[:, HEAD_DIM:HEAD_DIM + 1]
            for g in range(Q_PER_KV):
                hd = kh * Q_PER_KV + g
                o_ref[:, hd * HEAD_DIM:(hd + 1) * HEAD_DIM] = o[g * QUERY_BLOCK:(g + 1) * QUERY_BLOCK].astype(bf16)

    for span in range(SEQ // KV_SPAN):
        pl.when(i // blocks_per_span == span)(functools.partial(run, span))


def _dsa_prompt(q, qi, wi_t, kb, vb, kib):
    nq = SEQ // QUERY_BLOCK
    qmap = lambda b, i: (b * nq + i, 0)
    kmap = lambda b, i: (b, 0)
    return pl.pallas_call(
        _dsa_prompt_kernel,
        grid=(BATCH, nq),
        in_specs=[
            pl.BlockSpec((QUERY_BLOCK, Q_W), qmap),
            pl.BlockSpec((QUERY_BLOCK, QI_W), qmap),
            pl.BlockSpec((N_IDX_HEADS, QUERY_BLOCK), lambda b, i: (0, b * nq + i)),
            pl.BlockSpec((SEQ, KV_W), kmap),
            pl.BlockSpec((SEQ, 2 * KV_W), kmap),
            pl.BlockSpec((SEQ, IDX_DIM), kmap),
        ],
        out_specs=pl.BlockSpec((QUERY_BLOCK, Q_W), qmap),
        out_shape=jax.ShapeDtypeStruct((P_ROWS, Q_W), bf16),
        scratch_shapes=[
            pltpu.VMEM((SEQ, QUERY_BLOCK), jnp.int32),
            pltpu.VMEM((SEQ, QUERY_BLOCK), jnp.int16),
            pltpu.VMEM((SEQ, QUERY_BLOCK), jnp.int16),
            pltpu.VMEM((SUBLANES, QUERY_BLOCK), jnp.int32),
            pltpu.VMEM((SEQ, QUERY_BLOCK), f32),
            pltpu.VMEM((QUERY_BLOCK, SEQ), f32),
        ],
        compiler_params=_params("arbitrary", "arbitrary"),
        name="dsa_prompt",
    )(q, qi, wi_t, kb, vb, kib)


def _sidx_kernel(pt_ref, qi_ref, w_ref, kin_ref, cki_hbm, sc_ref, pbuf, sem, *, first_page):
    step = pl.program_id(0)
    slot = step % 2

    def page_copies(st, sl):
        return [pltpu.make_async_copy(cki_hbm.at[first_page + pt_ref[st * SIDX_SEQS + g, j]], pbuf.at[sl, g, j],
                                      sem.at[sl])
                for g in range(SIDX_SEQS) for j in range(N_PAGES)]

    @pl.when(step == 0)
    def _():
        for c in page_copies(0, 0):
            c.start()

    @pl.when(step + 1 < pl.num_programs(0))
    def _():
        for c in page_copies(step + 1, 1 - slot):
            c.start()

    for c in page_copies(step, slot):
        c.wait()

    for g in range(SIDX_SEQS):
        qi = qi_ref[g]
        d = jnp.concatenate([_dot(qi, pbuf[slot, g, j].astype(bf16)) for j in range(N_PAGES)]
                            + [_dot_nt(qi, kin_ref[g])], axis=1)
        d = jnp.maximum(d, 0.0) * w_ref[g]
        sc_ref[g] = jnp.sum(d.reshape(DEC_SEQ, N_IDX_HEADS, S_KEYS), axis=1) * IDX_SCALE


def _sample_index_scores(page_table, qi32, w32, ki_new, cache_kidx, layer):
    n_pool = cache_kidx.shape[0] // (DEPTH // 2)
    seq3 = lambda n, pt: (n, 0, 0)
    grid_spec = pltpu.PrefetchScalarGridSpec(
        num_scalar_prefetch=1,
        grid=(DEC_BATCH // SIDX_SEQS,),
        in_specs=[
            pl.BlockSpec((SIDX_SEQS, DEC_SEQ * N_IDX_HEADS, IDX_DIM), seq3),
            pl.BlockSpec((SIDX_SEQS, DEC_SEQ * N_IDX_HEADS, 1), seq3),
            pl.BlockSpec((SIDX_SEQS, LANES, IDX_DIM), seq3),
            pl.BlockSpec(memory_space=pl.ANY),
        ],
        out_specs=pl.BlockSpec((SIDX_SEQS, DEC_SEQ, S_KEYS), seq3),
        scratch_shapes=[
            pltpu.VMEM((2, SIDX_SEQS, N_PAGES, IDX_DIM, PAGE_SIZE), f32),
            pltpu.SemaphoreType.DMA((2,)),
        ],
    )
    return pl.pallas_call(
        functools.partial(_sidx_kernel, first_page=layer * n_pool),
        grid_spec=grid_spec,
        out_shape=jax.ShapeDtypeStruct((DEC_BATCH, DEC_SEQ, S_KEYS), f32),
        compiler_params=_params("arbitrary"),
        name="dsa_sample_index",
    )(page_table, qi32, w32, ki_new, cache_kidx)


def _stopk_kernel(sc_ref, bias_ref, key_s, hi_s, lo_s):
    shape = sc_ref.shape
    pos = lax.broadcasted_iota(jnp.int32, shape, 0)
    t = (pl.program_id(0) * LANES + lax.broadcasted_iota(jnp.int32, shape, 1)) % DEC_SEQ
    valid = pos <= t + PAST_LEN
    k_top = min(TOPK_MAX, (PAST_LEN + DEC_SEQ) // 4)
    key_s[...] = _order_key(sc_ref[...], valid)
    thr = _kth_largest(key_s, S_KEYS, k_top, hi_s, lo_s)
    _topk_bias_t(key_s[...], thr, k_top, bias_ref)


def _sample_topk_bias(sc_t):
    spec = pl.BlockSpec((S_KEYS, LANES), lambda i: (0, i))
    return pl.pallas_call(
        _stopk_kernel,
        grid=(S_ROWS // LANES,),
        in_specs=[spec],
        out_specs=spec,
        out_shape=jax.ShapeDtypeStruct((S_KEYS, S_ROWS), f32),
        scratch_shapes=[pltpu.VMEM((S_KEYS, LANES), jnp.int32), pltpu.VMEM((S_KEYS, LANES), jnp.int16),
                        pltpu.VMEM((S_KEYS, LANES), jnp.int16)],
        compiler_params=_params("arbitrary"),
        name="dsa_sample_topk",
    )(sc_t)


def _satt_kernel(pt_ref, q_ref, bias_ref, kn_ref, vn_ref, ck_hbm, cv_hbm, o_ref,
                 kbuf, vbuf, sem, *, first_page):
    step = pl.program_id(0)
    slot = step % 2

    def page_copies(st, sl):
        copies = []
        for g in range(SATT_SEQS):
            for j in range(N_PAGES):
                page = first_page + pt_ref[st * SATT_SEQS + g, j]
                copies.append(pltpu.make_async_copy(ck_hbm.at[page], kbuf.at[sl, g, j], sem.at[sl, 0]))
                copies.append(pltpu.make_async_copy(cv_hbm.at[page], vbuf.at[sl, g, j], sem.at[sl, 1]))
        return copies

    @pl.when(step == 0)
    def _():
        for c in page_copies(0, 0):
            c.start()

    @pl.when(step + 1 < pl.num_programs(0))
    def _():
        for c in page_copies(step + 1, 1 - slot):
            c.start()

    for c in page_copies(step, slot):
        c.wait()

    for g in range(SATT_SEQS):
        for kh in range(N_KV_HEADS):
            cols = slice(kh * HEAD_DIM, (kh + 1) * HEAD_DIM)
            head_rows = pl.ds(kh, PAGE_SIZE, stride=N_KV_HEADS)
            k_all = jnp.concatenate([kbuf[slot, g, j, head_rows, :].astype(bf16) for j in range(N_PAGES)]
                                    + [kn_ref[g, :, cols]], axis=0)
            v_all = jnp.concatenate([vbuf[slot, g, j, head_rows, :].astype(bf16) for j in range(N_PAGES)]
                                    + [vn_ref[g, :, cols]], axis=0)
            s = _dot_nt(q_ref[g, kh], k_all) + bias_ref[g]
            p = jnp.exp2((s - jnp.max(s, axis=1, keepdims=True)) * (ATTN_SCALE * LOG2_E))
            denom = jnp.sum(p, axis=1, keepdims=True)
            o_ref[g, kh] = (_dot(p.astype(bf16), v_all) / denom).astype(bf16)


def _sample_attention(page_table, q16, bias16, k_new, v_new, cache_k, cache_v, layer):
    n_pool = cache_k.shape[0] // (DEPTH // 2)
    rows = DEC_SEQ * Q_PER_KV
    seq3 = lambda n, pt: (n, 0, 0)
    seq4 = lambda n, pt: (n, 0, 0, 0)
    page_shape = (PAGE_SIZE * N_KV_HEADS, HEAD_DIM)
    grid_spec = pltpu.PrefetchScalarGridSpec(
        num_scalar_prefetch=1,
        grid=(DEC_BATCH // SATT_SEQS,),
        in_specs=[
            pl.BlockSpec((SATT_SEQS, N_KV_HEADS, rows, HEAD_DIM), seq4),
            pl.BlockSpec((SATT_SEQS, rows, S_KEYS), seq3),
            pl.BlockSpec((SATT_SEQS, LANES, KV_W), seq3),
            pl.BlockSpec((SATT_SEQS, LANES, KV_W), seq3),
            pl.BlockSpec(memory_space=pl.ANY),
            pl.BlockSpec(memory_space=pl.ANY),
        ],
        out_specs=pl.BlockSpec((SATT_SEQS, N_KV_HEADS, rows, HEAD_DIM), seq4),
        scratch_shapes=[
            pltpu.VMEM((2, SATT_SEQS, N_PAGES) + page_shape, f32),
            pltpu.VMEM((2, SATT_SEQS, N_PAGES) + page_shape, f32),
            pltpu.SemaphoreType.DMA((2, 2)),
        ],
    )
    return pl.pallas_call(
        functools.partial(_satt_kernel, first_page=layer * n_pool),
        grid_spec=grid_spec,
        out_shape=jax.ShapeDtypeStruct((DEC_BATCH, N_KV_HEADS, rows, HEAD_DIM), bf16),
        compiler_params=_params("arbitrary"),
        name="dsa_sample_attn",
    )(page_table, q16, bias16, k_new, v_new, cache_k, cache_v)


def _rope_tables():
    pos = jnp.concatenate([jnp.arange(SEQ), PAST_LEN + jnp.arange(TM) % DEC_SEQ]).astype(f32)

    def pattern(dim, n_rot_groups):
        r = dim // ROT_DIV
        half = r // 2
        inv = ROPE_THETA ** (-jnp.arange(half, dtype=f32) * 2.0 / r)
        ang = pos[:, None] * inv[None, :]
        cos, sin = jnp.cos(ang), jnp.sin(ang)
        ones = jnp.ones((pos.shape[0], dim - r), f32)
        zeros = jnp.zeros((pos.shape[0], dim - r), f32)
        zh = jnp.zeros_like(sin)
        c = jnp.concatenate([cos, cos, ones], axis=1)
        lo = jnp.concatenate([-sin, zh, zeros], axis=1)
        hi = jnp.concatenate([zh, sin, zeros], axis=1)
        ident = (jnp.ones((pos.shape[0], dim), f32), jnp.zeros((pos.shape[0], dim), f32),
                 jnp.zeros((pos.shape[0], dim), f32))
        reps = LANES // dim
        parts = [(c, lo, hi) if j < n_rot_groups else ident for j in range(reps)]
        return jnp.stack([jnp.concatenate([p_[t] for p_ in parts], axis=1) for t in range(3)])

    return pattern(HEAD_DIM, 1), pattern(IDX_DIM, 2), pattern(IDX_DIM, 1)


def _sample_mix_tables(w_s, b_s):
    t = jnp.arange(SUBLANES) % DEC_SEQ
    coefs = []
    for d in range(DEC_SEQ):
        c = jnp.where((t - d >= 0)[None, :], w_s[:, t, jnp.maximum(t - d, 0)], 0.0)
        coefs.append(jnp.repeat(c.T, GROUP_DIM_A, axis=1))
    bias = jnp.repeat(b_s[:, t].T, GROUP_DIM_A, axis=1)
    return jnp.stack(coefs), bias


def kernel(x_prompt, x_sample, cache_k, cache_v, cache_kidx, page_table, norm_mix, norm_ffn,
           a_w_in, a_v_gain, a_w_s, a_b_s, a_w_out, b_w_in, b_w_out, ffn_w1, ffn_w2, norm_final):
    x = (x_prompt.reshape(P_ROWS, D_MODEL), x_sample.reshape(S_ROWS, D_MODEL))
    kv_new = None
    rope_h, rope_i, rope_k = _rope_tables()
    n_pool = cache_k.shape[1]
    ck = cache_k.reshape(-1, PAGE_SIZE * N_KV_HEADS, HEAD_DIM)
    cv = cache_v.reshape(-1, PAGE_SIZE * N_KV_HEADS, HEAD_DIM)
    cki = jnp.swapaxes(cache_kidx, 2, 3).reshape(-1, IDX_DIM, PAGE_SIZE)
    g_final = norm_final.reshape(1, D_MODEL)
    outs = {name: [] for name in ("kip", "kis", "cvp", "cvs")}

    for i in range(DEPTH):
        j = i // 2
        g_mix = norm_mix[i].reshape(1, D_MODEL)
        if i % 2 == 0:
            scoef, sbias = _sample_mix_tables(a_w_s[j], a_b_s[j])
            bias = jnp.repeat(a_b_s[j].T, GROUP_DIM_A, axis=1)
            *m, cvp, cvs = _mixa(x, g_mix, a_w_in[j].astype(bf16), a_v_gain[j].reshape(1, D_A), a_w_s[j],
                                 bias, scoef, sbias)
            outs["cvp"].append(cvp)
            outs["cvs"].append(cvs.reshape(DEC_BATCH, DEC_SEQ, D_A))
            w_out = a_w_out[j]
        else:
            w = jnp.pad(b_w_in[j], ((0, 0), (0, B_PROJ_PAD - B_PROJ))).astype(bf16)
            q, kb, vb, qi, kiwi, kib, wi_t, *kv_new = _proj(x[0], g_mix, w, rope_h, rope_i, rope_k, j, kv_new)
            m_p = _dsa_prompt(q, qi, wi_t, kb, vb, kib)
            qi32 = qi[P_ROWS:].reshape(DEC_BATCH, DEC_SEQ * N_IDX_HEADS, IDX_DIM)
            w32 = kiwi[P_ROWS:, IDX_DIM:IDX_DIM + N_IDX_HEADS].reshape(DEC_BATCH, DEC_SEQ * N_IDX_HEADS, 1)
            pad_rows = lambda a: jnp.pad(a.reshape(DEC_BATCH, DEC_SEQ, -1), ((0, 0), (0, LANES - DEC_SEQ), (0, 0)))
            sc = _sample_index_scores(page_table, qi32, w32, pad_rows(kib[P_ROWS:]), cki, j)
            bias = _sample_topk_bias(sc.reshape(S_ROWS, S_KEYS).T).T
            bias16 = jnp.tile(bias.reshape(DEC_BATCH, DEC_SEQ, S_KEYS), (1, Q_PER_KV, 1))
            q16 = q[P_ROWS:].reshape(DEC_BATCH, DEC_SEQ, N_KV_HEADS, Q_PER_KV, HEAD_DIM)
            q16 = q16.transpose(0, 2, 3, 1, 4).reshape(DEC_BATCH, N_KV_HEADS, Q_PER_KV * DEC_SEQ, HEAD_DIM)
            v_new = vb[P_ROWS:].reshape(S_ROWS, N_KV_HEADS, 2, HEAD_DIM)[:, :, 0]
            o16 = _sample_attention(page_table, q16, bias16, pad_rows(kb[P_ROWS:]), pad_rows(v_new), ck, cv, j)
            m_s = o16.reshape(DEC_BATCH, N_KV_HEADS, Q_PER_KV, DEC_SEQ, HEAD_DIM).transpose(0, 3, 1, 2, 4)
            m = (m_p, m_s.reshape(S_ROWS, Q_W))
            outs["kip"].append(kiwi[:P_ROWS, :IDX_DIM].reshape(BATCH, SEQ, IDX_DIM))
            outs["kis"].append(kiwi[P_ROWS:, :IDX_DIM].reshape(DEC_BATCH, DEC_SEQ, IDX_DIM))
            w_out = b_w_out[j]
        x = tuple(_post(x, tuple(m), w_out.astype(bf16), norm_ffn[i].reshape(1, D_MODEL), ffn_w1[i].astype(bf16),
                        ffn_w2[i].astype(bf16), g_final, final=(i == DEPTH - 1)))

    y_prompt = x[0].reshape(BATCH, SEQ, D_MODEL)
    y_sample = x[1].reshape(DEC_BATCH, DEC_SEQ, D_MODEL)
    n_layers = DEPTH // 2
    kp, ks, vp, vs = kv_new
    prompt_kv = lambda a: a.reshape(n_layers, BATCH, SEQ, N_KV_HEADS, HEAD_DIM)
    sample_kv = lambda a: a.reshape(n_layers, DEC_BATCH, DEC_SEQ, N_KV_HEADS, HEAD_DIM)
    st = jnp.stack
    return (y_prompt, y_sample, prompt_kv(kp), prompt_kv(vp), st(outs["kip"]), sample_kv(ks), sample_kv(vs),
            st(outs["kis"]), st(outs["cvp"]), st(outs["cvs"]))
```

```python
import functools

import jax
import jax.numpy as jnp
from jax import lax
from jax.experimental import pallas as pl
from jax.experimental.pallas import tpu as pltpu

D_MODEL = 1024
BATCH = 8
SEQ = 2048
DEPTH = 4
DEC_BATCH = 128
DEC_SEQ = 4
PAST_LEN = 2048
PAGE_SIZE = 128
N_PAGES = PAST_LEN // PAGE_SIZE
CHUNK = 128
D_A = D_MODEL
N_GROUPS_A = 8
GROUP_DIM_A = D_A // N_GROUPS_A
N_HEADS = 8
HEAD_DIM = D_MODEL // N_HEADS
N_KV_HEADS = 2
Q_PER_KV = N_HEADS // N_KV_HEADS
N_IDX_HEADS = 8
IDX_DIM = 64
TOPK_MAX = 256
QUERY_BLOCK = 128
ROPE_THETA = 500000.0
ROT_DIV = 4
IDX_SCALE = float((IDX_DIM * N_IDX_HEADS) ** -0.5)
ATTN_SCALE = float(HEAD_DIM ** -0.5)
Q_W = N_HEADS * HEAD_DIM
KV_W = N_KV_HEADS * HEAD_DIM
QI_W = N_IDX_HEADS * IDX_DIM
B_PROJ = Q_W + 2 * KV_W + QI_W + IDX_DIM + N_IDX_HEADS
D_FF = 4 * D_MODEL
EPS = 1e-6

LANES = 128
SUBLANES = 8
P_ROWS = BATCH * SEQ
S_ROWS = DEC_BATCH * DEC_SEQ
ROWS = P_ROWS + S_ROWS
TM = 512
N_TILES = ROWS // TM
P_TILES = P_ROWS // TM
TILES_PER_SEQ = SEQ // TM
CHUNKS_PER_TILE = TM // CHUNK
FF_TILE = 1024
B_PROJ_PAD = Q_W + 2 * KV_W + QI_W + LANES
KIWI_OFF = Q_W + 2 * KV_W + QI_W
S_KEYS = PAST_LEN + LANES
KV_SPAN = 512
VMEM_LIMIT = 56 * 1024 * 1024
INT_MIN = -(2 ** 31)
LOG2_E = 1.4426950408889634
SATT_SEQS = 2
SIDX_SEQS = 4
BISECT_ROWS = 128

f32 = jnp.float32
bf16 = jnp.bfloat16

assert S_ROWS == TM and P_ROWS % TM == 0 and SEQ % TM == 0


def _params(*sem):
    return pltpu.CompilerParams(dimension_semantics=sem, vmem_limit_bytes=VMEM_LIMIT)


def _rms(x, g):
    return x * lax.rsqrt(jnp.mean(x * x, axis=-1, keepdims=True) + EPS) * g


def _gelu(x):
    a = -2.0 * LOG2_E * 0.7978845608028654
    return x / (1.0 + jnp.exp2(x * (a + (a * 0.044715) * (x * x))))


def _dot(a, b):
    return jnp.dot(a, b, preferred_element_type=f32)


def _dot_nt(a, b):
    return lax.dot_general(a, b, (((1,), (1,)), ((), ())), preferred_element_type=f32)


def _prompt_tile(i):
    return (jnp.minimum(i, P_TILES - 1), 0)


def _first_tile(i):
    return (0, 0)


def _pair_specs(width):
    return [pl.BlockSpec((TM, width), _prompt_tile), pl.BlockSpec((S_ROWS, width), _first_tile)]


def _rows_specs(n_arrays, width):
    return [pl.BlockSpec((TM, width), lambda i: (i, 0))] if n_arrays == 1 else _pair_specs(width)


def _rows_tile(i, refs):
    if len(refs) == 1:
        return refs[0][...]
    p_ref, s_ref = refs
    return jnp.where(i == P_TILES, s_ref[...], p_ref[...])


def _store_rows_tile(i, refs, value):
    if len(refs) == 1:
        refs[0][...] = value
        return
    p_ref, s_ref = refs

    @pl.when(i < P_TILES)
    def _():
        p_ref[...] = value

    @pl.when(i == P_TILES)
    def _():
        s_ref[...] = value


def _mixa_kernel(*refs, n_x):
    x_refs = refs[:n_x]
    g_ref, win_ref, gain_ref, ws_ref, bias_ref, scoef_ref, sbias_ref, m_ref, cvp_ref, cvs_ref = refs[n_x:]
    i = pl.program_id(0)
    h = _rms(_rows_tile(i, x_refs), g_ref[...]).astype(bf16)
    uv = _gelu(_dot(h, win_ref[...]))
    u = uv[:, :D_A]
    v = uv[:, D_A:]
    vc = v - jnp.mean(v, axis=-1, keepdims=True)
    vn = vc * lax.rsqrt(jnp.mean(vc * vc, axis=-1, keepdims=True) + EPS) * gain_ref[...]

    @pl.when(i < P_TILES)
    def _prompt():
        t_idx = lax.broadcasted_iota(jnp.int32, (CHUNK, CHUNK), 0)
        s_idx = lax.broadcasted_iota(jnp.int32, (CHUNK, CHUNK), 1)
        causal = s_idx <= t_idx
        ws = [jnp.where(causal, ws_ref[g], 0.0).astype(bf16) for g in range(N_GROUPS_A)]
        for c in range(CHUNKS_PER_TILE):
            rows = slice(c * CHUNK, (c + 1) * CHUNK)
            v_c = vn[rows].astype(bf16)
            mixed = jnp.concatenate(
                [_dot(ws[g], v_c[:, g * GROUP_DIM_A:(g + 1) * GROUP_DIM_A]) for g in range(N_GROUPS_A)],
                axis=1) + bias_ref[...]
            m_ref[rows, :] = (u[rows] * mixed).astype(bf16)

        @pl.when(i % TILES_PER_SEQ == TILES_PER_SEQ - 1)
        def _():
            cvp_ref[0] = vn[TM - CHUNK:]

    @pl.when(i == P_TILES)
    def _sample():
        grp = TM // SUBLANES
        v3 = vn.reshape(grp, SUBLANES, D_A)
        mixed = sbias_ref[...][None] + scoef_ref[0][None] * v3
        for d in range(1, DEC_SEQ):
            mixed = mixed + scoef_ref[d][None] * pltpu.roll(v3, d, 1)
        m_ref[...] = (u.reshape(grp, SUBLANES, D_A) * mixed).reshape(TM, D_A).astype(bf16)
        cvs_ref[...] = vn


def _mixa(x, g, w_in, gain, ws, bias, scoef, sbias):
    const2 = lambda i: (0, 0)
    const3 = lambda i: (0, 0, 0)
    return pl.pallas_call(
        functools.partial(_mixa_kernel, n_x=len(x)),
        grid=(N_TILES,),
        in_specs=_rows_specs(len(x), D_MODEL) + [
            pl.BlockSpec((1, D_MODEL), const2),
            pl.BlockSpec((D_MODEL, 2 * D_A), const2),
            pl.BlockSpec((1, D_A), const2),
            pl.BlockSpec((N_GROUPS_A, CHUNK, CHUNK), const3),
            pl.BlockSpec((CHUNK, D_A), const2),
            pl.BlockSpec((DEC_SEQ, SUBLANES, D_A), const3),
            pl.BlockSpec((SUBLANES, D_A), const2),
        ],
        out_specs=[
            pl.BlockSpec((TM, D_A), lambda i: (i, 0)),
            pl.BlockSpec((1, CHUNK, D_A), lambda i: (jnp.minimum(i // TILES_PER_SEQ, BATCH - 1), 0, 0)),
            pl.BlockSpec((S_ROWS, D_A), const2),
        ],
        out_shape=[
            jax.ShapeDtypeStruct((ROWS, D_A), bf16),
            jax.ShapeDtypeStruct((BATCH, CHUNK, D_A), f32),
            jax.ShapeDtypeStruct((S_ROWS, D_A), f32),
        ],
        compiler_params=_params("arbitrary"),
        name="mixer_a",
    )(*x, g, w_in, gain, ws, bias, scoef, sbias)


def _post_kernel(*refs, n_x, n_m, final):
    x_refs, m_refs = refs[:n_x], refs[n_x:n_x + n_m]
    wout_ref, g_ref, w1_ref, w2_ref, gf_ref = refs[n_x + n_m:n_x + n_m + 5]
    y_refs = refs[n_x + n_m + 5:]
    i = pl.program_id(0)
    x1 = _rows_tile(i, x_refs) + _dot(_rows_tile(i, m_refs), wout_ref[...])
    h = _rms(x1, g_ref[...]).astype(bf16)
    y = x1
    for c in range(D_FF // FF_TILE):
        a = jnp.maximum(_dot(h, w1_ref[:, c * FF_TILE:(c + 1) * FF_TILE]), 0.0)
        y = y + _dot((a * a).astype(bf16), w2_ref[c * FF_TILE:(c + 1) * FF_TILE, :])
    if final:
        y = _rms(y, gf_ref[...])
    _store_rows_tile(i, y_refs, y)


def _post(x, m, w_out, g, w1, w2, g_final, final):
    const2 = lambda i: (0, 0)
    resident = dict(pipeline_mode=pl.Buffered(1))
    if final:
        out_specs = _pair_specs(D_MODEL)
        out_shape = [jax.ShapeDtypeStruct((P_ROWS, D_MODEL), f32), jax.ShapeDtypeStruct((S_ROWS, D_MODEL), f32)]
    else:
        out_specs = _rows_specs(1, D_MODEL)
        out_shape = [jax.ShapeDtypeStruct((ROWS, D_MODEL), f32)]
    return pl.pallas_call(
        functools.partial(_post_kernel, n_x=len(x), n_m=len(m), final=final),
        grid=(N_TILES,),
        in_specs=_rows_specs(len(x), D_MODEL) + _rows_specs(len(m), D_MODEL) + [
            pl.BlockSpec((D_MODEL, D_MODEL), const2, **resident),
            pl.BlockSpec((1, D_MODEL), const2),
            pl.BlockSpec((D_MODEL, D_FF), const2, **resident),
            pl.BlockSpec((D_FF, D_MODEL), const2, **resident),
            pl.BlockSpec((1, D_MODEL), const2),
        ],
        out_specs=out_specs,
        out_shape=out_shape,
        compiler_params=_params("arbitrary"),
        name="post_ffn",
    )(*x, *m, w_out, g, w1, w2, g_final)


def _rope(x, cos, sin_lo, sin_hi, half):
    return x * cos + pltpu.roll(x, LANES - half, 1) * sin_lo + pltpu.roll(x, half, 1) * sin_hi


def _proj_kernel(*refs, n_alias):
    x_ref, g_ref, w_ref, rh_ref, ri_ref, rk_ref = refs[:6]
    (q_ref, kb_ref, vb_ref, qi_ref, kiwi_ref, kib_ref, wit_ref,
     kp_ref, ks_ref, vp_ref, vs_ref) = refs[6 + n_alias:]
    i = pl.program_id(0)
    h = _rms(x_ref[...], g_ref[...]).astype(bf16)
    p = _dot(h, w_ref[...])
    half_h = HEAD_DIM // ROT_DIV // 2
    half_i = IDX_DIM // ROT_DIV // 2
    ch, slh, shh = rh_ref[0], rh_ref[1], rh_ref[2]
    for hd in range(N_HEADS):
        cols = slice(hd * HEAD_DIM, (hd + 1) * HEAD_DIM)
        q_ref[:, cols] = _rope(p[:, cols], ch, slh, shh, half_h).astype(bf16)
    one_hot = jnp.where(lax.broadcasted_iota(jnp.int32, (TM, HEAD_DIM), 1) == 0, 1.0, 0.0).astype(bf16)
    for kh in range(N_KV_HEADS):
        cols = slice(kh * HEAD_DIM, (kh + 1) * HEAD_DIM)
        kr = _rope(p[:, Q_W + kh * HEAD_DIM:Q_W + (kh + 1) * HEAD_DIM], ch, slh, shh, half_h)
        v = p[:, Q_W + KV_W + kh * HEAD_DIM:Q_W + KV_W + (kh + 1) * HEAD_DIM]
        kb_ref[:, cols] = kr.astype(bf16)
        vb_ref[:, 2 * kh * HEAD_DIM:(2 * kh + 1) * HEAD_DIM] = v.astype(bf16)
        vb_ref[:, (2 * kh + 1) * HEAD_DIM:(2 * kh + 2) * HEAD_DIM] = one_hot
        head_rows = pl.ds(kh, TM, stride=N_KV_HEADS)

        for slot in range(kp_ref.shape[0]):
            @pl.when(i < P_TILES)
            def _(slot=slot):
                kp_ref[slot, head_rows, :] = kr
                vp_ref[slot, head_rows, :] = v

            @pl.when(i == P_TILES)
            def _(slot=slot):
                ks_ref[slot, head_rows, :] = kr
                vs_ref[slot, head_rows, :] = v
    ci, sli, shi = ri_ref[0], ri_ref[1], ri_ref[2]
    off = Q_W + 2 * KV_W
    for j in range(QI_W // LANES):
        cols = slice(j * LANES, (j + 1) * LANES)
        qi_ref[:, cols] = _rope(p[:, off + j * LANES:off + (j + 1) * LANES], ci, sli, shi, half_i).astype(bf16)
    kiwi = _rope(p[:, KIWI_OFF:KIWI_OFF + LANES], rk_ref[0], rk_ref[1], rk_ref[2], half_i)
    kiwi_ref[...] = kiwi
    kib_ref[...] = kiwi[:, :IDX_DIM].astype(bf16)
    wit_ref[...] = kiwi.T[IDX_DIM:IDX_DIM + N_IDX_HEADS, :]


def _proj(x, g, w, rope_h, rope_i, rope_k, layer, kv_prev):
    const2 = lambda i: (0, 0)
    rope_map = lambda i: (0, jnp.where(i < P_TILES, i % TILES_PER_SEQ, TILES_PER_SEQ), 0)
    row = lambda i: (i, 0)
    widths = [(Q_W, bf16), (KV_W, bf16), (2 * KV_W, bf16), (QI_W, bf16), (LANES, f32), (IDX_DIM, bf16)]
    n_layers = DEPTH // 2
    kv_rows = TM * N_KV_HEADS
    slots, first = (n_layers, 0) if kv_prev is None else (1, layer)
    kv_specs = [pl.BlockSpec((slots, kv_rows, HEAD_DIM), lambda i: (first, jnp.minimum(i, P_TILES - 1), 0)),
                pl.BlockSpec((slots, kv_rows, HEAD_DIM), lambda i: (first, 0, 0))] * 2
    kv_shapes = [jax.ShapeDtypeStruct((n_layers, P_ROWS * N_KV_HEADS, HEAD_DIM), f32),
                 jax.ShapeDtypeStruct((n_layers, S_ROWS * N_KV_HEADS, HEAD_DIM), f32)] * 2
    n_in = 6
    n_out = len(widths) + 1
    alias_in = [] if kv_prev is None else list(kv_prev)
    return pl.pallas_call(
        functools.partial(_proj_kernel, n_alias=len(alias_in)),
        grid=(N_TILES,),
        in_specs=_rows_specs(1, D_MODEL) + [
            pl.BlockSpec((1, D_MODEL), const2),
            pl.BlockSpec((D_MODEL, B_PROJ_PAD), const2),
            pl.BlockSpec((3, TM, LANES), rope_map),
            pl.BlockSpec((3, TM, LANES), rope_map),
            pl.BlockSpec((3, TM, LANES), rope_map),
        ] + [pl.BlockSpec(memory_space=pl.ANY)] * len(alias_in),
        out_specs=[pl.BlockSpec((TM, w_), row) for w_, _ in widths]
        + [pl.BlockSpec((N_IDX_HEADS, TM), lambda i: (0, i))] + kv_specs,
        out_shape=[jax.ShapeDtypeStruct((ROWS, w_), dt) for w_, dt in widths]
        + [jax.ShapeDtypeStruct((N_IDX_HEADS, ROWS), f32)] + kv_shapes,
        input_output_aliases={n_in + a: n_out + a for a in range(len(alias_in))},
        compiler_params=_params("arbitrary"),
        name="dsa_proj",
    )(x, g, w, rope_h, rope_i, rope_k, *alias_in)


def _order_key(sc, valid):
    bits = pltpu.bitcast(sc, jnp.int32)
    return jnp.where(valid, bits ^ ((bits >> 31) & 0x7FFFFFFF), INT_MIN)


def _count0(mask):
    return jnp.sum(jnp.where(mask, 1.0, 0.0), axis=0, keepdims=True)


def _kth_largest(key_ref, n_keys, k, hi_s, lo_s):
    i16 = jnp.int16
    half_min = -(2 ** 15)
    n_chunks = n_keys // BISECT_ROWS

    def chunk(c):
        return slice(c * BISECT_ROWS, (c + 1) * BISECT_ROWS)

    def count(ref, pred):
        acc = jnp.zeros((BISECT_ROWS, LANES), i16)
        for c in range(n_chunks):
            acc = jnp.where(pred(ref[chunk(c), :]), acc + jnp.ones((), i16), acc)
        return jnp.sum(acc.astype(jnp.int32), axis=0, keepdims=True)

    def bisect(ref, thr, n_bits, need):
        def step(j, thr):
            cand = thr + lax.shift_left(jnp.int32(1), n_bits - 1 - j)
            cand16 = cand.astype(i16)
            return jnp.where(count(ref, lambda blk: blk >= cand16) >= need, cand, thr)

        return lax.fori_loop(0, n_bits, step, thr)

    for c in range(n_chunks):
        key = key_ref[chunk(c), :]
        hi_s[chunk(c), :] = (key >> 16).astype(i16)
        lo_s[chunk(c), :] = ((key & 0xFFFF) + half_min).astype(i16)
    zero = jnp.zeros((1, LANES), jnp.int32)
    zero16 = zero.astype(i16)
    base = jnp.where(count(hi_s, lambda blk: blk >= zero16) >= k, zero, zero + half_min)
    t_hi = bisect(hi_s, base, 15, k)
    t_hi16 = t_hi.astype(i16)
    need_lo = k - count(hi_s, lambda blk: blk > t_hi16)
    for c in range(n_chunks):
        lo_s[chunk(c), :] = jnp.where(hi_s[chunk(c), :] == t_hi16, lo_s[chunk(c), :], jnp.full((), half_min, i16))
    t_lo = bisect(lo_s, zero + half_min, 16, need_lo)
    return t_hi * (2 ** 16) + (t_lo - half_min)


def _topk_bias_t(key, thr, k, out_ref):
    n_keys = key.shape[0]
    kf = float(k)
    thr = jnp.maximum(thr, INT_MIN + 1)
    ge = key >= thr
    excess = jnp.max(_count0(ge)) - kf

    @pl.when(excess <= 0.0)
    def _():
        out_ref[:n_keys, :] = jnp.where(ge, 0.0, -jnp.inf)

    @pl.when(excess > 0.0)
    def _():
        pos = lax.broadcasted_iota(jnp.int32, key.shape, 0)
        above = key > thr
        tie = key == thr
        need = kf - _count0(above)
        n_bits = max(1, (n_keys - 1).bit_length())

        def step(j, p):
            cand = p + lax.shift_left(jnp.int32(1), n_bits - 1 - j)
            return jnp.where(_count0(tie & (pos < cand)) < need, cand, p)

        last = lax.fori_loop(0, n_bits, step, jnp.zeros_like(thr))
        out_ref[:n_keys, :] = jnp.where(above | (tie & (pos <= last)), 0.0, -jnp.inf)


def _dsa_prompt_kernel(q_ref, qi_ref, wt_ref, k_ref, v_ref, ki_ref, o_ref,
                       key_s, hi_s, lo_s, thr_s, biast_s, bias_s):
    i = pl.program_id(1)
    k_top = min(TOPK_MAX, SEQ // 4)
    blocks_per_span = KV_SPAN // QUERY_BLOCK

    def run(span):
        n_keys = (span + 1) * KV_SPAN
        ki = ki_ref[:n_keys, :]
        sc = jnp.zeros((n_keys, QUERY_BLOCK), f32)
        for hd in range(N_IDX_HEADS):
            d = _dot_nt(ki, qi_ref[:, hd * IDX_DIM:(hd + 1) * IDX_DIM])
            sc = sc + jnp.maximum(d, 0.0) * wt_ref[hd:hd + 1, :]
        sc = sc * IDX_SCALE
        pos = lax.broadcasted_iota(jnp.int32, (n_keys, QUERY_BLOCK), 0)
        query = i * QUERY_BLOCK + lax.broadcasted_iota(jnp.int32, (n_keys, QUERY_BLOCK), 1)
        valid = pos <= query
        key_s[:n_keys, :] = _order_key(sc, valid)

        thr_s[...] = jnp.full(thr_s.shape, INT_MIN, jnp.int32)
        for blk in range(span * blocks_per_span, (span + 1) * blocks_per_span):
            if (blk + 1) * QUERY_BLOCK > k_top:
                @pl.when(i == blk)
                def _(blk=blk):
                    thr_s[0:1, :] = _kth_largest(key_s, (blk + 1) * QUERY_BLOCK, k_top, hi_s, lo_s)

        _topk_bias_t(key_s[:n_keys, :], thr_s[0:1, :], k_top, biast_s)
        bias_s[:, :n_keys] = biast_s[:n_keys, :].T
        bias = bias_s[:, :n_keys][None]
        for kh in range(N_KV_HEADS):
            q_rows = jnp.concatenate(
                [q_ref[:, (kh * Q_PER_KV + g) * HEAD_DIM:(kh * Q_PER_KV + g + 1) * HEAD_DIM]
                 for g in range(Q_PER_KV)], axis=0)
            s = _dot_nt(q_rows, k_ref[:n_keys, kh * HEAD_DIM:(kh + 1) * HEAD_DIM])
            s = (s.reshape(Q_PER_KV, QUERY_BLOCK, n_keys) + bias).reshape(Q_PER_KV * QUERY_BLOCK, n_keys)
            p = jnp.exp2((s - jnp.max(s, axis=1, keepdims=True)) * (ATTN_SCALE * LOG2_E))
            o = _dot(p.astype(bf16), v_ref[:n_keys, 2 * kh * HEAD_DIM:(2 * kh + 2) * HEAD_DIM])
            o = o[:, :HEAD_DIM] / o[:, HEAD_DIM:HEAD_DIM + 1]
            for g in range(Q_PER_KV):
                hd = kh * Q_PER_KV + g
                o_ref[:, hd * HEAD_DIM:(hd + 1) * HEAD_DIM] = o[g * QUERY_BLOCK:(g + 1) * QUERY_BLOCK].astype(bf16)

    for span in range(SEQ // KV_SPAN):
        pl.when(i // blocks_per_span == span)(functools.partial(run, span))


def _dsa_prompt(q, qi, wi_t, kb, vb, kib):
    nq = SEQ // QUERY_BLOCK
    qmap = lambda b, i: (b * nq + i, 0)
    kmap = lambda b, i: (b, 0)
    return pl.pallas_call(
        _dsa_prompt_kernel,
        grid=(BATCH, nq),
        in_specs=[
            pl.BlockSpec((QUERY_BLOCK, Q_W), qmap),
            pl.BlockSpec((QUERY_BLOCK, QI_W), qmap),
            pl.BlockSpec((N_IDX_HEADS, QUERY_BLOCK), lambda b, i: (0, b * nq + i)),
            pl.BlockSpec((SEQ, KV_W), kmap),
            pl.BlockSpec((SEQ, 2 * KV_W), kmap),
            pl.BlockSpec((SEQ, IDX_DIM), kmap),
        ],
        out_specs=pl.BlockSpec((QUERY_BLOCK, Q_W), qmap),
        out_shape=jax.ShapeDtypeStruct((P_ROWS, Q_W), bf16),
        scratch_shapes=[
            pltpu.VMEM((SEQ, QUERY_BLOCK), jnp.int32),
            pltpu.VMEM((SEQ, QUERY_BLOCK), jnp.int16),
            pltpu.VMEM((SEQ, QUERY_BLOCK), jnp.int16),
            pltpu.VMEM((SUBLANES, QUERY_BLOCK), jnp.int32),
            pltpu.VMEM((SEQ, QUERY_BLOCK), f32),
            pltpu.VMEM((QUERY_BLOCK, SEQ), f32),
        ],
        compiler_params=_params("arbitrary", "arbitrary"),
        name="dsa_prompt",
    )(q, qi, wi_t, kb, vb, kib)


def _sidx_kernel(pt_ref, qi_ref, w_ref, kin_ref, cki_hbm, sc_ref, pbuf, sem, *, first_page):
    step = pl.program_id(0)
    slot = step % 2

    def page_copies(st, sl):
        return [pltpu.make_async_copy(cki_hbm.at[first_page + pt_ref[st * SIDX_SEQS + g, j]], pbuf.at[sl, g, j],
                                      sem.at[sl])
                for g in range(SIDX_SEQS) for j in range(N_PAGES)]

    @pl.when(step == 0)
    def _():
        for c in page_copies(0, 0):
            c.start()

    @pl.when(step + 1 < pl.num_programs(0))
    def _():
        for c in page_copies(step + 1, 1 - slot):
            c.start()

    for c in page_copies(step, slot):
        c.wait()

    for g in range(SIDX_SEQS):
        qi = qi_ref[g]
        d = jnp.concatenate([_dot(qi, pbuf[slot, g, j].astype(bf16)) for j in range(N_PAGES)]
                            + [_dot_nt(qi, kin_ref[g])], axis=1)
        d = jnp.maximum(d, 0.0) * w_ref[g]
        sc_ref[g] = jnp.sum(d.reshape(DEC_SEQ, N_IDX_HEADS, S_KEYS), axis=1) * IDX_SCALE


def _sample_index_scores(page_table, qi32, w32, ki_new, cache_kidx, layer):
    n_pool = cache_kidx.shape[0] // (DEPTH // 2)
    seq3 = lambda n, pt: (n, 0, 0)
    grid_spec = pltpu.PrefetchScalarGridSpec(
        num_scalar_prefetch=1,
        grid=(DEC_BATCH // SIDX_SEQS,),
        in_specs=[
            pl.BlockSpec((SIDX_SEQS, DEC_SEQ * N_IDX_HEADS, IDX_DIM), seq3),
            pl.BlockSpec((SIDX_SEQS, DEC_SEQ * N_IDX_HEADS, 1), seq3),
            pl.BlockSpec((SIDX_SEQS, LANES, IDX_DIM), seq3),
            pl.BlockSpec(memory_space=pl.ANY),
        ],
        out_specs=pl.BlockSpec((SIDX_SEQS, DEC_SEQ, S_KEYS), seq3),
        scratch_shapes=[
            pltpu.VMEM((2, SIDX_SEQS, N_PAGES, IDX_DIM, PAGE_SIZE), f32),
            pltpu.SemaphoreType.DMA((2,)),
        ],
    )
    return pl.pallas_call(
        functools.partial(_sidx_kernel, first_page=layer * n_pool),
        grid_spec=grid_spec,
        out_shape=jax.ShapeDtypeStruct((DEC_BATCH, DEC_SEQ, S_KEYS), f32),
        compiler_params=_params("arbitrary"),
        name="dsa_sample_index",
    )(page_table, qi32, w32, ki_new, cache_kidx)


def _stopk_kernel(sc_ref, bias_ref, key_s, hi_s, lo_s):
    shape = sc_ref.shape
    pos = lax.broadcasted_iota(jnp.int32, shape, 0)
    t = (pl.program_id(0) * LANES + lax.broadcasted_iota(jnp.int32, shape, 1)) % DEC_SEQ
    valid = pos <= t + PAST_LEN
    k_top = min(TOPK_MAX, (PAST_LEN + DEC_SEQ) // 4)
    key_s[...] = _order_key(sc_ref[...], valid)
    thr = _kth_largest(key_s, S_KEYS, k_top, hi_s, lo_s)
    _topk_bias_t(key_s[...], thr, k_top, bias_ref)


def _sample_topk_bias(sc_t):
    spec = pl.BlockSpec((S_KEYS, LANES), lambda i: (0, i))
    return pl.pallas_call(
        _stopk_kernel,
        grid=(S_ROWS // LANES,),
        in_specs=[spec],
        out_specs=spec,
        out_shape=jax.ShapeDtypeStruct((S_KEYS, S_ROWS), f32),
        scratch_shapes=[pltpu.VMEM((S_KEYS, LANES), jnp.int32), pltpu.VMEM((S_KEYS, LANES), jnp.int16),
                        pltpu.VMEM((S_KEYS, LANES), jnp.int16)],
        compiler_params=_params("arbitrary"),
        name="dsa_sample_topk",
    )(sc_t)


def _satt_kernel(pt_ref, q_ref, bias_ref, kn_ref, vn_ref, ck_hbm, cv_hbm, o_ref,
                 kbuf, vbuf, sem, *, first_page):
    step = pl.program_id(0)
    slot = step % 2

    def page_copies(st, sl):
        copies = []
        for g in range(SATT_SEQS):
            for j in range(N_PAGES):
                page = first_page + pt_ref[st * SATT_SEQS + g, j]
                copies.append(pltpu.make_async_copy(ck_hbm.at[page], kbuf.at[sl, g, j], sem.at[sl, 0]))
                copies.append(pltpu.make_async_copy(cv_hbm.at[page], vbuf.at[sl, g, j], sem.at[sl, 1]))
        return copies

    @pl.when(step == 0)
    def _():
        for c in page_copies(0, 0):
            c.start()

    @pl.when(step + 1 < pl.num_programs(0))
    def _():
        for c in page_copies(step + 1, 1 - slot):
            c.start()

    for c in page_copies(step, slot):
        c.wait()

    for g in range(SATT_SEQS):
        for kh in range(N_KV_HEADS):
            cols = slice(kh * HEAD_DIM, (kh + 1) * HEAD_DIM)
            head_rows = pl.ds(kh, PAGE_SIZE, stride=N_KV_HEADS)
            k_all = jnp.concatenate([kbuf[slot, g, j, head_rows, :].astype(bf16) for j in range(N_PAGES)]
                                    + [kn_ref[g, :, cols]], axis=0)
            v_all = jnp.concatenate([vbuf[slot, g, j, head_rows, :].astype(bf16) for j in range(N_PAGES)]
                                    + [vn_ref[g, :, cols]], axis=0)
            s = _dot_nt(q_ref[g, kh], k_all) + bias_ref[g]
            p = jnp.exp2((s - jnp.max(s, axis=1, keepdims=True)) * (ATTN_SCALE * LOG2_E))
            denom = jnp.sum(p, axis=1, keepdims=True)
            o_ref[g, kh] = (_dot(p.astype(bf16), v_all) / denom).astype(bf16)


def _sample_attention(page_table, q16, bias16, k_new, v_new, cache_k, cache_v, layer):
    n_pool = cache_k.shape[0] // (DEPTH // 2)
    rows = DEC_SEQ * Q_PER_KV
    seq3 = lambda n, pt: (n, 0, 0)
    seq4 = lambda n, pt: (n, 0, 0, 0)
    page_shape = (PAGE_SIZE * N_KV_HEADS, HEAD_DIM)
    grid_spec = pltpu.PrefetchScalarGridSpec(
        num_scalar_prefetch=1,
        grid=(DEC_BATCH // SATT_SEQS,),
        in_specs=[
            pl.BlockSpec((SATT_SEQS, N_KV_HEADS, rows, HEAD_DIM), seq4),
            pl.BlockSpec((SATT_SEQS, rows, S_KEYS), seq3),
            pl.BlockSpec((SATT_SEQS, LANES, KV_W), seq3),
            pl.BlockSpec((SATT_SEQS, LANES, KV_W), seq3),
            pl.BlockSpec(memory_space=pl.ANY),
            pl.BlockSpec(memory_space=pl.ANY),
        ],
        out_specs=pl.BlockSpec((SATT_SEQS, N_KV_HEADS, rows, HEAD_DIM), seq4),
        scratch_shapes=[
            pltpu.VMEM((2, SATT_SEQS, N_PAGES) + page_shape, f32),
            pltpu.VMEM((2, SATT_SEQS, N_PAGES) + page_shape, f32),
            pltpu.SemaphoreType.DMA((2, 2)),
        ],
    )
    return pl.pallas_call(
        functools.partial(_satt_kernel, first_page=layer * n_pool),
        grid_spec=grid_spec,
        out_shape=jax.ShapeDtypeStruct((DEC_BATCH, N_KV_HEADS, rows, HEAD_DIM), bf16),
        compiler_params=_params("arbitrary"),
        name="dsa_sample_attn",
    )(page_table, q16, bias16, k_new, v_new, cache_k, cache_v)


def _rope_tables():
    pos = jnp.concatenate([jnp.arange(SEQ), PAST_LEN + jnp.arange(TM) % DEC_SEQ]).astype(f32)

    def pattern(dim, n_rot_groups):
        r = dim // ROT_DIV
        half = r // 2
        inv = ROPE_THETA ** (-jnp.arange(half, dtype=f32) * 2.0 / r)
        ang = pos[:, None] * inv[None, :]
        cos, sin = jnp.cos(ang), jnp.sin(ang)
        ones = jnp.ones((pos.shape[0], dim - r), f32)
        zeros = jnp.zeros((pos.shape[0], dim - r), f32)
        zh = jnp.zeros_like(sin)
        c = jnp.concatenate([cos, cos, ones], axis=1)
        lo = jnp.concatenate([-sin, zh, zeros], axis=1)
        hi = jnp.concatenate([zh, sin, zeros], axis=1)
        ident = (jnp.ones((pos.shape[0], dim), f32), jnp.zeros((pos.shape[0], dim), f32),
                 jnp.zeros((pos.shape[0], dim), f32))
        reps = LANES // dim
        parts = [(c, lo, hi) if j < n_rot_groups else ident for j in range(reps)]
        return jnp.stack([jnp.concatenate([p_[t] for p_ in parts], axis=1) for t in range(3)])

    return pattern(HEAD_DIM, 1), pattern(IDX_DIM, 2), pattern(IDX_DIM, 1)


def _sample_mix_tables(w_s, b_s):
    t = jnp.arange(SUBLANES) % DEC_SEQ
    coefs = []
    for d in range(DEC_SEQ):
        c = jnp.where((t - d >= 0)[None, :], w_s[:, t, jnp.maximum(t - d, 0)], 0.0)
        coefs.append(jnp.repeat(c.T, GROUP_DIM_A, axis=1))
    bias = jnp.repeat(b_s[:, t].T, GROUP_DIM_A, axis=1)
    return jnp.stack(coefs), bias


def kernel(x_prompt, x_sample, cache_k, cache_v, cache_kidx, page_table, norm_mix, norm_ffn,
           a_w_in, a_v_gain, a_w_s, a_b_s, a_w_out, b_w_in, b_w_out, ffn_w1, ffn_w2, norm_final):
    x = (x_prompt.reshape(P_ROWS, D_MODEL), x_sample.reshape(S_ROWS, D_MODEL))
    kv_new = None
    rope_h, rope_i, rope_k = _rope_tables()
    n_pool = cache_k.shape[1]
    ck = cache_k.reshape(-1, PAGE_SIZE * N_KV_HEADS, HEAD_DIM)
    cv = cache_v.reshape(-1, PAGE_SIZE * N_KV_HEADS, HEAD_DIM)
    cki = jnp.swapaxes(cache_kidx, 2, 3).reshape(-1, IDX_DIM, PAGE_SIZE)
    g_final = norm_final.reshape(1, D_MODEL)
    outs = {name: [] for name in ("kip", "kis", "cvp", "cvs")}

    for i in range(DEPTH):
        j = i // 2
        g_mix = norm_mix[i].reshape(1, D_MODEL)
        if i % 2 == 0:
            scoef, sbias = _sample_mix_tables(a_w_s[j], a_b_s[j])
            bias = jnp.repeat(a_b_s[j].T, GROUP_DIM_A, axis=1)
            *m, cvp, cvs = _mixa(x, g_mix, a_w_in[j].astype(bf16), a_v_gain[j].reshape(1, D_A), a_w_s[j],
                                 bias, scoef, sbias)
            outs["cvp"].append(cvp)
            outs["cvs"].append(cvs.reshape(DEC_BATCH, DEC_SEQ, D_A))
            w_out = a_w_out[j]
        else:
            w = jnp.pad(b_w_in[j], ((0, 0), (0, B_PROJ_PAD - B_PROJ))).astype(bf16)
            q, kb, vb, qi, kiwi, kib, wi_t, *kv_new = _proj(x[0], g_mix, w, rope_h, rope_i, rope_k, j, kv_new)
            m_p = _dsa_prompt(q, qi, wi_t, kb, vb, kib)
            qi32 = qi[P_ROWS:].reshape(DEC_BATCH, DEC_SEQ * N_IDX_HEADS, IDX_DIM)
            w32 = kiwi[P_ROWS:, IDX_DIM:IDX_DIM + N_IDX_HEADS].reshape(DEC_BATCH, DEC_SEQ * N_IDX_HEADS, 1)
            pad_rows = lambda a: jnp.pad(a.reshape(DEC_BATCH, DEC_SEQ, -1), ((0, 0), (0, LANES - DEC_SEQ), (0, 0)))
            sc = _sample_index_scores(page_table, qi32, w32, pad_rows(kib[P_ROWS:]), cki, j)
            bias = _sample_topk_bias(sc.reshape(S_ROWS, S_KEYS).T).T
            bias16 = jnp.tile(bias.reshape(DEC_BATCH, DEC_SEQ, S_KEYS), (1, Q_PER_KV, 1))
            q16 = q[P_ROWS:].reshape(DEC_BATCH, DEC_SEQ, N_KV_HEADS, Q_PER_KV, HEAD_DIM)
            q16 = q16.transpose(0, 2, 3, 1, 4).reshape(DEC_BATCH, N_KV_HEADS, Q_PER_KV * DEC_SEQ, HEAD_DIM)
            v_new = vb[P_ROWS:].reshape(S_ROWS, N_KV_HEADS, 2, HEAD_DIM)[:, :, 0]
            o16 = _sample_attention(page_table, q16, bias16, pad_rows(kb[P_ROWS:]), pad_rows(v_new), ck, cv, j)
            m_s = o16.reshape(DEC_BATCH, N_KV_HEADS, Q_PER_KV, DEC_SEQ, HEAD_DIM).transpose(0, 3, 1, 2, 4)
            m = (m_p, m_s.reshape(S_ROWS, Q_W))
            outs["kip"].append(kiwi[:P_ROWS, :IDX_DIM].reshape(BATCH, SEQ, IDX_DIM))
            outs["kis"].append(kiwi[P_ROWS:, :IDX_DIM].reshape(DEC_BATCH, DEC_SEQ, IDX_DIM))
            w_out = b_w_out[j]
        x = tuple(_post(x, tuple(m), w_out.astype(bf16), norm_ffn[i].reshape(1, D_MODEL), ffn_w1[i].astype(bf16),
                        ffn_w2[i].astype(bf16), g_final, final=(i == DEPTH - 1)))

    y_prompt = x[0].reshape(BATCH, SEQ, D_MODEL)
    y_sample = x[1].reshape(DEC_BATCH, DEC_SEQ, D_MODEL)
    n_layers = DEPTH // 2
    kp, ks, vp, vs = kv_new
    prompt_kv = lambda a: a.reshape(n_layers, BATCH, SEQ, N_KV_HEADS, HEAD_DIM)
    sample_kv = lambda a: a.reshape(n_layers, DEC_BATCH, DEC_SEQ, N_KV_HEADS, HEAD_DIM)
    st = jnp.stack
    return (y_prompt, y_sample, prompt_kv(kp), prompt_kv(vp), st(outs["kip"]), sample_kv(ks), sample_kv(vs),
            st(outs["kis"]), st(outs["cvp"]), st(outs["cvs"]))
```

```python
import functools

import jax
import jax.numpy as jnp
from jax import lax
from jax.experimental import pallas as pl
from jax.experimental.pallas import tpu as pltpu

D_MODEL = 1024
BATCH = 8
SEQ = 2048
DEPTH = 4
DEC_BATCH = 128
DEC_SEQ = 4
PAST_LEN = 2048
PAGE_SIZE = 128
N_PAGES = PAST_LEN // PAGE_SIZE
CHUNK = 128
D_A = D_MODEL
N_GROUPS_A = 8
GROUP_DIM_A = D_A // N_GROUPS_A
N_HEADS = 8
HEAD_DIM = D_MODEL // N_HEADS
N_KV_HEADS = 2
Q_PER_KV = N_HEADS // N_KV_HEADS
N_IDX_HEADS = 8
IDX_DIM = 64
TOPK_MAX = 256
QUERY_BLOCK = 128
ROPE_THETA = 500000.0
ROT_DIV = 4
IDX_SCALE = float((IDX_DIM * N_IDX_HEADS) ** -0.5)
ATTN_SCALE = float(HEAD_DIM ** -0.5)
Q_W = N_HEADS * HEAD_DIM
KV_W = N_KV_HEADS * HEAD_DIM
QI_W = N_IDX_HEADS * IDX_DIM
B_PROJ = Q_W + 2 * KV_W + QI_W + IDX_DIM + N_IDX_HEADS
D_FF = 4 * D_MODEL
EPS = 1e-6

LANES = 128
SUBLANES = 8
P_ROWS = BATCH * SEQ
S_ROWS = DEC_BATCH * DEC_SEQ
ROWS = P_ROWS + S_ROWS
TM = 512
N_TILES = ROWS // TM
P_TILES = P_ROWS // TM
TILES_PER_SEQ = SEQ // TM
CHUNKS_PER_TILE = TM // CHUNK
FF_TILE = 1024
B_PROJ_PAD = Q_W + 2 * KV_W + QI_W + LANES
KIWI_OFF = Q_W + 2 * KV_W + QI_W
S_KEYS = PAST_LEN + LANES
KV_SPAN = 512
VMEM_LIMIT = 56 * 1024 * 1024
INT_MIN = -(2 ** 31)
F32_LOWEST = -3.4028234663852886e38
LOG2_E = 1.4426950408889634
SATT_SEQS = 2
SIDX_SEQS = 4
BISECT_ROWS = 64

f32 = jnp.float32
bf16 = jnp.bfloat16

assert S_ROWS == TM and P_ROWS % TM == 0 and SEQ % TM == 0


def _params(*sem):
    return pltpu.CompilerParams(dimension_semantics=sem, vmem_limit_bytes=VMEM_LIMIT)


def _rms(x, g):
    return x * lax.rsqrt(jnp.mean(x * x, axis=-1, keepdims=True) + EPS) * g


def _gelu(x):
    a = -2.0 * LOG2_E * 0.7978845608028654
    return x / (1.0 + jnp.exp2(x * (a + (a * 0.044715) * (x * x))))


def _dot(a, b):
    return jnp.dot(a, b, preferred_element_type=f32)


def _dot_nt(a, b):
    return lax.dot_general(a, b, (((1,), (1,)), ((), ())), preferred_element_type=f32)


def _prompt_tile(i):
    return (jnp.minimum(i, P_TILES - 1), 0)


def _first_tile(i):
    return (0, 0)


def _pair_specs(width):
    return [pl.BlockSpec((TM, width), _prompt_tile), pl.BlockSpec((S_ROWS, width), _first_tile)]


def _layer_spec(shape, layer, **kwargs):
    return pl.BlockSpec((None,) + shape, lambda i: (layer, 0, 0), **kwargs)


def _rows_specs(n_arrays, width):
    return [pl.BlockSpec((TM, width), lambda i: (i, 0))] if n_arrays == 1 else _pair_specs(width)


def _rows_tile(i, refs):
    if len(refs) == 1:
        return refs[0][...]
    p_ref, s_ref = refs
    return jnp.where(i == P_TILES, s_ref[...], p_ref[...])


def _store_rows_tile(i, refs, value):
    if len(refs) == 1:
        refs[0][...] = value
        return
    p_ref, s_ref = refs

    @pl.when(i < P_TILES)
    def _():
        p_ref[...] = value

    @pl.when(i == P_TILES)
    def _():
        s_ref[...] = value


def _mixa_kernel(*refs, n_x):
    x_refs = refs[:n_x]
    g_ref, win_ref, gain_ref, ws_ref, bias_ref, scoef_ref, sbias_ref, m_ref, cvp_ref, cvs_ref = refs[n_x:]
    i = pl.program_id(0)
    h = _rms(_rows_tile(i, x_refs), g_ref[...]).astype(bf16)
    uv = _gelu(_dot(h, win_ref[...]))
    u = uv[:, :D_A]
    v = uv[:, D_A:]
    vc = v - jnp.mean(v, axis=-1, keepdims=True)
    vn = vc * lax.rsqrt(jnp.mean(vc * vc, axis=-1, keepdims=True) + EPS) * gain_ref[...]

    @pl.when(i < P_TILES)
    def _prompt():
        t_idx = lax.broadcasted_iota(jnp.int32, (CHUNK, CHUNK), 0)
        s_idx = lax.broadcasted_iota(jnp.int32, (CHUNK, CHUNK), 1)
        causal = s_idx <= t_idx
        ws = [jnp.where(causal, ws_ref[g], 0.0).astype(bf16) for g in range(N_GROUPS_A)]
        for c in range(CHUNKS_PER_TILE):
            rows = slice(c * CHUNK, (c + 1) * CHUNK)
            v_c = vn[rows].astype(bf16)
            mixed = jnp.concatenate(
                [_dot(ws[g], v_c[:, g * GROUP_DIM_A:(g + 1) * GROUP_DIM_A]) for g in range(N_GROUPS_A)],
                axis=1) + bias_ref[...]
            m_ref[rows, :] = (u[rows] * mixed).astype(bf16)

        @pl.when(i % TILES_PER_SEQ == TILES_PER_SEQ - 1)
        def _():
            cvp_ref[0] = vn[TM - CHUNK:]

    @pl.when(i == P_TILES)
    def _sample():
        grp = TM // SUBLANES
        v3 = vn.reshape(grp, SUBLANES, D_A)
        mixed = sbias_ref[...][None] + scoef_ref[0][None] * v3
        for d in range(1, DEC_SEQ):
            mixed = mixed + scoef_ref[d][None] * pltpu.roll(v3, d, 1)
        m_ref[...] = (u.reshape(grp, SUBLANES, D_A) * mixed).reshape(TM, D_A).astype(bf16)
        cvs_ref[...] = vn


def _mixa(x, g, w_in, layer, gain, ws, bias, scoef, sbias):
    const2 = lambda i: (0, 0)
    const3 = lambda i: (0, 0, 0)
    return pl.pallas_call(
        functools.partial(_mixa_kernel, n_x=len(x)),
        grid=(N_TILES,),
        in_specs=_rows_specs(len(x), D_MODEL) + [
            pl.BlockSpec((1, D_MODEL), const2),
            _layer_spec((D_MODEL, 2 * D_A), layer),
            pl.BlockSpec((1, D_A), const2),
            pl.BlockSpec((N_GROUPS_A, CHUNK, CHUNK), const3),
            pl.BlockSpec((CHUNK, D_A), const2),
            pl.BlockSpec((DEC_SEQ, SUBLANES, D_A), const3),
            pl.BlockSpec((SUBLANES, D_A), const2),
        ],
        out_specs=[
            pl.BlockSpec((TM, D_A), lambda i: (i, 0)),
            pl.BlockSpec((1, CHUNK, D_A), lambda i: (jnp.minimum(i // TILES_PER_SEQ, BATCH - 1), 0, 0)),
            pl.BlockSpec((S_ROWS, D_A), const2),
        ],
        out_shape=[
            jax.ShapeDtypeStruct((ROWS, D_A), bf16),
            jax.ShapeDtypeStruct((BATCH, CHUNK, D_A), f32),
            jax.ShapeDtypeStruct((S_ROWS, D_A), f32),
        ],
        compiler_params=_params("arbitrary"),
        name="mixer_a",
    )(*x, g, w_in, gain, ws, bias, scoef, sbias)


def _post_kernel(*refs, n_x, n_m, final):
    x_refs, m_refs = refs[:n_x], refs[n_x:n_x + n_m]
    wout_ref, g_ref, w1_ref, w2_ref, gf_ref = refs[n_x + n_m:n_x + n_m + 5]
    y_refs = refs[n_x + n_m + 5:]
    i = pl.program_id(0)
    x1 = _rows_tile(i, x_refs) + _dot(_rows_tile(i, m_refs), wout_ref[...])
    h = _rms(x1, g_ref[...]).astype(bf16)
    y = x1
    for c in range(D_FF // FF_TILE):
        a = jnp.maximum(_dot(h, w1_ref[:, c * FF_TILE:(c + 1) * FF_TILE]), 0.0)
        y = y + _dot((a * a).astype(bf16), w2_ref[c * FF_TILE:(c + 1) * FF_TILE, :])
    if final:
        y = _rms(y, gf_ref[...])
    _store_rows_tile(i, y_refs, y)


def _post(x, m, w_out, mixer_layer, g, w1, w2, layer, g_final, final):
    const2 = lambda i: (0, 0)
    resident = dict(pipeline_mode=pl.Buffered(1))
    if final:
        out_specs = _pair_specs(D_MODEL)
        out_shape = [jax.ShapeDtypeStruct((P_ROWS, D_MODEL), f32), jax.ShapeDtypeStruct((S_ROWS, D_MODEL), f32)]
    else:
        out_specs = _rows_specs(1, D_MODEL)
        out_shape = [jax.ShapeDtypeStruct((ROWS, D_MODEL), f32)]
    return pl.pallas_call(
        functools.partial(_post_kernel, n_x=len(x), n_m=len(m), final=final),
        grid=(N_TILES,),
        in_specs=_rows_specs(len(x), D_MODEL) + _rows_specs(len(m), D_MODEL) + [
            _layer_spec((D_MODEL, D_MODEL), mixer_layer, **resident),
            pl.BlockSpec((1, D_MODEL), const2),
            _layer_spec((D_MODEL, D_FF), layer, **resident),
            _layer_spec((D_FF, D_MODEL), layer, **resident),
            pl.BlockSpec((1, D_MODEL), const2),
        ],
        out_specs=out_specs,
        out_shape=out_shape,
        compiler_params=_params("arbitrary"),
        name="post_ffn",
    )(*x, *m, w_out, g, w1, w2, g_final)


def _rope(x, cos, sin_lo, sin_hi, half):
    return x * cos + pltpu.roll(x, LANES - half, 1) * sin_lo + pltpu.roll(x, half, 1) * sin_hi


def _proj_kernel(*refs, n_alias):
    x_ref, g_ref, w_ref, rh_ref, ri_ref, rk_ref = refs[:6]
    (q_ref, kb_ref, vb_ref, qi_ref, kiwi_ref, kib_ref, wit_ref,
     kp_ref, ks_ref, vp_ref, vs_ref) = refs[6 + n_alias:]
    i = pl.program_id(0)
    h = _rms(x_ref[...], g_ref[...]).astype(bf16)
    p = _dot(h, w_ref[...])
    half_h = HEAD_DIM // ROT_DIV // 2
    half_i = IDX_DIM // ROT_DIV // 2
    ch, slh, shh = rh_ref[0], rh_ref[1], rh_ref[2]
    for hd in range(N_HEADS):
        cols = slice(hd * HEAD_DIM, (hd + 1) * HEAD_DIM)
        q_ref[:, cols] = (_rope(p[:, cols], ch, slh, shh, half_h) * (ATTN_SCALE * LOG2_E)).astype(bf16)
    one_hot = jnp.where(lax.broadcasted_iota(jnp.int32, (TM, HEAD_DIM), 1) == 0, 1.0, 0.0).astype(bf16)
    for kh in range(N_KV_HEADS):
        cols = slice(kh * HEAD_DIM, (kh + 1) * HEAD_DIM)
        kr = _rope(p[:, Q_W + kh * HEAD_DIM:Q_W + (kh + 1) * HEAD_DIM], ch, slh, shh, half_h)
        v = p[:, Q_W + KV_W + kh * HEAD_DIM:Q_W + KV_W + (kh + 1) * HEAD_DIM]
        kb_ref[:, cols] = kr.astype(bf16)
        vb_ref[:, 2 * kh * HEAD_DIM:(2 * kh + 1) * HEAD_DIM] = v.astype(bf16)
        vb_ref[:, (2 * kh + 1) * HEAD_DIM:(2 * kh + 2) * HEAD_DIM] = one_hot
        head_rows = pl.ds(kh, TM, stride=N_KV_HEADS)

        for slot in range(kp_ref.shape[0]):
            @pl.when(i < P_TILES)
            def _(slot=slot):
                kp_ref[slot, head_rows, :] = kr
                vp_ref[slot, head_rows, :] = v

            @pl.when(i == P_TILES)
            def _(slot=slot):
                ks_ref[slot, head_rows, :] = kr
                vs_ref[slot, head_rows, :] = v
    ci, sli, shi = ri_ref[0], ri_ref[1], ri_ref[2]
    off = Q_W + 2 * KV_W
    for j in range(QI_W // LANES):
        cols = slice(j * LANES, (j + 1) * LANES)
        qi_ref[:, cols] = _rope(p[:, off + j * LANES:off + (j + 1) * LANES], ci, sli, shi, half_i).astype(bf16)
    kiwi = _rope(p[:, KIWI_OFF:KIWI_OFF + LANES], rk_ref[0], rk_ref[1], rk_ref[2], half_i)
    kiwi_ref[...] = kiwi
    kib_ref[...] = kiwi[:, :IDX_DIM].astype(bf16)
    wit_ref[...] = kiwi.T[IDX_DIM:IDX_DIM + N_IDX_HEADS, :]


def _proj(x, g, w, rope_h, rope_i, rope_k, layer, kv_prev):
    const2 = lambda i: (0, 0)
    rope_map = lambda i: (0, jnp.where(i < P_TILES, i % TILES_PER_SEQ, TILES_PER_SEQ), 0)
    row = lambda i: (i, 0)
    widths = [(Q_W, bf16), (KV_W, bf16), (2 * KV_W, bf16), (QI_W, bf16), (LANES, f32), (IDX_DIM, bf16)]
    n_layers = DEPTH // 2
    kv_rows = TM * N_KV_HEADS
    slots, first = (n_layers, 0) if kv_prev is None else (1, layer)
    kv_specs = [pl.BlockSpec((slots, kv_rows, HEAD_DIM), lambda i: (first, jnp.minimum(i, P_TILES - 1), 0)),
                pl.BlockSpec((slots, kv_rows, HEAD_DIM), lambda i: (first, 0, 0))] * 2
    kv_shapes = [jax.ShapeDtypeStruct((n_layers, P_ROWS * N_KV_HEADS, HEAD_DIM), f32),
                 jax.ShapeDtypeStruct((n_layers, S_ROWS * N_KV_HEADS, HEAD_DIM), f32)] * 2
    n_in = 6
    n_out = len(widths) + 1
    alias_in = [] if kv_prev is None else list(kv_prev)
    return pl.pallas_call(
        functools.partial(_proj_kernel, n_alias=len(alias_in)),
        grid=(N_TILES,),
        in_specs=_rows_specs(1, D_MODEL) + [
            pl.BlockSpec((1, D_MODEL), const2),
            _layer_spec((D_MODEL, B_PROJ_PAD), layer),
            pl.BlockSpec((3, TM, LANES), rope_map),
            pl.BlockSpec((3, TM, LANES), rope_map),
            pl.BlockSpec((3, TM, LANES), rope_map),
        ] + [pl.BlockSpec(memory_space=pl.ANY)] * len(alias_in),
        out_specs=[pl.BlockSpec((TM, w_), row) for w_, _ in widths]
        + [pl.BlockSpec((N_IDX_HEADS, TM), lambda i: (0, i))] + kv_specs,
        out_shape=[jax.ShapeDtypeStruct((ROWS, w_), dt) for w_, dt in widths]
        + [jax.ShapeDtypeStruct((N_IDX_HEADS, ROWS), f32)] + kv_shapes,
        input_output_aliases={n_in + a: n_out + a for a in range(len(alias_in))},
        compiler_params=_params("arbitrary"),
        name="dsa_proj",
    )(x, g, w, rope_h, rope_i, rope_k, *alias_in)


def _order_float(key):
    return pltpu.bitcast(key ^ ((key >> 31) & 0x7FFFFFFF), f32)


def _count0(mask):
    return jnp.sum(jnp.where(mask, 1.0, 0.0), axis=0, keepdims=True)


def _kth_largest(sc_ref, n_keys, k):
    kf = float(k)

    def count_ge(cand_key):
        cand = _order_float(cand_key)
        acc = jnp.zeros((BISECT_ROWS, LANES), f32)
        for c in range(n_keys // BISECT_ROWS):
            acc = jnp.where(sc_ref[c * BISECT_ROWS:(c + 1) * BISECT_ROWS, :] >= cand, acc + 1.0, acc)
        return jnp.sum(acc, axis=0, keepdims=True)

    zero = jnp.zeros((1, LANES), jnp.int32)
    thr = jnp.where(count_ge(zero) >= kf, zero, zero + INT_MIN)

    def bit_step(j, thr):
        cand = thr | lax.shift_left(jnp.int32(1), 30 - j)
        return jnp.where(count_ge(cand) >= kf, cand, thr)

    thr = _order_float(lax.fori_loop(0, 31, bit_step, thr))
    return jnp.where(thr >= F32_LOWEST, thr, F32_LOWEST)


def _topk_bias_t(sc, thr, k, out_ref):
    n_keys = sc.shape[0]
    kf = float(k)
    ge = sc >= thr
    excess = jnp.max(_count0(ge)) - kf

    @pl.when(excess <= 0.0)
    def _():
        out_ref[:n_keys, :] = jnp.where(ge, 0.0, -jnp.inf)

    @pl.when(excess > 0.0)
    def _():
        pos = lax.broadcasted_iota(jnp.int32, sc.shape, 0)
        above = sc > thr
        tie = sc == thr
        need = kf - _count0(above)
        n_bits = max(1, (n_keys - 1).bit_length())

        def step(j, p):
            cand = p + lax.shift_left(jnp.int32(1), n_bits - 1 - j)
            return jnp.where(_count0(tie & (pos < cand)) < need, cand, p)

        last = lax.fori_loop(0, n_bits, step, jnp.zeros(thr.shape, jnp.int32))
        out_ref[:n_keys, :] = jnp.where(above | (tie & (pos <= last)), 0.0, -jnp.inf)


def _dsa_prompt_kernel(q_ref, qi_ref, wt_ref, k_ref, v_ref, ki_ref, o_ref, key_s, thr_s, biast_s, bias_s):
    i = pl.program_id(1)
    k_top = min(TOPK_MAX, SEQ // 4)
    blocks_per_span = KV_SPAN // QUERY_BLOCK

    def run(span):
        n_keys = (span + 1) * KV_SPAN
        ki = ki_ref[:n_keys, :]
        sc = jnp.zeros((n_keys, QUERY_BLOCK), f32)
        for hd in range(N_IDX_HEADS):
            d = _dot_nt(ki, qi_ref[:, hd * IDX_DIM:(hd + 1) * IDX_DIM])
            sc = sc + jnp.maximum(d, 0.0) * wt_ref[hd:hd + 1, :]
        sc = sc * IDX_SCALE
        pos = lax.broadcasted_iota(jnp.int32, (n_keys, QUERY_BLOCK), 0)
        query = i * QUERY_BLOCK + lax.broadcasted_iota(jnp.int32, (n_keys, QUERY_BLOCK), 1)
        valid = pos <= query
        key_s[:n_keys, :] = jnp.where(valid, sc, -jnp.inf)

        thr_s[...] = jnp.full(thr_s.shape, F32_LOWEST, f32)
        for blk in range(span * blocks_per_span, (span + 1) * blocks_per_span):
            if (blk + 1) * QUERY_BLOCK > k_top:
                @pl.when(i == blk)
                def _(blk=blk):
                    thr_s[0:1, :] = _kth_largest(key_s, (blk + 1) * QUERY_BLOCK, k_top)

        _topk_bias_t(key_s[:n_keys, :], thr_s[0:1, :], k_top, biast_s)
        bias_s[:, :n_keys] = biast_s[:n_keys, :].T
        bias = bias_s[:, :n_keys][None]
        for kh in range(N_KV_HEADS):
            q_rows = jnp.concatenate(
                [q_ref[:, (kh * Q_PER_KV + g) * HEAD_DIM:(kh * Q_PER_KV + g + 1) * HEAD_DIM]
                 for g in range(Q_PER_KV)], axis=0)
            s = _dot_nt(q_rows, k_ref[:n_keys, kh * HEAD_DIM:(kh + 1) * HEAD_DIM])
            s = (s.reshape(Q_PER_KV, QUERY_BLOCK, n_keys) + bias).reshape(Q_PER_KV * QUERY_BLOCK, n_keys)
            p = jnp.exp2(s - jnp.max(s, axis=1, keepdims=True))
            o = _dot(p.astype(bf16), v_ref[:n_keys, 2 * kh * HEAD_DIM:(2 * kh + 2) * HEAD_DIM])
            o = o[:, :HEAD_DIM] / o[:, HEAD_DIM:HEAD_DIM + 1]
            for g in range(Q_PER_KV):
                hd = kh * Q_PER_KV + g
                o_ref[:, hd * HEAD_DIM:(hd + 1) * HEAD_DIM] = o[g * QUERY_BLOCK:(g + 1) * QUERY_BLOCK].astype(bf16)

    for span in range(SEQ // KV_SPAN):
        pl.when(i // blocks_per_span == span)(functools.partial(run, span))


def _dsa_prompt(q, qi, wi_t, kb, vb, kib):
    nq = SEQ // QUERY_BLOCK
    qmap = lambda b, i: (b * nq + i, 0)
    kmap = lambda b, i: (b, 0)
    return pl.pallas_call(
        _dsa_prompt_kernel,
        grid=(BATCH, nq),
        in_specs=[
            pl.BlockSpec((QUERY_BLOCK, Q_W), qmap),
            pl.BlockSpec((QUERY_BLOCK, QI_W), qmap),
            pl.BlockSpec((N_IDX_HEADS, QUERY_BLOCK), lambda b, i: (0, b * nq + i)),
            pl.BlockSpec((SEQ, KV_W), kmap),
            pl.BlockSpec((SEQ, 2 * KV_W), kmap),
            pl.BlockSpec((SEQ, IDX_DIM), kmap),
        ],
        out_specs=pl.BlockSpec((QUERY_BLOCK, Q_W), qmap),
        out_shape=jax.ShapeDtypeStruct((P_ROWS, Q_W), bf16),
        scratch_shapes=[
            pltpu.VMEM((SEQ, QUERY_BLOCK), f32),
            pltpu.VMEM((SUBLANES, QUERY_BLOCK), f32),
            pltpu.VMEM((SEQ, QUERY_BLOCK), f32),
            pltpu.VMEM((QUERY_BLOCK, SEQ), f32),
        ],
        compiler_params=_params("arbitrary", "arbitrary"),
        name="dsa_prompt",
    )(q, qi, wi_t, kb, vb, kib)


def _sidx_kernel(pt_ref, qi_ref, w_ref, kin_ref, cki_hbm, sc_ref, pbuf, sem, *, first_page):
    step = pl.program_id(0)
    slot = step % 2

    def page_copies(st, sl):
        return [pltpu.make_async_copy(cki_hbm.at[first_page + pt_ref[st * SIDX_SEQS + g, j]], pbuf.at[sl, g, j],
                                      sem.at[sl])
                for g in range(SIDX_SEQS) for j in range(N_PAGES)]

    @pl.when(step == 0)
    def _():
        for c in page_copies(0, 0):
            c.start()

    @pl.when(step + 1 < pl.num_programs(0))
    def _():
        for c in page_copies(step + 1, 1 - slot):
            c.start()

    for c in page_copies(step, slot):
        c.wait()

    for g in range(SIDX_SEQS):
        qi = qi_ref[g]
        d = jnp.concatenate([_dot(qi, pbuf[slot, g, j].astype(bf16)) for j in range(N_PAGES)]
                            + [_dot_nt(qi, kin_ref[g])], axis=1)
        d = jnp.maximum(d, 0.0) * w_ref[g]
        sc_ref[g] = jnp.sum(d.reshape(DEC_SEQ, N_IDX_HEADS, S_KEYS), axis=1) * IDX_SCALE


def _sample_index_scores(page_table, qi32, w32, ki_new, cache_kidx, layer):
    n_pool = cache_kidx.shape[0] // (DEPTH // 2)
    seq3 = lambda n, pt: (n, 0, 0)
    grid_spec = pltpu.PrefetchScalarGridSpec(
        num_scalar_prefetch=1,
        grid=(DEC_BATCH // SIDX_SEQS,),
        in_specs=[
            pl.BlockSpec((SIDX_SEQS, DEC_SEQ * N_IDX_HEADS, IDX_DIM), seq3),
            pl.BlockSpec((SIDX_SEQS, DEC_SEQ * N_IDX_HEADS, 1), seq3),
            pl.BlockSpec((SIDX_SEQS, LANES, IDX_DIM), seq3),
            pl.BlockSpec(memory_space=pl.ANY),
        ],
        out_specs=pl.BlockSpec((SIDX_SEQS, DEC_SEQ, S_KEYS), seq3),
        scratch_shapes=[
            pltpu.VMEM((2, SIDX_SEQS, N_PAGES, IDX_DIM, PAGE_SIZE), f32),
            pltpu.SemaphoreType.DMA((2,)),
        ],
    )
    return pl.pallas_call(
        functools.partial(_sidx_kernel, first_page=layer * n_pool),
        grid_spec=grid_spec,
        out_shape=jax.ShapeDtypeStruct((DEC_BATCH, DEC_SEQ, S_KEYS), f32),
        compiler_params=_params("arbitrary"),
        name="dsa_sample_index",
    )(page_table, qi32, w32, ki_new, cache_kidx)


def _stopk_kernel(sc_ref, bias_ref, key_s):
    shape = sc_ref.shape
    pos = lax.broadcasted_iota(jnp.int32, shape, 0)
    t = (pl.program_id(0) * LANES + lax.broadcasted_iota(jnp.int32, shape, 1)) % DEC_SEQ
    valid = pos <= t + PAST_LEN
    k_top = min(TOPK_MAX, (PAST_LEN + DEC_SEQ) // 4)
    key_s[...] = jnp.where(valid, sc_ref[...], -jnp.inf)
    thr = _kth_largest(key_s, S_KEYS, k_top)
    _topk_bias_t(key_s[...], thr, k_top, bias_ref)


def _sample_topk_bias(sc_t):
    spec = pl.BlockSpec((S_KEYS, LANES), lambda i: (0, i))
    return pl.pallas_call(
        _stopk_kernel,
        grid=(S_ROWS // LANES,),
        in_specs=[spec],
        out_specs=spec,
        out_shape=jax.ShapeDtypeStruct((S_KEYS, S_ROWS), f32),
        scratch_shapes=[pltpu.VMEM((S_KEYS, LANES), f32)],
        compiler_params=_params("arbitrary"),
        name="dsa_sample_topk",
    )(sc_t)


def _satt_kernel(pt_ref, q_ref, bias_ref, kn_ref, vn_ref, ck_hbm, cv_hbm, o_ref,
                 kbuf, vbuf, sem, *, first_page):
    step = pl.program_id(0)
    slot = step % 2

    def page_copies(st, sl):
        copies = []
        for g in range(SATT_SEQS):
            for j in range(N_PAGES):
                page = first_page + pt_ref[st * SATT_SEQS + g, j]
                copies.append(pltpu.make_async_copy(ck_hbm.at[page], kbuf.at[sl, g, j], sem.at[sl, 0]))
                copies.append(pltpu.make_async_copy(cv_hbm.at[page], vbuf.at[sl, g, j], sem.at[sl, 1]))
        return copies

    @pl.when(step == 0)
    def _():
        for c in page_copies(0, 0):
            c.start()

    @pl.when(step + 1 < pl.num_programs(0))
    def _():
        for c in page_copies(step + 1, 1 - slot):
            c.start()

    for c in page_copies(step, slot):
        c.wait()

    for g in range(SATT_SEQS):
        for kh in range(N_KV_HEADS):
            cols = slice(kh * HEAD_DIM, (kh + 1) * HEAD_DIM)
            head_rows = pl.ds(kh, PAGE_SIZE, stride=N_KV_HEADS)
            k_all = jnp.concatenate([kbuf[slot, g, j, head_rows, :].astype(bf16) for j in range(N_PAGES)]
                                    + [kn_ref[g, :, cols]], axis=0)
            v_all = jnp.concatenate([vbuf[slot, g, j, head_rows, :].astype(bf16) for j in range(N_PAGES)]
                                    + [vn_ref[g, :, cols]], axis=0)
            s = _dot_nt(q_ref[g, kh], k_all) + bias_ref[g]
            p = jnp.exp2(s - jnp.max(s, axis=1, keepdims=True))
            denom = jnp.sum(p, axis=1, keepdims=True)
            o_ref[g, kh] = (_dot(p.astype(bf16), v_all) / denom).astype(bf16)


def _sample_attention(page_table, q16, bias16, k_new, v_new, cache_k, cache_v, layer):
    n_pool = cache_k.shape[0] // (DEPTH // 2)
    rows = DEC_SEQ * Q_PER_KV
    seq3 = lambda n, pt: (n, 0, 0)
    seq4 = lambda n, pt: (n, 0, 0, 0)
    page_shape = (PAGE_SIZE * N_KV_HEADS, HEAD_DIM)
    grid_spec = pltpu.PrefetchScalarGridSpec(
        num_scalar_prefetch=1,
        grid=(DEC_BATCH // SATT_SEQS,),
        in_specs=[
            pl.BlockSpec((SATT_SEQS, N_KV_HEADS, rows, HEAD_DIM), seq4),
            pl.BlockSpec((SATT_SEQS, rows, S_KEYS), seq3),
            pl.BlockSpec((SATT_SEQS, LANES, KV_W), seq3),
            pl.BlockSpec((SATT_SEQS, LANES, KV_W), seq3),
            pl.BlockSpec(memory_space=pl.ANY),
            pl.BlockSpec(memory_space=pl.ANY),
        ],
        out_specs=pl.BlockSpec((SATT_SEQS, N_KV_HEADS, rows, HEAD_DIM), seq4),
        scratch_shapes=[
            pltpu.VMEM((2, SATT_SEQS, N_PAGES) + page_shape, f32),
            pltpu.VMEM((2, SATT_SEQS, N_PAGES) + page_shape, f32),
            pltpu.SemaphoreType.DMA((2, 2)),
        ],
    )
    return pl.pallas_call(
        functools.partial(_satt_kernel, first_page=layer * n_pool),
        grid_spec=grid_spec,
        out_shape=jax.ShapeDtypeStruct((DEC_BATCH, N_KV_HEADS, rows, HEAD_DIM), bf16),
        compiler_params=_params("arbitrary"),
        name="dsa_sample_attn",
    )(page_table, q16, bias16, k_new, v_new, cache_k, cache_v)


def _rope_tables():
    pos = jnp.concatenate([jnp.arange(SEQ), PAST_LEN + jnp.arange(TM) % DEC_SEQ]).astype(f32)

    def pattern(dim, n_rot_groups):
        r = dim // ROT_DIV
        half = r // 2
        inv = ROPE_THETA ** (-jnp.arange(half, dtype=f32) * 2.0 / r)
        ang = pos[:, None] * inv[None, :]
        cos, sin = jnp.cos(ang), jnp.sin(ang)
        ones = jnp.ones((pos.shape[0], dim - r), f32)
        zeros = jnp.zeros((pos.shape[0], dim - r), f32)
        zh = jnp.zeros_like(sin)
        c = jnp.concatenate([cos, cos, ones], axis=1)
        lo = jnp.concatenate([-sin, zh, zeros], axis=1)
        hi = jnp.concatenate([zh, sin, zeros], axis=1)
        ident = (jnp.ones((pos.shape[0], dim), f32), jnp.zeros((pos.shape[0], dim), f32),
                 jnp.zeros((pos.shape[0], dim), f32))
        reps = LANES // dim
        parts = [(c, lo, hi) if j < n_rot_groups else ident for j in range(reps)]
        return jnp.stack([jnp.concatenate([p_[t] for p_ in parts], axis=1) for t in range(3)])

    return pattern(HEAD_DIM, 1), pattern(IDX_DIM, 2), pattern(IDX_DIM, 1)


def _sample_mix_tables(w_s, b_s):
    t = jnp.arange(SUBLANES) % DEC_SEQ
    coefs = []
    for d in range(DEC_SEQ):
        c = jnp.where((t - d >= 0)[None, :], w_s[:, t, jnp.maximum(t - d, 0)], 0.0)
        coefs.append(jnp.repeat(c.T, GROUP_DIM_A, axis=1))
    bias = jnp.repeat(b_s[:, t].T, GROUP_DIM_A, axis=1)
    return jnp.stack(coefs), bias


def kernel(x_prompt, x_sample, cache_k, cache_v, cache_kidx, page_table, norm_mix, norm_ffn,
           a_w_in, a_v_gain, a_w_s, a_b_s, a_w_out, b_w_in, b_w_out, ffn_w1, ffn_w2, norm_final):
    x = (x_prompt.reshape(P_ROWS, D_MODEL), x_sample.reshape(S_ROWS, D_MODEL))
    kv_new = None
    rope_h, rope_i, rope_k = _rope_tables()
    n_pool = cache_k.shape[1]
    ck = cache_k.reshape(-1, PAGE_SIZE * N_KV_HEADS, HEAD_DIM)
    cv = cache_v.reshape(-1, PAGE_SIZE * N_KV_HEADS, HEAD_DIM)
    cki = jnp.swapaxes(cache_kidx, 2, 3).reshape(-1, IDX_DIM, PAGE_SIZE)
    g_final = norm_final.reshape(1, D_MODEL)
    a_w_in_b, a_w_out_b, b_w_out_b = a_w_in.astype(bf16), a_w_out.astype(bf16), b_w_out.astype(bf16)
    b_w_in_b = jnp.pad(b_w_in, ((0, 0), (0, 0), (0, B_PROJ_PAD - B_PROJ))).astype(bf16)
    ffn_w1_b, ffn_w2_b = ffn_w1.astype(bf16), ffn_w2.astype(bf16)
    outs = {name: [] for name in ("kip", "kis", "cvp", "cvs")}

    for i in range(DEPTH):
        j = i // 2
        g_mix = norm_mix[i].reshape(1, D_MODEL)
        if i % 2 == 0:
            scoef, sbias = _sample_mix_tables(a_w_s[j], a_b_s[j])
            bias = jnp.repeat(a_b_s[j].T, GROUP_DIM_A, axis=1)
            *m, cvp, cvs = _mixa(x, g_mix, a_w_in_b, j, a_v_gain[j].reshape(1, D_A), a_w_s[j], bias, scoef, sbias)
            outs["cvp"].append(cvp)
            outs["cvs"].append(cvs.reshape(DEC_BATCH, DEC_SEQ, D_A))
            w_out = a_w_out_b
        else:
            q, kb, vb, qi, kiwi, kib, wi_t, *kv_new = _proj(x[0], g_mix, b_w_in_b, rope_h, rope_i, rope_k, j, kv_new)
            m_p = _dsa_prompt(q, qi, wi_t, kb, vb, kib)
            qi32 = qi[P_ROWS:].reshape(DEC_BATCH, DEC_SEQ * N_IDX_HEADS, IDX_DIM)
            w32 = kiwi[P_ROWS:, IDX_DIM:IDX_DIM + N_IDX_HEADS].reshape(DEC_BATCH, DEC_SEQ * N_IDX_HEADS, 1)
            pad_rows = lambda a: jnp.pad(a.reshape(DEC_BATCH, DEC_SEQ, -1), ((0, 0), (0, LANES - DEC_SEQ), (0, 0)))
            sc = _sample_index_scores(page_table, qi32, w32, pad_rows(kib[P_ROWS:]), cki, j)
            bias = _sample_topk_bias(sc.reshape(S_ROWS, S_KEYS).T).T
            bias16 = jnp.tile(bias.reshape(DEC_BATCH, DEC_SEQ, S_KEYS), (1, Q_PER_KV, 1))
            q16 = q[P_ROWS:].reshape(DEC_BATCH, DEC_SEQ, N_KV_HEADS, Q_PER_KV, HEAD_DIM)
            q16 = q16.transpose(0, 2, 3, 1, 4).reshape(DEC_BATCH, N_KV_HEADS, Q_PER_KV * DEC_SEQ, HEAD_DIM)
            v_new = vb[P_ROWS:].reshape(S_ROWS, N_KV_HEADS, 2, HEAD_DIM)[:, :, 0]
            o16 = _sample_attention(page_table, q16, bias16, pad_rows(kb[P_ROWS:]), pad_rows(v_new), ck, cv, j)
            m_s = o16.reshape(DEC_BATCH, N_KV_HEADS, Q_PER_KV, DEC_SEQ, HEAD_DIM).transpose(0, 3, 1, 2, 4)
            m = (m_p, m_s.reshape(S_ROWS, Q_W))
            outs["kip"].append(kiwi[:P_ROWS, :IDX_DIM].reshape(BATCH, SEQ, IDX_DIM))
            outs["kis"].append(kiwi[P_ROWS:, :IDX_DIM].reshape(DEC_BATCH, DEC_SEQ, IDX_DIM))
            w_out = b_w_out_b
        x = tuple(_post(x, tuple(m), w_out, j, norm_ffn[i].reshape(1, D_MODEL), ffn_w1_b, ffn_w2_b, i, g_final,
                        final=(i == DEPTH - 1)))

    y_prompt = x[0].reshape(BATCH, SEQ, D_MODEL)
    y_sample = x[1].reshape(DEC_BATCH, DEC_SEQ, D_MODEL)
    n_layers = DEPTH // 2
    kp, ks, vp, vs = kv_new
    prompt_kv = lambda a: a.reshape(n_layers, BATCH, SEQ, N_KV_HEADS, HEAD_DIM)
    sample_kv = lambda a: a.reshape(n_layers, DEC_BATCH, DEC_SEQ, N_KV_HEADS, HEAD_DIM)
    st = jnp.stack
    return (y_prompt, y_sample, prompt_kv(kp), prompt_kv(vp), st(outs["kip"]), sample_kv(ks), sample_kv(vs),
            st(outs["kis"]), st(outs["cvp"]), st(outs["cvs"]))
```

```python
import functools

import jax
import jax.numpy as jnp
from jax import lax
from jax.experimental import pallas as pl
from jax.experimental.pallas import tpu as pltpu

D_MODEL = 1024
BATCH = 8
SEQ = 2048
DEPTH = 4
DEC_BATCH = 128
DEC_SEQ = 4
PAST_LEN = 2048
PAGE_SIZE = 128
N_PAGES = PAST_LEN // PAGE_SIZE
CHUNK = 128
D_A = D_MODEL
N_GROUPS_A = 8
GROUP_DIM_A = D_A // N_GROUPS_A
N_HEADS = 8
HEAD_DIM = D_MODEL // N_HEADS
N_KV_HEADS = 2
Q_PER_KV = N_HEADS // N_KV_HEADS
N_IDX_HEADS = 8
IDX_DIM = 64
TOPK_MAX = 256
QUERY_BLOCK = 128
ROPE_THETA = 500000.0
ROT_DIV = 4
IDX_SCALE = float((IDX_DIM * N_IDX_HEADS) ** -0.5)
ATTN_SCALE = float(HEAD_DIM ** -0.5)
Q_W = N_HEADS * HEAD_DIM
KV_W = N_KV_HEADS * HEAD_DIM
QI_W = N_IDX_HEADS * IDX_DIM
B_PROJ = Q_W + 2 * KV_W + QI_W + IDX_DIM + N_IDX_HEADS
D_FF = 4 * D_MODEL
EPS = 1e-6

LANES = 128
SUBLANES = 8
P_ROWS = BATCH * SEQ
S_ROWS = DEC_BATCH * DEC_SEQ
ROWS = P_ROWS + S_ROWS
TM = 512
N_TILES = ROWS // TM
P_TILES = P_ROWS // TM
TILES_PER_SEQ = SEQ // TM
CHUNKS_PER_TILE = TM // CHUNK
FF_TILE = 1024
B_PROJ_PAD = Q_W + 2 * KV_W + QI_W + LANES
KIWI_OFF = Q_W + 2 * KV_W + QI_W
S_KEYS = PAST_LEN + LANES
ATT_CHUNK = 256
KV_SPAN = 512
VMEM_LIMIT = 56 * 1024 * 1024
INT_MIN = -(2 ** 31)
F32_LOWEST = -3.4028234663852886e38
LOG2_E = 1.4426950408889634
SATT_SEQS = 2
SIDX_SEQS = 4
BISECT_ROWS = 64

f32 = jnp.float32
bf16 = jnp.bfloat16

assert S_ROWS == TM and P_ROWS % TM == 0 and SEQ % TM == 0


def _params(*sem):
    return pltpu.CompilerParams(dimension_semantics=sem, vmem_limit_bytes=VMEM_LIMIT)


def _rms(x, g):
    return x * lax.rsqrt(jnp.mean(x * x, axis=-1, keepdims=True) + EPS) * g


def _gelu(x):
    a = -2.0 * LOG2_E * 0.7978845608028654
    return x / (1.0 + jnp.exp2(x * (a + (a * 0.044715) * (x * x))))


def _dot(a, b):
    return jnp.dot(a, b, preferred_element_type=f32)


def _dot_nt(a, b):
    return lax.dot_general(a, b, (((1,), (1,)), ((), ())), preferred_element_type=f32)


def _prompt_tile(i):
    return (jnp.minimum(i, P_TILES - 1), 0)


def _first_tile(i):
    return (0, 0)


def _pair_specs(width):
    return [pl.BlockSpec((TM, width), _prompt_tile), pl.BlockSpec((S_ROWS, width), _first_tile)]


def _layer_spec(shape, layer, **kwargs):
    return pl.BlockSpec((None,) + shape, lambda i: (layer, 0, 0), **kwargs)


def _rows_specs(n_arrays, width):
    return [pl.BlockSpec((TM, width), lambda i: (i, 0))] if n_arrays == 1 else _pair_specs(width)


def _rows_tile(i, refs):
    if len(refs) == 1:
        return refs[0][...]
    p_ref, s_ref = refs
    return jnp.where(i == P_TILES, s_ref[...], p_ref[...])


def _store_rows_tile(i, refs, value):
    if len(refs) == 1:
        refs[0][...] = value
        return
    p_ref, s_ref = refs

    @pl.when(i < P_TILES)
    def _():
        p_ref[...] = value

    @pl.when(i == P_TILES)
    def _():
        s_ref[...] = value


def _mixa_kernel(*refs, n_x):
    x_refs = refs[:n_x]
    g_ref, win_ref, gain_ref, ws_ref, bias_ref, scoef_ref, sbias_ref, m_ref, cvp_ref, cvs_ref = refs[n_x:]
    i = pl.program_id(0)
    h = _rms(_rows_tile(i, x_refs), g_ref[...]).astype(bf16)
    uv = _gelu(_dot(h, win_ref[...]))
    u = uv[:, :D_A]
    v = uv[:, D_A:]
    vc = v - jnp.mean(v, axis=-1, keepdims=True)
    vn = vc * lax.rsqrt(jnp.mean(vc * vc, axis=-1, keepdims=True) + EPS) * gain_ref[...]

    @pl.when(i < P_TILES)
    def _prompt():
        t_idx = lax.broadcasted_iota(jnp.int32, (CHUNK, CHUNK), 0)
        s_idx = lax.broadcasted_iota(jnp.int32, (CHUNK, CHUNK), 1)
        causal = s_idx <= t_idx
        ws = [jnp.where(causal, ws_ref[g], 0.0).astype(bf16) for g in range(N_GROUPS_A)]
        for c in range(CHUNKS_PER_TILE):
            rows = slice(c * CHUNK, (c + 1) * CHUNK)
            v_c = vn[rows].astype(bf16)
            mixed = jnp.concatenate(
                [_dot(ws[g], v_c[:, g * GROUP_DIM_A:(g + 1) * GROUP_DIM_A]) for g in range(N_GROUPS_A)],
                axis=1) + bias_ref[...]
            m_ref[rows, :] = (u[rows] * mixed).astype(bf16)

        @pl.when(i % TILES_PER_SEQ == TILES_PER_SEQ - 1)
        def _():
            cvp_ref[0] = vn[TM - CHUNK:]

    @pl.when(i == P_TILES)
    def _sample():
        grp = TM // SUBLANES
        v3 = vn.reshape(grp, SUBLANES, D_A)
        mixed = sbias_ref[...][None] + scoef_ref[0][None] * v3
        for d in range(1, DEC_SEQ):
            mixed = mixed + scoef_ref[d][None] * pltpu.roll(v3, d, 1)
        m_ref[...] = (u.reshape(grp, SUBLANES, D_A) * mixed).reshape(TM, D_A).astype(bf16)
        cvs_ref[...] = vn


def _mixa(x, g, w_in, layer, gain, ws, bias, scoef, sbias):
    const2 = lambda i: (0, 0)
    const3 = lambda i: (0, 0, 0)
    return pl.pallas_call(
        functools.partial(_mixa_kernel, n_x=len(x)),
        grid=(N_TILES,),
        in_specs=_rows_specs(len(x), D_MODEL) + [
            pl.BlockSpec((1, D_MODEL), const2),
            _layer_spec((D_MODEL, 2 * D_A), layer),
            pl.BlockSpec((1, D_A), const2),
            pl.BlockSpec((N_GROUPS_A, CHUNK, CHUNK), const3),
            pl.BlockSpec((CHUNK, D_A), const2),
            pl.BlockSpec((DEC_SEQ, SUBLANES, D_A), const3),
            pl.BlockSpec((SUBLANES, D_A), const2),
        ],
        out_specs=[
            pl.BlockSpec((TM, D_A), lambda i: (i, 0)),
            pl.BlockSpec((1, CHUNK, D_A), lambda i: (jnp.minimum(i // TILES_PER_SEQ, BATCH - 1), 0, 0)),
            pl.BlockSpec((S_ROWS, D_A), const2),
        ],
        out_shape=[
            jax.ShapeDtypeStruct((ROWS, D_A), bf16),
            jax.ShapeDtypeStruct((BATCH, CHUNK, D_A), f32),
            jax.ShapeDtypeStruct((S_ROWS, D_A), f32),
        ],
        compiler_params=_params("arbitrary"),
        name="mixer_a",
    )(*x, g, w_in, gain, ws, bias, scoef, sbias)


def _post_kernel(*refs, n_x, n_m, final):
    x_refs, m_refs = refs[:n_x], refs[n_x:n_x + n_m]
    wout_ref, g_ref, w1_ref, w2_ref, gf_ref = refs[n_x + n_m:n_x + n_m + 5]
    y_refs = refs[n_x + n_m + 5:]
    i = pl.program_id(0)
    x1 = _rows_tile(i, x_refs) + _dot(_rows_tile(i, m_refs), wout_ref[...])
    h = _rms(x1, g_ref[...]).astype(bf16)
    y = x1
    for c in range(D_FF // FF_TILE):
        a = jnp.maximum(_dot(h, w1_ref[:, c * FF_TILE:(c + 1) * FF_TILE]), 0.0)
        y = y + _dot((a * a).astype(bf16), w2_ref[c * FF_TILE:(c + 1) * FF_TILE, :])
    if final:
        y = _rms(y, gf_ref[...])
    _store_rows_tile(i, y_refs, y)


def _post(x, m, w_out, mixer_layer, g, w1, w2, layer, g_final, final):
    const2 = lambda i: (0, 0)
    resident = dict(pipeline_mode=pl.Buffered(1))
    if final:
        out_specs = _pair_specs(D_MODEL)
        out_shape = [jax.ShapeDtypeStruct((P_ROWS, D_MODEL), f32), jax.ShapeDtypeStruct((S_ROWS, D_MODEL), f32)]
    else:
        out_specs = _rows_specs(1, D_MODEL)
        out_shape = [jax.ShapeDtypeStruct((ROWS, D_MODEL), f32)]
    return pl.pallas_call(
        functools.partial(_post_kernel, n_x=len(x), n_m=len(m), final=final),
        grid=(N_TILES,),
        in_specs=_rows_specs(len(x), D_MODEL) + _rows_specs(len(m), D_MODEL) + [
            _layer_spec((D_MODEL, D_MODEL), mixer_layer, **resident),
            pl.BlockSpec((1, D_MODEL), const2),
            _layer_spec((D_MODEL, D_FF), layer, **resident),
            _layer_spec((D_FF, D_MODEL), layer, **resident),
            pl.BlockSpec((1, D_MODEL), const2),
        ],
        out_specs=out_specs,
        out_shape=out_shape,
        compiler_params=_params("arbitrary"),
        name="post_ffn",
    )(*x, *m, w_out, g, w1, w2, g_final)


def _rope(x, cos, sin_lo, sin_hi, half):
    return x * cos + pltpu.roll(x, LANES - half, 1) * sin_lo + pltpu.roll(x, half, 1) * sin_hi


def _proj_kernel(*refs, n_alias):
    x_ref, g_ref, w_ref, rh_ref, ri_ref, rk_ref = refs[:6]
    (q_ref, kb_ref, vb_ref, qi_ref, kiwi_ref, kib_ref, wit_ref,
     kp_ref, ks_ref, vp_ref, vs_ref) = refs[6 + n_alias:]
    i = pl.program_id(0)
    h = _rms(x_ref[...], g_ref[...]).astype(bf16)
    p = _dot(h, w_ref[...])
    half_h = HEAD_DIM // ROT_DIV // 2
    half_i = IDX_DIM // ROT_DIV // 2
    ch, slh, shh = rh_ref[0], rh_ref[1], rh_ref[2]
    for hd in range(N_HEADS):
        cols = slice(hd * HEAD_DIM, (hd + 1) * HEAD_DIM)
        q_ref[:, cols] = (_rope(p[:, cols], ch, slh, shh, half_h) * (ATTN_SCALE * LOG2_E)).astype(bf16)
    one_hot = jnp.where(lax.broadcasted_iota(jnp.int32, (TM, HEAD_DIM), 1) == 0, 1.0, 0.0).astype(bf16)
    for kh in range(N_KV_HEADS):
        cols = slice(kh * HEAD_DIM, (kh + 1) * HEAD_DIM)
        kr = _rope(p[:, Q_W + kh * HEAD_DIM:Q_W + (kh + 1) * HEAD_DIM], ch, slh, shh, half_h)
        v = p[:, Q_W + KV_W + kh * HEAD_DIM:Q_W + KV_W + (kh + 1) * HEAD_DIM]
        kb_ref[:, cols] = kr.astype(bf16)
        vb_ref[:, 2 * kh * HEAD_DIM:(2 * kh + 1) * HEAD_DIM] = v.astype(bf16)
        vb_ref[:, (2 * kh + 1) * HEAD_DIM:(2 * kh + 2) * HEAD_DIM] = one_hot
        head_rows = pl.ds(kh, TM, stride=N_KV_HEADS)

        for slot in range(kp_ref.shape[0]):
            @pl.when(i < P_TILES)
            def _(slot=slot):
                kp_ref[slot, head_rows, :] = kr
                vp_ref[slot, head_rows, :] = v

            @pl.when(i == P_TILES)
            def _(slot=slot):
                ks_ref[slot, head_rows, :] = kr
                vs_ref[slot, head_rows, :] = v
    ci, sli, shi = ri_ref[0], ri_ref[1], ri_ref[2]
    off = Q_W + 2 * KV_W
    for j in range(QI_W // LANES):
        cols = slice(j * LANES, (j + 1) * LANES)
        qi_ref[:, cols] = _rope(p[:, off + j * LANES:off + (j + 1) * LANES], ci, sli, shi, half_i).astype(bf16)
    kiwi = _rope(p[:, KIWI_OFF:KIWI_OFF + LANES], rk_ref[0], rk_ref[1], rk_ref[2], half_i)
    kiwi_ref[...] = kiwi
    kib_ref[...] = kiwi[:, :IDX_DIM].astype(bf16)
    wit_ref[...] = kiwi.T[IDX_DIM:IDX_DIM + N_IDX_HEADS, :]


def _proj(x, g, w, rope_h, rope_i, rope_k, layer, kv_prev):
    const2 = lambda i: (0, 0)
    rope_map = lambda i: (0, jnp.where(i < P_TILES, i % TILES_PER_SEQ, TILES_PER_SEQ), 0)
    row = lambda i: (i, 0)
    widths = [(Q_W, bf16), (KV_W, bf16), (2 * KV_W, bf16), (QI_W, bf16), (LANES, f32), (IDX_DIM, bf16)]
    n_layers = DEPTH // 2
    kv_rows = TM * N_KV_HEADS
    slots, first = (n_layers, 0) if kv_prev is None else (1, layer)
    kv_specs = [pl.BlockSpec((slots, kv_rows, HEAD_DIM), lambda i: (first, jnp.minimum(i, P_TILES - 1), 0)),
                pl.BlockSpec((slots, kv_rows, HEAD_DIM), lambda i: (first, 0, 0))] * 2
    kv_shapes = [jax.ShapeDtypeStruct((n_layers, P_ROWS * N_KV_HEADS, HEAD_DIM), f32),
                 jax.ShapeDtypeStruct((n_layers, S_ROWS * N_KV_HEADS, HEAD_DIM), f32)] * 2
    n_in = 6
    n_out = len(widths) + 1
    alias_in = [] if kv_prev is None else list(kv_prev)
    return pl.pallas_call(
        functools.partial(_proj_kernel, n_alias=len(alias_in)),
        grid=(N_TILES,),
        in_specs=_rows_specs(1, D_MODEL) + [
            pl.BlockSpec((1, D_MODEL), const2),
            _layer_spec((D_MODEL, B_PROJ_PAD), layer),
            pl.BlockSpec((3, TM, LANES), rope_map),
            pl.BlockSpec((3, TM, LANES), rope_map),
            pl.BlockSpec((3, TM, LANES), rope_map),
        ] + [pl.BlockSpec(memory_space=pl.ANY)] * len(alias_in),
        out_specs=[pl.BlockSpec((TM, w_), row) for w_, _ in widths]
        + [pl.BlockSpec((N_IDX_HEADS, TM), lambda i: (0, i))] + kv_specs,
        out_shape=[jax.ShapeDtypeStruct((ROWS, w_), dt) for w_, dt in widths]
        + [jax.ShapeDtypeStruct((N_IDX_HEADS, ROWS), f32)] + kv_shapes,
        input_output_aliases={n_in + a: n_out + a for a in range(len(alias_in))},
        compiler_params=_params("arbitrary"),
        name="dsa_proj",
    )(x, g, w, rope_h, rope_i, rope_k, *alias_in)


def _order_float(key):
    return pltpu.bitcast(key ^ ((key >> 31) & 0x7FFFFFFF), f32)


def _count0(mask):
    return jnp.sum(jnp.where(mask, 1.0, 0.0), axis=0, keepdims=True)


def _kth_largest(sc_ref, n_keys, k):
    kf = float(k)

    def count_ge(cand_key):
        cand = _order_float(cand_key)
        acc = jnp.zeros((BISECT_ROWS, LANES), f32)
        for c in range(n_keys // BISECT_ROWS):
            acc = jnp.where(sc_ref[c * BISECT_ROWS:(c + 1) * BISECT_ROWS, :] >= cand, acc + 1.0, acc)
        return jnp.sum(acc, axis=0, keepdims=True)

    zero = jnp.zeros((1, LANES), jnp.int32)
    thr = jnp.where(count_ge(zero) >= kf, zero, zero + INT_MIN)

    def bit_step(j, thr):
        cand = thr | lax.shift_left(jnp.int32(1), 30 - j)
        return jnp.where(count_ge(cand) >= kf, cand, thr)

    thr = _order_float(lax.fori_loop(0, 31, bit_step, thr))
    return jnp.where(thr >= F32_LOWEST, thr, F32_LOWEST)


def _topk_bias_t(sc, thr, k, out_ref):
    n_keys = sc.shape[0]
    kf = float(k)
    ge = sc >= thr
    excess = jnp.max(_count0(ge)) - kf

    @pl.when(excess <= 0.0)
    def _():
        out_ref[:n_keys, :] = jnp.where(ge, 0.0, -jnp.inf)

    @pl.when(excess > 0.0)
    def _():
        pos = lax.broadcasted_iota(jnp.int32, sc.shape, 0)
        above = sc > thr
        tie = sc == thr
        need = kf - _count0(above)
        n_bits = max(1, (n_keys - 1).bit_length())

        def step(j, p):
            cand = p + lax.shift_left(jnp.int32(1), n_bits - 1 - j)
            return jnp.where(_count0(tie & (pos < cand)) < need, cand, p)

        last = lax.fori_loop(0, n_bits, step, jnp.zeros(thr.shape, jnp.int32))
        out_ref[:n_keys, :] = jnp.where(above | (tie & (pos <= last)), 0.0, -jnp.inf)


def _dsa_prompt_kernel(q_ref, qi_ref, wt_ref, k_ref, v_ref, ki_ref, o_ref, key_s, thr_s, biast_s, bias_s):
    i = pl.program_id(1)
    k_top = min(TOPK_MAX, SEQ // 4)
    blocks_per_span = KV_SPAN // QUERY_BLOCK

    def run(span):
        n_keys = (span + 1) * KV_SPAN
        ki = ki_ref[:n_keys, :]
        wt = wt_ref[...] * IDX_SCALE
        sc = None
        for hd in range(N_IDX_HEADS):
            d = _dot_nt(ki, qi_ref[:, hd * IDX_DIM:(hd + 1) * IDX_DIM])
            term = jnp.maximum(d, 0.0) * wt[hd:hd + 1, :]
            sc = term if sc is None else sc + term
        pos = lax.broadcasted_iota(jnp.int32, (n_keys, QUERY_BLOCK), 0)
        query = i * QUERY_BLOCK + lax.broadcasted_iota(jnp.int32, (n_keys, QUERY_BLOCK), 1)
        valid = pos <= query
        key_s[:n_keys, :] = jnp.where(valid, sc, -jnp.inf)

        thr_s[...] = jnp.full(thr_s.shape, F32_LOWEST, f32)
        for blk in range(span * blocks_per_span, (span + 1) * blocks_per_span):
            if (blk + 1) * QUERY_BLOCK > k_top:
                @pl.when(i == blk)
                def _(blk=blk):
                    thr_s[0:1, :] = _kth_largest(key_s, (blk + 1) * QUERY_BLOCK, k_top)

        _topk_bias_t(key_s[:n_keys, :], thr_s[0:1, :], k_top, biast_s)
        bias_s[:, :n_keys] = biast_s[:n_keys, :].T
        bias = bias_s[:, :n_keys][None]
        for kh in range(N_KV_HEADS):
            q_rows = jnp.concatenate(
                [q_ref[:, (kh * Q_PER_KV + g) * HEAD_DIM:(kh * Q_PER_KV + g + 1) * HEAD_DIM]
                 for g in range(Q_PER_KV)], axis=0)
            chunks = [slice(c, c + ATT_CHUNK) for c in range(0, n_keys, ATT_CHUNK)]
            s = []
            for ck in chunks:
                s_c = _dot_nt(q_rows, k_ref[ck, kh * HEAD_DIM:(kh + 1) * HEAD_DIM])
                s.append((s_c.reshape(Q_PER_KV, QUERY_BLOCK, ATT_CHUNK) + bias[:, :, ck])
                         .reshape(Q_PER_KV * QUERY_BLOCK, ATT_CHUNK))
            row_max = functools.reduce(jnp.maximum, [jnp.max(s_c, axis=1, keepdims=True) for s_c in s])
            o = None
            for ck, s_c in zip(chunks, s):
                o_c = _dot(jnp.exp2(s_c - row_max).astype(bf16), v_ref[ck, 2 * kh * HEAD_DIM:(2 * kh + 2) * HEAD_DIM])
                o = o_c if o is None else o + o_c
            o = o[:, :HEAD_DIM] / o[:, HEAD_DIM:HEAD_DIM + 1]
            for g in range(Q_PER_KV):
                hd = kh * Q_PER_KV + g
                o_ref[:, hd * HEAD_DIM:(hd + 1) * HEAD_DIM] = o[g * QUERY_BLOCK:(g + 1) * QUERY_BLOCK].astype(bf16)

    for span in range(SEQ // KV_SPAN):
        pl.when(i // blocks_per_span == span)(functools.partial(run, span))


def _dsa_prompt(q, qi, wi_t, kb, vb, kib):
    nq = SEQ // QUERY_BLOCK
    qmap = lambda b, i: (b * nq + i, 0)
    kmap = lambda b, i: (b, 0)
    return pl.pallas_call(
        _dsa_prompt_kernel,
        grid=(BATCH, nq),
        in_specs=[
            pl.BlockSpec((QUERY_BLOCK, Q_W), qmap),
            pl.BlockSpec((QUERY_BLOCK, QI_W), qmap),
            pl.BlockSpec((N_IDX_HEADS, QUERY_BLOCK), lambda b, i: (0, b * nq + i)),
            pl.BlockSpec((SEQ, KV_W), kmap),
            pl.BlockSpec((SEQ, 2 * KV_W), kmap),
            pl.BlockSpec((SEQ, IDX_DIM), kmap),
        ],
        out_specs=pl.BlockSpec((QUERY_BLOCK, Q_W), qmap),
        out_shape=jax.ShapeDtypeStruct((P_ROWS, Q_W), bf16),
        scratch_shapes=[
            pltpu.VMEM((SEQ, QUERY_BLOCK), f32),
            pltpu.VMEM((SUBLANES, QUERY_BLOCK), f32),
            pltpu.VMEM((SEQ, QUERY_BLOCK), f32),
            pltpu.VMEM((QUERY_BLOCK, SEQ), f32),
        ],
        compiler_params=_params("arbitrary", "arbitrary"),
        name="dsa_prompt",
    )(q, qi, wi_t, kb, vb, kib)


def _sidx_kernel(pt_ref, qi_ref, w_ref, kin_ref, cki_hbm, sc_ref, pbuf, sem, *, first_page):
    step = pl.program_id(0)
    slot = step % 2

    def page_copies(st, sl):
        return [pltpu.make_async_copy(cki_hbm.at[first_page + pt_ref[st * SIDX_SEQS + g, j]], pbuf.at[sl, g, j],
                                      sem.at[sl])
                for g in range(SIDX_SEQS) for j in range(N_PAGES)]

    @pl.when(step == 0)
    def _():
        for c in page_copies(0, 0):
            c.start()

    @pl.when(step + 1 < pl.num_programs(0))
    def _():
        for c in page_copies(step + 1, 1 - slot):
            c.start()

    for c in page_copies(step, slot):
        c.wait()

    for g in range(SIDX_SEQS):
        qi = qi_ref[g]
        d = jnp.concatenate([_dot(qi, pbuf[slot, g, j].astype(bf16)) for j in range(N_PAGES)]
                            + [_dot_nt(qi, kin_ref[g])], axis=1)
        d = jnp.maximum(d, 0.0) * w_ref[g]
        sc_ref[g] = jnp.sum(d.reshape(DEC_SEQ, N_IDX_HEADS, S_KEYS), axis=1) * IDX_SCALE


def _sample_index_scores(page_table, qi32, w32, ki_new, cache_kidx, layer):
    n_pool = cache_kidx.shape[0] // (DEPTH // 2)
    seq3 = lambda n, pt: (n, 0, 0)
    grid_spec = pltpu.PrefetchScalarGridSpec(
        num_scalar_prefetch=1,
        grid=(DEC_BATCH // SIDX_SEQS,),
        in_specs=[
            pl.BlockSpec((SIDX_SEQS, DEC_SEQ * N_IDX_HEADS, IDX_DIM), seq3),
            pl.BlockSpec((SIDX_SEQS, DEC_SEQ * N_IDX_HEADS, 1), seq3),
            pl.BlockSpec((SIDX_SEQS, LANES, IDX_DIM), seq3),
            pl.BlockSpec(memory_space=pl.ANY),
        ],
        out_specs=pl.BlockSpec((SIDX_SEQS, DEC_SEQ, S_KEYS), seq3),
        scratch_shapes=[
            pltpu.VMEM((2, SIDX_SEQS, N_PAGES, IDX_DIM, PAGE_SIZE), f32),
            pltpu.SemaphoreType.DMA((2,)),
        ],
    )
    return pl.pallas_call(
        functools.partial(_sidx_kernel, first_page=layer * n_pool),
        grid_spec=grid_spec,
        out_shape=jax.ShapeDtypeStruct((DEC_BATCH, DEC_SEQ, S_KEYS), f32),
        compiler_params=_params("arbitrary"),
        name="dsa_sample_index",
    )(page_table, qi32, w32, ki_new, cache_kidx)


def _stopk_kernel(sc_ref, bias_ref, key_s):
    shape = sc_ref.shape
    pos = lax.broadcasted_iota(jnp.int32, shape, 0)
    t = (pl.program_id(0) * LANES + lax.broadcasted_iota(jnp.int32, shape, 1)) % DEC_SEQ
    valid = pos <= t + PAST_LEN
    k_top = min(TOPK_MAX, (PAST_LEN + DEC_SEQ) // 4)
    key_s[...] = jnp.where(valid, sc_ref[...], -jnp.inf)
    thr = _kth_largest(key_s, S_KEYS, k_top)
    _topk_bias_t(key_s[...], thr, k_top, bias_ref)


def _sample_topk_bias(sc_t):
    spec = pl.BlockSpec((S_KEYS, LANES), lambda i: (0, i))
    return pl.pallas_call(
        _stopk_kernel,
        grid=(S_ROWS // LANES,),
        in_specs=[spec],
        out_specs=spec,
        out_shape=jax.ShapeDtypeStruct((S_KEYS, S_ROWS), f32),
        scratch_shapes=[pltpu.VMEM((S_KEYS, LANES), f32)],
        compiler_params=_params("arbitrary"),
        name="dsa_sample_topk",
    )(sc_t)


def _satt_kernel(pt_ref, q_ref, bias_ref, kn_ref, vn_ref, ck_hbm, cv_hbm, o_ref,
                 kbuf, vbuf, sem, *, first_page):
    step = pl.program_id(0)
    slot = step % 2

    def page_copies(st, sl):
        copies = []
        for g in range(SATT_SEQS):
            for j in range(N_PAGES):
                page = first_page + pt_ref[st * SATT_SEQS + g, j]
                copies.append(pltpu.make_async_copy(ck_hbm.at[page], kbuf.at[sl, g, j], sem.at[sl, 0]))
                copies.append(pltpu.make_async_copy(cv_hbm.at[page], vbuf.at[sl, g, j], sem.at[sl, 1]))
        return copies

    @pl.when(step == 0)
    def _():
        for c in page_copies(0, 0):
            c.start()

    @pl.when(step + 1 < pl.num_programs(0))
    def _():
        for c in page_copies(step + 1, 1 - slot):
            c.start()

    for c in page_copies(step, slot):
        c.wait()

    for g in range(SATT_SEQS):
        for kh in range(N_KV_HEADS):
            cols = slice(kh * HEAD_DIM, (kh + 1) * HEAD_DIM)
            head_rows = pl.ds(kh, PAGE_SIZE, stride=N_KV_HEADS)
            k_all = jnp.concatenate([kbuf[slot, g, j, head_rows, :].astype(bf16) for j in range(N_PAGES)]
                                    + [kn_ref[g, :, cols]], axis=0)
            v_all = jnp.concatenate([vbuf[slot, g, j, head_rows, :].astype(bf16) for j in range(N_PAGES)]
                                    + [vn_ref[g, :, cols]], axis=0)
            s = _dot_nt(q_ref[g, kh], k_all) + bias_ref[g]
            p = jnp.exp2(s - jnp.max(s, axis=1, keepdims=True))
            denom = jnp.sum(p, axis=1, keepdims=True)
            o_ref[g, kh] = (_dot(p.astype(bf16), v_all) / denom).astype(bf16)


def _sample_attention(page_table, q16, bias16, k_new, v_new, cache_k, cache_v, layer):
    n_pool = cache_k.shape[0] // (DEPTH // 2)
    rows = DEC_SEQ * Q_PER_KV
    seq3 = lambda n, pt: (n, 0, 0)
    seq4 = lambda n, pt: (n, 0, 0, 0)
    page_shape = (PAGE_SIZE * N_KV_HEADS, HEAD_DIM)
    grid_spec = pltpu.PrefetchScalarGridSpec(
        num_scalar_prefetch=1,
        grid=(DEC_BATCH // SATT_SEQS,),
        in_specs=[
            pl.BlockSpec((SATT_SEQS, N_KV_HEADS, rows, HEAD_DIM), seq4),
            pl.BlockSpec((SATT_SEQS, rows, S_KEYS), seq3),
            pl.BlockSpec((SATT_SEQS, LANES, KV_W), seq3),
            pl.BlockSpec((SATT_SEQS, LANES, KV_W), seq3),
            pl.BlockSpec(memory_space=pl.ANY),
            pl.BlockSpec(memory_space=pl.ANY),
        ],
        out_specs=pl.BlockSpec((SATT_SEQS, N_KV_HEADS, rows, HEAD_DIM), seq4),
        scratch_shapes=[
            pltpu.VMEM((2, SATT_SEQS, N_PAGES) + page_shape, f32),
            pltpu.VMEM((2, SATT_SEQS, N_PAGES) + page_shape, f32),
            pltpu.SemaphoreType.DMA((2, 2)),
        ],
    )
    return pl.pallas_call(
        functools.partial(_satt_kernel, first_page=layer * n_pool),
        grid_spec=grid_spec,
        out_shape=jax.ShapeDtypeStruct((DEC_BATCH, N_KV_HEADS, rows, HEAD_DIM), bf16),
        compiler_params=_params("arbitrary"),
        name="dsa_sample_attn",
    )(page_table, q16, bias16, k_new, v_new, cache_k, cache_v)


def _rope_tables():
    pos = jnp.concatenate([jnp.arange(SEQ), PAST_LEN + jnp.arange(TM) % DEC_SEQ]).astype(f32)

    def pattern(dim, n_rot_groups):
        r = dim // ROT_DIV
        half = r // 2
        inv = ROPE_THETA ** (-jnp.arange(half, dtype=f32) * 2.0 / r)
        ang = pos[:, None] * inv[None, :]
        cos, sin = jnp.cos(ang), jnp.sin(ang)
        ones = jnp.ones((pos.shape[0], dim - r), f32)
        zeros = jnp.zeros((pos.shape[0], dim - r), f32)
        zh = jnp.zeros_like(sin)
        c = jnp.concatenate([cos, cos, ones], axis=1)
        lo = jnp.concatenate([-sin, zh, zeros], axis=1)
        hi = jnp.concatenate([zh, sin, zeros], axis=1)
        ident = (jnp.ones((pos.shape[0], dim), f32), jnp.zeros((pos.shape[0], dim), f32),
                 jnp.zeros((pos.shape[0], dim), f32))
        reps = LANES // dim
        parts = [(c, lo, hi) if j < n_rot_groups else ident for j in range(reps)]
        return jnp.stack([jnp.concatenate([p_[t] for p_ in parts], axis=1) for t in range(3)])

    return pattern(HEAD_DIM, 1), pattern(IDX_DIM, 2), pattern(IDX_DIM, 1)


def _sample_mix_tables(w_s, b_s):
    t = jnp.arange(SUBLANES) % DEC_SEQ
    coefs = []
    for d in range(DEC_SEQ):
        c = jnp.where((t - d >= 0)[None, :], w_s[:, t, jnp.maximum(t - d, 0)], 0.0)
        coefs.append(jnp.repeat(c.T, GROUP_DIM_A, axis=1))
    bias = jnp.repeat(b_s[:, t].T, GROUP_DIM_A, axis=1)
    return jnp.stack(coefs), bias


def kernel(x_prompt, x_sample, cache_k, cache_v, cache_kidx, page_table, norm_mix, norm_ffn,
           a_w_in, a_v_gain, a_w_s, a_b_s, a_w_out, b_w_in, b_w_out, ffn_w1, ffn_w2, norm_final):
    x = (x_prompt.reshape(P_ROWS, D_MODEL), x_sample.reshape(S_ROWS, D_MODEL))
    kv_new = None
    rope_h, rope_i, rope_k = _rope_tables()
    n_pool = cache_k.shape[1]
    ck = cache_k.reshape(-1, PAGE_SIZE * N_KV_HEADS, HEAD_DIM)
    cv = cache_v.reshape(-1, PAGE_SIZE * N_KV_HEADS, HEAD_DIM)
    cki = jnp.swapaxes(cache_kidx, 2, 3).reshape(-1, IDX_DIM, PAGE_SIZE)
    g_final = norm_final.reshape(1, D_MODEL)
    a_w_in_b, a_w_out_b, b_w_out_b = a_w_in.astype(bf16), a_w_out.astype(bf16), b_w_out.astype(bf16)
    b_w_in_b = jnp.pad(b_w_in, ((0, 0), (0, 0), (0, B_PROJ_PAD - B_PROJ))).astype(bf16)
    ffn_w1_b, ffn_w2_b = ffn_w1.astype(bf16), ffn_w2.astype(bf16)
    outs = {name: [] for name in ("kip", "kis", "cvp", "cvs")}

    for i in range(DEPTH):
        j = i // 2
        g_mix = norm_mix[i].reshape(1, D_MODEL)
        if i % 2 == 0:
            scoef, sbias = _sample_mix_tables(a_w_s[j], a_b_s[j])
            bias = jnp.repeat(a_b_s[j].T, GROUP_DIM_A, axis=1)
            *m, cvp, cvs = _mixa(x, g_mix, a_w_in_b, j, a_v_gain[j].reshape(1, D_A), a_w_s[j], bias, scoef, sbias)
            outs["cvp"].append(cvp)
            outs["cvs"].append(cvs.reshape(DEC_BATCH, DEC_SEQ, D_A))
            w_out = a_w_out_b
        else:
            q, kb, vb, qi, kiwi, kib, wi_t, *kv_new = _proj(x[0], g_mix, b_w_in_b, rope_h, rope_i, rope_k, j, kv_new)
            m_p = _dsa_prompt(q, qi, wi_t, kb, vb, kib)
            qi32 = qi[P_ROWS:].reshape(DEC_BATCH, DEC_SEQ * N_IDX_HEADS, IDX_DIM)
            w32 = kiwi[P_ROWS:, IDX_DIM:IDX_DIM + N_IDX_HEADS].reshape(DEC_BATCH, DEC_SEQ * N_IDX_HEADS, 1)
            pad_rows = lambda a: jnp.pad(a.reshape(DEC_BATCH, DEC_SEQ, -1), ((0, 0), (0, LANES - DEC_SEQ), (0, 0)))
            sc = _sample_index_scores(page_table, qi32, w32, pad_rows(kib[P_ROWS:]), cki, j)
            bias = _sample_topk_bias(sc.reshape(S_ROWS, S_KEYS).T).T
            bias16 = jnp.tile(bias.reshape(DEC_BATCH, DEC_SEQ, S_KEYS), (1, Q_PER_KV, 1))
            q16 = q[P_ROWS:].reshape(DEC_BATCH, DEC_SEQ, N_KV_HEADS, Q_PER_KV, HEAD_DIM)
            q16 = q16.transpose(0, 2, 3, 1, 4).reshape(DEC_BATCH, N_KV_HEADS, Q_PER_KV * DEC_SEQ, HEAD_DIM)
            v_new = vb[P_ROWS:].reshape(S_ROWS, N_KV_HEADS, 2, HEAD_DIM)[:, :, 0]
            o16 = _sample_attention(page_table, q16, bias16, pad_rows(kb[P_ROWS:]), pad_rows(v_new), ck, cv, j)
            m_s = o16.reshape(DEC_BATCH, N_KV_HEADS, Q_PER_KV, DEC_SEQ, HEAD_DIM).transpose(0, 3, 1, 2, 4)
            m = (m_p, m_s.reshape(S_ROWS, Q_W))
            outs["kip"].append(kiwi[:P_ROWS, :IDX_DIM].reshape(BATCH, SEQ, IDX_DIM))
            outs["kis"].append(kiwi[P_ROWS:, :IDX_DIM].reshape(DEC_BATCH, DEC_SEQ, IDX_DIM))
            w_out = b_w_out_b
        x = tuple(_post(x, tuple(m), w_out, j, norm_ffn[i].reshape(1, D_MODEL), ffn_w1_b, ffn_w2_b, i, g_final,
                        final=(i == DEPTH - 1)))

    y_prompt = x[0].reshape(BATCH, SEQ, D_MODEL)
    y_sample = x[1].reshape(DEC_BATCH, DEC_SEQ, D_MODEL)
    n_layers = DEPTH // 2
    kp, ks, vp, vs = kv_new
    prompt_kv = lambda a: a.reshape(n_layers, BATCH, SEQ, N_KV_HEADS, HEAD_DIM)
    sample_kv = lambda a: a.reshape(n_layers, DEC_BATCH, DEC_SEQ, N_KV_HEADS, HEAD_DIM)
    st = jnp.stack
    return (y_prompt, y_sample, prompt_kv(kp), prompt_kv(vp), st(outs["kip"]), sample_kv(ks), sample_kv(vs),
            st(outs["kis"]), st(outs["cvp"]), st(outs["cvs"]))
```

```python
import functools

import jax
import jax.numpy as jnp
from jax import lax
from jax.experimental import pallas as pl
from jax.experimental.pallas import tpu as pltpu

D_MODEL = 1024
BATCH = 8
SEQ = 2048
DEPTH = 4
DEC_BATCH = 128
DEC_SEQ = 4
PAST_LEN = 2048
PAGE_SIZE = 128
N_PAGES = PAST_LEN // PAGE_SIZE
CHUNK = 128
D_A = D_MODEL
N_GROUPS_A = 8
GROUP_DIM_A = D_A // N_GROUPS_A
N_HEADS = 8
HEAD_DIM = D_MODEL // N_HEADS
N_KV_HEADS = 2
Q_PER_KV = N_HEADS // N_KV_HEADS
N_IDX_HEADS = 8
IDX_DIM = 64
TOPK_MAX = 256
QUERY_BLOCK = 128
ROPE_THETA = 500000.0
ROT_DIV = 4
IDX_SCALE = float((IDX_DIM * N_IDX_HEADS) ** -0.5)
ATTN_SCALE = float(HEAD_DIM ** -0.5)
Q_W = N_HEADS * HEAD_DIM
KV_W = N_KV_HEADS * HEAD_DIM
QI_W = N_IDX_HEADS * IDX_DIM
B_PROJ = Q_W + 2 * KV_W + QI_W + IDX_DIM + N_IDX_HEADS
D_FF = 4 * D_MODEL
EPS = 1e-6

LANES = 128
SUBLANES = 8
P_ROWS = BATCH * SEQ
S_ROWS = DEC_BATCH * DEC_SEQ
ROWS = P_ROWS + S_ROWS
TM = 512
N_TILES = ROWS // TM
P_TILES = P_ROWS // TM
TILES_PER_SEQ = SEQ // TM
CHUNKS_PER_TILE = TM // CHUNK
FF_TILE = 1024
B_PROJ_PAD = Q_W + 2 * KV_W + QI_W + LANES
KIWI_OFF = Q_W + 2 * KV_W + QI_W
S_KEYS = PAST_LEN + LANES
ATT_CHUNK = 256
KV_SPAN = 512
VMEM_LIMIT = 56 * 1024 * 1024
INT_MIN = -(2 ** 31)
F32_LOWEST = -3.4028234663852886e38
LOG2_E = 1.4426950408889634
SATT_SEQS = 2
SIDX_SEQS = 4
BISECT_ROWS = 64

f32 = jnp.float32
bf16 = jnp.bfloat16

assert S_ROWS == TM and P_ROWS % TM == 0 and SEQ % TM == 0


def _params(*sem):
    return pltpu.CompilerParams(dimension_semantics=sem, vmem_limit_bytes=VMEM_LIMIT)


def _rms(x, g):
    return x * lax.rsqrt(jnp.mean(x * x, axis=-1, keepdims=True) + EPS) * g


def _gelu(x):
    a = -2.0 * LOG2_E * 0.7978845608028654
    return x / (1.0 + jnp.exp2(x * (a + (a * 0.044715) * (x * x))))


def _dot(a, b):
    return jnp.dot(a, b, preferred_element_type=f32)


def _dot_nt(a, b):
    return lax.dot_general(a, b, (((1,), (1,)), ((), ())), preferred_element_type=f32)


def _prompt_tile(i):
    return (jnp.minimum(i, P_TILES - 1), 0)


def _first_tile(i):
    return (0, 0)


def _pair_specs(width):
    return [pl.BlockSpec((TM, width), _prompt_tile), pl.BlockSpec((S_ROWS, width), _first_tile)]


def _layer_spec(shape, layer, **kwargs):
    return pl.BlockSpec((None,) + shape, lambda i: (layer, 0, 0), **kwargs)


def _rows_specs(n_arrays, width):
    return [pl.BlockSpec((TM, width), lambda i: (i, 0))] if n_arrays == 1 else _pair_specs(width)


def _rows_tile(i, refs):
    if len(refs) == 1:
        return refs[0][...]
    p_ref, s_ref = refs
    return jnp.where(i == P_TILES, s_ref[...], p_ref[...])


def _store_rows_tile(i, refs, value):
    if len(refs) == 1:
        refs[0][...] = value
        return
    p_ref, s_ref = refs

    @pl.when(i < P_TILES)
    def _():
        p_ref[...] = value

    @pl.when(i == P_TILES)
    def _():
        s_ref[...] = value


def _mixa_kernel(*refs, n_x):
    x_refs = refs[:n_x]
    g_ref, win_ref, gain_ref, ws_ref, bias_ref, scoef_ref, sbias_ref, m_ref, cvp_ref, cvs_ref = refs[n_x:]
    i = pl.program_id(0)
    h = _rms(_rows_tile(i, x_refs), g_ref[...]).astype(bf16)
    uv = _gelu(_dot(h, win_ref[...]))
    u = uv[:, :D_A]
    v = uv[:, D_A:]
    vc = v - jnp.mean(v, axis=-1, keepdims=True)
    vn = vc * lax.rsqrt(jnp.mean(vc * vc, axis=-1, keepdims=True) + EPS) * gain_ref[...]

    @pl.when(i < P_TILES)
    def _prompt():
        t_idx = lax.broadcasted_iota(jnp.int32, (CHUNK, CHUNK), 0)
        s_idx = lax.broadcasted_iota(jnp.int32, (CHUNK, CHUNK), 1)
        causal = s_idx <= t_idx
        ws = [jnp.where(causal, ws_ref[g], 0.0).astype(bf16) for g in range(N_GROUPS_A)]
        for c in range(CHUNKS_PER_TILE):
            rows = slice(c * CHUNK, (c + 1) * CHUNK)
            v_c = vn[rows].astype(bf16)
            mixed = jnp.concatenate(
                [_dot(ws[g], v_c[:, g * GROUP_DIM_A:(g + 1) * GROUP_DIM_A]) for g in range(N_GROUPS_A)],
                axis=1) + bias_ref[...]
            m_ref[rows, :] = (u[rows] * mixed).astype(bf16)

        @pl.when(i % TILES_PER_SEQ == TILES_PER_SEQ - 1)
        def _():
            cvp_ref[0] = vn[TM - CHUNK:]

    @pl.when(i == P_TILES)
    def _sample():
        grp = TM // SUBLANES
        v3 = vn.reshape(grp, SUBLANES, D_A)
        mixed = sbias_ref[...][None] + scoef_ref[0][None] * v3
        for d in range(1, DEC_SEQ):
            mixed = mixed + scoef_ref[d][None] * pltpu.roll(v3, d, 1)
        m_ref[...] = (u.reshape(grp, SUBLANES, D_A) * mixed).reshape(TM, D_A).astype(bf16)
        cvs_ref[...] = vn


def _mixa(x, g, w_in, layer, gain, ws, bias, scoef, sbias):
    const2 = lambda i: (0, 0)
    const3 = lambda i: (0, 0, 0)
    return pl.pallas_call(
        functools.partial(_mixa_kernel, n_x=len(x)),
        grid=(N_TILES,),
        in_specs=_rows_specs(len(x), D_MODEL) + [
            pl.BlockSpec((1, D_MODEL), const2),
            _layer_spec((D_MODEL, 2 * D_A), layer),
            pl.BlockSpec((1, D_A), const2),
            pl.BlockSpec((N_GROUPS_A, CHUNK, CHUNK), const3),
            pl.BlockSpec((CHUNK, D_A), const2),
            pl.BlockSpec((DEC_SEQ, SUBLANES, D_A), const3),
            pl.BlockSpec((SUBLANES, D_A), const2),
        ],
        out_specs=[
            pl.BlockSpec((TM, D_A), lambda i: (i, 0)),
            pl.BlockSpec((1, CHUNK, D_A), lambda i: (jnp.minimum(i // TILES_PER_SEQ, BATCH - 1), 0, 0)),
            pl.BlockSpec((S_ROWS, D_A), const2),
        ],
        out_shape=[
            jax.ShapeDtypeStruct((ROWS, D_A), bf16),
            jax.ShapeDtypeStruct((BATCH, CHUNK, D_A), f32),
            jax.ShapeDtypeStruct((S_ROWS, D_A), f32),
        ],
        compiler_params=_params("arbitrary"),
        name="mixer_a",
    )(*x, g, w_in, gain, ws, bias, scoef, sbias)


def _post_kernel(*refs, n_x, n_m, final):
    x_refs, m_refs = refs[:n_x], refs[n_x:n_x + n_m]
    wout_ref, g_ref, w1_ref, w2_ref, gf_ref = refs[n_x + n_m:n_x + n_m + 5]
    y_refs = refs[n_x + n_m + 5:]
    i = pl.program_id(0)
    x1 = _rows_tile(i, x_refs) + _dot(_rows_tile(i, m_refs), wout_ref[...])
    h = _rms(x1, g_ref[...]).astype(bf16)
    y = x1
    for c in range(D_FF // FF_TILE):
        a = jnp.maximum(_dot(h, w1_ref[:, c * FF_TILE:(c + 1) * FF_TILE]), 0.0)
        y = y + _dot((a * a).astype(bf16), w2_ref[c * FF_TILE:(c + 1) * FF_TILE, :])
    if final:
        y = _rms(y, gf_ref[...])
    _store_rows_tile(i, y_refs, y)


def _post(x, m, w_out, mixer_layer, g, w1, w2, layer, g_final, final):
    const2 = lambda i: (0, 0)
    resident = dict(pipeline_mode=pl.Buffered(1))
    if final:
        out_specs = _pair_specs(D_MODEL)
        out_shape = [jax.ShapeDtypeStruct((P_ROWS, D_MODEL), f32), jax.ShapeDtypeStruct((S_ROWS, D_MODEL), f32)]
    else:
        out_specs = _rows_specs(1, D_MODEL)
        out_shape = [jax.ShapeDtypeStruct((ROWS, D_MODEL), f32)]
    return pl.pallas_call(
        functools.partial(_post_kernel, n_x=len(x), n_m=len(m), final=final),
        grid=(N_TILES,),
        in_specs=_rows_specs(len(x), D_MODEL) + _rows_specs(len(m), D_MODEL) + [
            _layer_spec((D_MODEL, D_MODEL), mixer_layer, **resident),
            pl.BlockSpec((1, D_MODEL), const2),
            _layer_spec((D_MODEL, D_FF), layer, **resident),
            _layer_spec((D_FF, D_MODEL), layer, **resident),
            pl.BlockSpec((1, D_MODEL), const2),
        ],
        out_specs=out_specs,
        out_shape=out_shape,
        compiler_params=_params("arbitrary"),
        name="post_ffn",
    )(*x, *m, w_out, g, w1, w2, g_final)


def _rope(x, cos, sin_lo, sin_hi, half):
    return x * cos + pltpu.roll(x, LANES - half, 1) * sin_lo + pltpu.roll(x, half, 1) * sin_hi


def _proj_kernel(*refs, n_alias):
    x_ref, g_ref, w_ref, rh_ref, ri_ref, rk_ref = refs[:6]
    (q_ref, kb_ref, vb_ref, qi_ref, kiwi_ref, kib_ref, wit_ref,
     kp_ref, ks_ref, vp_ref, vs_ref) = refs[6 + n_alias:]
    i = pl.program_id(0)
    h = _rms(x_ref[...], g_ref[...]).astype(bf16)
    p = _dot(h, w_ref[...])
    half_h = HEAD_DIM // ROT_DIV // 2
    half_i = IDX_DIM // ROT_DIV // 2
    ch, slh, shh = rh_ref[0], rh_ref[1], rh_ref[2]
    for hd in range(N_HEADS):
        cols = slice(hd * HEAD_DIM, (hd + 1) * HEAD_DIM)
        q_ref[:, cols] = (_rope(p[:, cols], ch, slh, shh, half_h) * (ATTN_SCALE * LOG2_E)).astype(bf16)
    one_hot = jnp.where(lax.broadcasted_iota(jnp.int32, (TM, HEAD_DIM), 1) == 0, 1.0, 0.0).astype(bf16)
    for kh in range(N_KV_HEADS):
        cols = slice(kh * HEAD_DIM, (kh + 1) * HEAD_DIM)
        kr = _rope(p[:, Q_W + kh * HEAD_DIM:Q_W + (kh + 1) * HEAD_DIM], ch, slh, shh, half_h)
        v = p[:, Q_W + KV_W + kh * HEAD_DIM:Q_W + KV_W + (kh + 1) * HEAD_DIM]
        kb_ref[:, cols] = kr.astype(bf16)
        vb_ref[:, 2 * kh * HEAD_DIM:(2 * kh + 1) * HEAD_DIM] = v.astype(bf16)
        vb_ref[:, (2 * kh + 1) * HEAD_DIM:(2 * kh + 2) * HEAD_DIM] = one_hot
        head_rows = pl.ds(kh, TM, stride=N_KV_HEADS)

        for slot in range(kp_ref.shape[0]):
            @pl.when(i < P_TILES)
            def _(slot=slot):
                kp_ref[slot, head_rows, :] = kr
                vp_ref[slot, head_rows, :] = v

            @pl.when(i == P_TILES)
            def _(slot=slot):
                ks_ref[slot, head_rows, :] = kr
                vs_ref[slot, head_rows, :] = v
    ci, sli, shi = ri_ref[0], ri_ref[1], ri_ref[2]
    off = Q_W + 2 * KV_W
    for j in range(QI_W // LANES):
        cols = slice(j * LANES, (j + 1) * LANES)
        qi_ref[:, cols] = _rope(p[:, off + j * LANES:off + (j + 1) * LANES], ci, sli, shi, half_i).astype(bf16)
    kiwi = _rope(p[:, KIWI_OFF:KIWI_OFF + LANES], rk_ref[0], rk_ref[1], rk_ref[2], half_i)
    kiwi_ref[...] = kiwi
    kib_ref[...] = kiwi[:, :IDX_DIM].astype(bf16)
    wit_ref[...] = kiwi.T[IDX_DIM:IDX_DIM + N_IDX_HEADS, :]


def _proj(x, g, w, rope_h, rope_i, rope_k, layer, kv_prev):
    const2 = lambda i: (0, 0)
    rope_map = lambda i: (0, jnp.where(i < P_TILES, i % TILES_PER_SEQ, TILES_PER_SEQ), 0)
    row = lambda i: (i, 0)
    widths = [(Q_W, bf16), (KV_W, bf16), (2 * KV_W, bf16), (QI_W, bf16), (LANES, f32), (IDX_DIM, bf16)]
    n_layers = DEPTH // 2
    kv_rows = TM * N_KV_HEADS
    slots, first = (n_layers, 0) if kv_prev is None else (1, layer)
    kv_specs = [pl.BlockSpec((slots, kv_rows, HEAD_DIM), lambda i: (first, jnp.minimum(i, P_TILES - 1), 0)),
                pl.BlockSpec((slots, kv_rows, HEAD_DIM), lambda i: (first, 0, 0))] * 2
    kv_shapes = [jax.ShapeDtypeStruct((n_layers, P_ROWS * N_KV_HEADS, HEAD_DIM), f32),
                 jax.ShapeDtypeStruct((n_layers, S_ROWS * N_KV_HEADS, HEAD_DIM), f32)] * 2
    n_in = 6
    n_out = len(widths) + 1
    alias_in = [] if kv_prev is None else list(kv_prev)
    return pl.pallas_call(
        functools.partial(_proj_kernel, n_alias=len(alias_in)),
        grid=(N_TILES,),
        in_specs=_rows_specs(1, D_MODEL) + [
            pl.BlockSpec((1, D_MODEL), const2),
            _layer_spec((D_MODEL, B_PROJ_PAD), layer),
            pl.BlockSpec((3, TM, LANES), rope_map),
            pl.BlockSpec((3, TM, LANES), rope_map),
            pl.BlockSpec((3, TM, LANES), rope_map),
        ] + [pl.BlockSpec(memory_space=pl.ANY)] * len(alias_in),
        out_specs=[pl.BlockSpec((TM, w_), row) for w_, _ in widths]
        + [pl.BlockSpec((N_IDX_HEADS, TM), lambda i: (0, i))] + kv_specs,
        out_shape=[jax.ShapeDtypeStruct((ROWS, w_), dt) for w_, dt in widths]
        + [jax.ShapeDtypeStruct((N_IDX_HEADS, ROWS), f32)] + kv_shapes,
        input_output_aliases={n_in + a: n_out + a for a in range(len(alias_in))},
        compiler_params=_params("arbitrary"),
        name="dsa_proj",
    )(x, g, w, rope_h, rope_i, rope_k, *alias_in)


def _order_float(key):
    return pltpu.bitcast(key ^ ((key >> 31) & 0x7FFFFFFF), f32)


def _count0(mask):
    return jnp.sum(jnp.where(mask, 1.0, 0.0), axis=0, keepdims=True)


def _kth_largest(sc_ref, n_keys, k):
    kf = float(k)

    def count_ge(cand_key):
        cand = _order_float(cand_key)
        acc = jnp.zeros((BISECT_ROWS, LANES), f32)
        for c in range(n_keys // BISECT_ROWS):
            acc = jnp.where(sc_ref[c * BISECT_ROWS:(c + 1) * BISECT_ROWS, :] >= cand, acc + 1.0, acc)
        return jnp.sum(acc, axis=0, keepdims=True)

    zero = jnp.zeros((1, LANES), jnp.int32)
    thr = jnp.where(count_ge(zero) >= kf, zero, zero + INT_MIN)

    def bit_step(j, thr):
        cand = thr | lax.shift_left(jnp.int32(1), 30 - j)
        return jnp.where(count_ge(cand) >= kf, cand, thr)

    thr = _order_float(lax.fori_loop(0, 31, bit_step, thr))
    return jnp.where(thr >= F32_LOWEST, thr, F32_LOWEST)


def _topk_bias_t(sc, thr, k, out_ref):
    n_keys = sc.shape[0]
    kf = float(k)
    ge = sc >= thr
    excess = jnp.max(_count0(ge)) - kf

    @pl.when(excess <= 0.0)
    def _():
        out_ref[:n_keys, :] = jnp.where(ge, 0.0, -jnp.inf)

    @pl.when(excess > 0.0)
    def _():
        pos = lax.broadcasted_iota(jnp.int32, sc.shape, 0)
        above = sc > thr
        tie = sc == thr
        need = kf - _count0(above)
        n_bits = max(1, (n_keys - 1).bit_length())

        def step(j, p):
            cand = p + lax.shift_left(jnp.int32(1), n_bits - 1 - j)
            return jnp.where(_count0(tie & (pos < cand)) < need, cand, p)

        last = lax.fori_loop(0, n_bits, step, jnp.zeros(thr.shape, jnp.int32))
        out_ref[:n_keys, :] = jnp.where(above | (tie & (pos <= last)), 0.0, -jnp.inf)


def _dsa_prompt_kernel(q_ref, qi_ref, wt_ref, k_ref, v_ref, ki_ref, o_ref, key_s, thr_s, biast_s, bias_s):
    i = pl.program_id(1)
    k_top = min(TOPK_MAX, SEQ // 4)
    blocks_per_span = KV_SPAN // QUERY_BLOCK

    def run(span):
        n_keys = (span + 1) * KV_SPAN
        ki = ki_ref[:n_keys, :]
        wt = wt_ref[...] * IDX_SCALE
        sc = None
        for hd in range(N_IDX_HEADS):
            d = _dot_nt(ki, qi_ref[:, hd * IDX_DIM:(hd + 1) * IDX_DIM])
            term = jnp.maximum(d, 0.0) * wt[hd:hd + 1, :]
            sc = term if sc is None else sc + term
        pos = lax.broadcasted_iota(jnp.int32, (n_keys, QUERY_BLOCK), 0)
        query = i * QUERY_BLOCK + lax.broadcasted_iota(jnp.int32, (n_keys, QUERY_BLOCK), 1)
        valid = pos <= query
        key_s[:n_keys, :] = jnp.where(valid, sc, -jnp.inf)

        thr_s[...] = jnp.full(thr_s.shape, F32_LOWEST, f32)
        for blk in range(span * blocks_per_span, (span + 1) * blocks_per_span):
            if (blk + 1) * QUERY_BLOCK > k_top:
                @pl.when(i == blk)
                def _(blk=blk):
                    thr_s[0:1, :] = _kth_largest(key_s, (blk + 1) * QUERY_BLOCK, k_top)

        _topk_bias_t(key_s[:n_keys, :], thr_s[0:1, :], k_top, biast_s)
        bias_s[:, :n_keys] = biast_s[:n_keys, :].T
        bias = bias_s[:, :n_keys][None]
        for kh in range(N_KV_HEADS):
            q_rows = jnp.concatenate(
                [q_ref[:, (kh * Q_PER_KV + g) * HEAD_DIM:(kh * Q_PER_KV + g + 1) * HEAD_DIM]
                 for g in range(Q_PER_KV)], axis=0)
            chunks = [slice(c, c + ATT_CHUNK) for c in range(0, n_keys, ATT_CHUNK)]
            s = []
            for ck in chunks:
                s_c = _dot_nt(q_rows, k_ref[ck, kh * HEAD_DIM:(kh + 1) * HEAD_DIM])
                s.append((s_c.reshape(Q_PER_KV, QUERY_BLOCK, ATT_CHUNK) + bias[:, :, ck])
                         .reshape(Q_PER_KV * QUERY_BLOCK, ATT_CHUNK))
            row_max = functools.reduce(jnp.maximum, [jnp.max(s_c, axis=1, keepdims=True) for s_c in s])
            o = None
            for ck, s_c in zip(chunks, s):
                o_c = _dot(jnp.exp2(s_c - row_max).astype(bf16), v_ref[ck, 2 * kh * HEAD_DIM:(2 * kh + 2) * HEAD_DIM])
                o = o_c if o is None else o + o_c
            o = o[:, :HEAD_DIM] / o[:, HEAD_DIM:HEAD_DIM + 1]
            for g in range(Q_PER_KV):
                hd = kh * Q_PER_KV + g
                o_ref[:, hd * HEAD_DIM:(hd + 1) * HEAD_DIM] = o[g * QUERY_BLOCK:(g + 1) * QUERY_BLOCK].astype(bf16)

    for span in range(SEQ // KV_SPAN):
        pl.when(i // blocks_per_span == span)(functools.partial(run, span))


def _dsa_prompt(q, qi, wi_t, kb, vb, kib):
    nq = SEQ // QUERY_BLOCK
    qmap = lambda b, i: (b * nq + i, 0)
    kmap = lambda b, i: (b, 0)
    return pl.pallas_call(
        _dsa_prompt_kernel,
        grid=(BATCH, nq),
        in_specs=[
            pl.BlockSpec((QUERY_BLOCK, Q_W), qmap),
            pl.BlockSpec((QUERY_BLOCK, QI_W), qmap),
            pl.BlockSpec((N_IDX_HEADS, QUERY_BLOCK), lambda b, i: (0, b * nq + i)),
            pl.BlockSpec((SEQ, KV_W), kmap),
            pl.BlockSpec((SEQ, 2 * KV_W), kmap),
            pl.BlockSpec((SEQ, IDX_DIM), kmap),
        ],
        out_specs=pl.BlockSpec((QUERY_BLOCK, Q_W), qmap),
        out_shape=jax.ShapeDtypeStruct((P_ROWS, Q_W), bf16),
        scratch_shapes=[
            pltpu.VMEM((SEQ, QUERY_BLOCK), f32),
            pltpu.VMEM((SUBLANES, QUERY_BLOCK), f32),
            pltpu.VMEM((SEQ, QUERY_BLOCK), f32),
            pltpu.VMEM((QUERY_BLOCK, SEQ), f32),
        ],
        compiler_params=_params("arbitrary", "arbitrary"),
        name="dsa_prompt",
    )(q, qi, wi_t, kb, vb, kib)


def _sidx_kernel(pt_ref, qi_ref, w_ref, kin_ref, cki_hbm, sc_ref, pbuf, sem, *, first_page):
    step = pl.program_id(0)
    slot = step % 2

    def page_copies(st, sl):
        return [pltpu.make_async_copy(cki_hbm.at[first_page + pt_ref[st * SIDX_SEQS + g, j]], pbuf.at[sl, g, j],
                                      sem.at[sl])
                for g in range(SIDX_SEQS) for j in range(N_PAGES)]

    @pl.when(step == 0)
    def _():
        for c in page_copies(0, 0):
            c.start()

    @pl.when(step + 1 < pl.num_programs(0))
    def _():
        for c in page_copies(step + 1, 1 - slot):
            c.start()

    for c in page_copies(step, slot):
        c.wait()

    for g in range(SIDX_SEQS):
        qi = qi_ref[g]
        d = jnp.concatenate([_dot(qi, pbuf[slot, g, j].astype(bf16)) for j in range(N_PAGES)]
                            + [_dot_nt(qi, kin_ref[g])], axis=1)
        d = jnp.maximum(d, 0.0) * w_ref[g]
        sc_ref[g] = jnp.sum(d.reshape(DEC_SEQ, N_IDX_HEADS, S_KEYS), axis=1) * IDX_SCALE


def _sample_index_scores(page_table, qi32, w32, ki_new, cache_kidx, layer):
    n_pool = cache_kidx.shape[0] // (DEPTH // 2)
    seq3 = lambda n, pt: (n, 0, 0)
    grid_spec = pltpu.PrefetchScalarGridSpec(
        num_scalar_prefetch=1,
        grid=(DEC_BATCH // SIDX_SEQS,),
        in_specs=[
            pl.BlockSpec((SIDX_SEQS, DEC_SEQ * N_IDX_HEADS, IDX_DIM), seq3),
            pl.BlockSpec((SIDX_SEQS, DEC_SEQ * N_IDX_HEADS, 1), seq3),
            pl.BlockSpec((SIDX_SEQS, LANES, IDX_DIM), seq3),
            pl.BlockSpec(memory_space=pl.ANY),
        ],
        out_specs=pl.BlockSpec((SIDX_SEQS, DEC_SEQ, S_KEYS), seq3),
        scratch_shapes=[
            pltpu.VMEM((2, SIDX_SEQS, N_PAGES, IDX_DIM, PAGE_SIZE), f32),
            pltpu.SemaphoreType.DMA((2,)),
        ],
    )
    return pl.pallas_call(
        functools.partial(_sidx_kernel, first_page=layer * n_pool),
        grid_spec=grid_spec,
        out_shape=jax.ShapeDtypeStruct((DEC_BATCH, DEC_SEQ, S_KEYS), f32),
        compiler_params=_params("arbitrary"),
        name="dsa_sample_index",
    )(page_table, qi32, w32, ki_new, cache_kidx)


def _stopk_kernel(sc_ref, bias_ref, key_s):
    shape = sc_ref.shape
    pos = lax.broadcasted_iota(jnp.int32, shape, 0)
    t = (pl.program_id(0) * LANES + lax.broadcasted_iota(jnp.int32, shape, 1)) % DEC_SEQ
    valid = pos <= t + PAST_LEN
    k_top = min(TOPK_MAX, (PAST_LEN + DEC_SEQ) // 4)
    key_s[...] = jnp.where(valid, sc_ref[...], -jnp.inf)
    thr = _kth_largest(key_s, S_KEYS, k_top)
    _topk_bias_t(key_s[...], thr, k_top, bias_ref)


def _sample_topk_bias(sc_t):
    spec = pl.BlockSpec((S_KEYS, LANES), lambda i: (0, i))
    return pl.pallas_call(
        _stopk_kernel,
        grid=(S_ROWS // LANES,),
        in_specs=[spec],
        out_specs=spec,
        out_shape=jax.ShapeDtypeStruct((S_KEYS, S_ROWS), f32),
        scratch_shapes=[pltpu.VMEM((S_KEYS, LANES), f32)],
        compiler_params=_params("arbitrary"),
        name="dsa_sample_topk",
    )(sc_t)


def _satt_kernel(pt_ref, q_ref, bias_ref, kn_ref, vn_ref, ck_hbm, cv_hbm, o_ref,
                 kbuf, vbuf, sem, *, first_page):
    step = pl.program_id(0)
    slot = step % 2

    def page_copies(st, sl):
        copies = []
        for g in range(SATT_SEQS):
            for j in range(N_PAGES):
                page = first_page + pt_ref[st * SATT_SEQS + g, j]
                copies.append(pltpu.make_async_copy(ck_hbm.at[page], kbuf.at[sl, g, j], sem.at[sl, 0]))
                copies.append(pltpu.make_async_copy(cv_hbm.at[page], vbuf.at[sl, g, j], sem.at[sl, 1]))
        return copies

    @pl.when(step == 0)
    def _():
        for c in page_copies(0, 0):
            c.start()

    @pl.when(step + 1 < pl.num_programs(0))
    def _():
        for c in page_copies(step + 1, 1 - slot):
            c.start()

    for c in page_copies(step, slot):
        c.wait()

    for g in range(SATT_SEQS):
        for kh in range(N_KV_HEADS):
            cols = slice(kh * HEAD_DIM, (kh + 1) * HEAD_DIM)
            head_rows = pl.ds(kh, PAGE_SIZE, stride=N_KV_HEADS)
            k_all = jnp.concatenate([kbuf[slot, g, j, head_rows, :].astype(bf16) for j in range(N_PAGES)]
                                    + [kn_ref[g, :, cols]], axis=0)
            v_all = jnp.concatenate([vbuf[slot, g, j, head_rows, :].astype(bf16) for j in range(N_PAGES)]
                                    + [vn_ref[g, :, cols]], axis=0)
            s = _dot_nt(q_ref[g, kh], k_all)
            s = (s.reshape(Q_PER_KV, DEC_SEQ, S_KEYS) + bias_ref[g][None]).reshape(Q_PER_KV * DEC_SEQ, S_KEYS)
            p = jnp.exp2(s - jnp.max(s, axis=1, keepdims=True))
            denom = jnp.sum(p, axis=1, keepdims=True)
            o_ref[g, kh] = (_dot(p.astype(bf16), v_all) / denom).astype(bf16)


def _sample_attention(page_table, q16, bias, k_new, v_new, cache_k, cache_v, layer):
    n_pool = cache_k.shape[0] // (DEPTH // 2)
    rows = DEC_SEQ * Q_PER_KV
    seq3 = lambda n, pt: (n, 0, 0)
    seq4 = lambda n, pt: (n, 0, 0, 0)
    page_shape = (PAGE_SIZE * N_KV_HEADS, HEAD_DIM)
    grid_spec = pltpu.PrefetchScalarGridSpec(
        num_scalar_prefetch=1,
        grid=(DEC_BATCH // SATT_SEQS,),
        in_specs=[
            pl.BlockSpec((SATT_SEQS, N_KV_HEADS, rows, HEAD_DIM), seq4),
            pl.BlockSpec((SATT_SEQS, DEC_SEQ, S_KEYS), seq3),
            pl.BlockSpec((SATT_SEQS, LANES, KV_W), seq3),
            pl.BlockSpec((SATT_SEQS, LANES, KV_W), seq3),
            pl.BlockSpec(memory_space=pl.ANY),
            pl.BlockSpec(memory_space=pl.ANY),
        ],
        out_specs=pl.BlockSpec((SATT_SEQS, N_KV_HEADS, rows, HEAD_DIM), seq4),
        scratch_shapes=[
            pltpu.VMEM((2, SATT_SEQS, N_PAGES) + page_shape, f32),
            pltpu.VMEM((2, SATT_SEQS, N_PAGES) + page_shape, f32),
            pltpu.SemaphoreType.DMA((2, 2)),
        ],
    )
    return pl.pallas_call(
        functools.partial(_satt_kernel, first_page=layer * n_pool),
        grid_spec=grid_spec,
        out_shape=jax.ShapeDtypeStruct((DEC_BATCH, N_KV_HEADS, rows, HEAD_DIM), bf16),
        compiler_params=_params("arbitrary"),
        name="dsa_sample_attn",
    )(page_table, q16, bias, k_new, v_new, cache_k, cache_v)


def _rope_tables():
    pos = jnp.concatenate([jnp.arange(SEQ), PAST_LEN + jnp.arange(TM) % DEC_SEQ]).astype(f32)

    def pattern(dim, n_rot_groups):
        r = dim // ROT_DIV
        half = r // 2
        inv = ROPE_THETA ** (-jnp.arange(half, dtype=f32) * 2.0 / r)
        ang = pos[:, None] * inv[None, :]
        cos, sin = jnp.cos(ang), jnp.sin(ang)
        ones = jnp.ones((pos.shape[0], dim - r), f32)
        zeros = jnp.zeros((pos.shape[0], dim - r), f32)
        zh = jnp.zeros_like(sin)
        c = jnp.concatenate([cos, cos, ones], axis=1)
        lo = jnp.concatenate([-sin, zh, zeros], axis=1)
        hi = jnp.concatenate([zh, sin, zeros], axis=1)
        ident = (jnp.ones((pos.shape[0], dim), f32), jnp.zeros((pos.shape[0], dim), f32),
                 jnp.zeros((pos.shape[0], dim), f32))
        reps = LANES // dim
        parts = [(c, lo, hi) if j < n_rot_groups else ident for j in range(reps)]
        return jnp.stack([jnp.concatenate([p_[t] for p_ in parts], axis=1) for t in range(3)])

    return pattern(HEAD_DIM, 1), pattern(IDX_DIM, 2), pattern(IDX_DIM, 1)


def _sample_mix_tables(w_s, b_s):
    t = jnp.arange(SUBLANES) % DEC_SEQ
    coefs = []
    for d in range(DEC_SEQ):
        c = jnp.where((t - d >= 0)[None, :], w_s[:, t, jnp.maximum(t - d, 0)], 0.0)
        coefs.append(jnp.repeat(c.T, GROUP_DIM_A, axis=1))
    bias = jnp.repeat(b_s[:, t].T, GROUP_DIM_A, axis=1)
    return jnp.stack(coefs), bias


def kernel(x_prompt, x_sample, cache_k, cache_v, cache_kidx, page_table, norm_mix, norm_ffn,
           a_w_in, a_v_gain, a_w_s, a_b_s, a_w_out, b_w_in, b_w_out, ffn_w1, ffn_w2, norm_final):
    x = (x_prompt.reshape(P_ROWS, D_MODEL), x_sample.reshape(S_ROWS, D_MODEL))
    kv_new = None
    rope_h, rope_i, rope_k = _rope_tables()
    n_pool = cache_k.shape[1]
    ck = cache_k.reshape(-1, PAGE_SIZE * N_KV_HEADS, HEAD_DIM)
    cv = cache_v.reshape(-1, PAGE_SIZE * N_KV_HEADS, HEAD_DIM)
    cki = jnp.swapaxes(cache_kidx, 2, 3).reshape(-1, IDX_DIM, PAGE_SIZE)
    g_final = norm_final.reshape(1, D_MODEL)
    a_w_in_b, a_w_out_b, b_w_out_b = a_w_in.astype(bf16), a_w_out.astype(bf16), b_w_out.astype(bf16)
    b_w_in_b = jnp.pad(b_w_in, ((0, 0), (0, 0), (0, B_PROJ_PAD - B_PROJ))).astype(bf16)
    ffn_w1_b, ffn_w2_b = ffn_w1.astype(bf16), ffn_w2.astype(bf16)
    outs = {name: [] for name in ("kip", "kis", "cvp", "cvs")}

    for i in range(DEPTH):
        j = i // 2
        g_mix = norm_mix[i].reshape(1, D_MODEL)
        if i % 2 == 0:
            scoef, sbias = _sample_mix_tables(a_w_s[j], a_b_s[j])
            bias = jnp.repeat(a_b_s[j].T, GROUP_DIM_A, axis=1)
            *m, cvp, cvs = _mixa(x, g_mix, a_w_in_b, j, a_v_gain[j].reshape(1, D_A), a_w_s[j], bias, scoef, sbias)
            outs["cvp"].append(cvp)
            outs["cvs"].append(cvs.reshape(DEC_BATCH, DEC_SEQ, D_A))
            w_out = a_w_out_b
        else:
            q, kb, vb, qi, kiwi, kib, wi_t, *kv_new = _proj(x[0], g_mix, b_w_in_b, rope_h, rope_i, rope_k, j, kv_new)
            m_p = _dsa_prompt(q, qi, wi_t, kb, vb, kib)
            qi32 = qi[P_ROWS:].reshape(DEC_BATCH, DEC_SEQ * N_IDX_HEADS, IDX_DIM)
            w32 = kiwi[P_ROWS:, IDX_DIM:IDX_DIM + N_IDX_HEADS].reshape(DEC_BATCH, DEC_SEQ * N_IDX_HEADS, 1)
            pad_rows = lambda a: jnp.pad(a.reshape(DEC_BATCH, DEC_SEQ, -1), ((0, 0), (0, LANES - DEC_SEQ), (0, 0)))
            sc = _sample_index_scores(page_table, qi32, w32, pad_rows(kib[P_ROWS:]), cki, j)
            bias = _sample_topk_bias(sc.reshape(S_ROWS, S_KEYS).T).T.reshape(DEC_BATCH, DEC_SEQ, S_KEYS)
            q16 = q[P_ROWS:].reshape(DEC_BATCH, DEC_SEQ, N_KV_HEADS, Q_PER_KV, HEAD_DIM)
            q16 = q16.transpose(0, 2, 3, 1, 4).reshape(DEC_BATCH, N_KV_HEADS, Q_PER_KV * DEC_SEQ, HEAD_DIM)
            v_new = vb[P_ROWS:].reshape(S_ROWS, N_KV_HEADS, 2, HEAD_DIM)[:, :, 0]
            o16 = _sample_attention(page_table, q16, bias, pad_rows(kb[P_ROWS:]), pad_rows(v_new), ck, cv, j)
            m_s = o16.reshape(DEC_BATCH, N_KV_HEADS, Q_PER_KV, DEC_SEQ, HEAD_DIM).transpose(0, 3, 1, 2, 4)
            m = (m_p, m_s.reshape(S_ROWS, Q_W))
            outs["kip"].append(kiwi[:P_ROWS, :IDX_DIM].reshape(BATCH, SEQ, IDX_DIM))
            outs["kis"].append(kiwi[P_ROWS:, :IDX_DIM].reshape(DEC_BATCH, DEC_SEQ, IDX_DIM))
            w_out = b_w_out_b
        x = tuple(_post(x, tuple(m), w_out, j, norm_ffn[i].reshape(1, D_MODEL), ffn_w1_b, ffn_w2_b, i, g_final,
                        final=(i == DEPTH - 1)))

    y_prompt = x[0].reshape(BATCH, SEQ, D_MODEL)
    y_sample = x[1].reshape(DEC_BATCH, DEC_SEQ, D_MODEL)
    n_layers = DEPTH // 2
    kp, ks, vp, vs = kv_new
    prompt_kv = lambda a: a.reshape(n_layers, BATCH, SEQ, N_KV_HEADS, HEAD_DIM)
    sample_kv = lambda a: a.reshape(n_layers, DEC_BATCH, DEC_SEQ, N_KV_HEADS, HEAD_DIM)
    st = jnp.stack
    return (y_prompt, y_sample, prompt_kv(kp), prompt_kv(vp), st(outs["kip"]), sample_kv(ks), sample_kv(vs),
            st(outs["kis"]), st(outs["cvp"]), st(outs["cvs"]))
```

```python
import functools

import jax
import jax.numpy as jnp
import numpy as np
from jax import lax
from jax.experimental import pallas as pl
from jax.experimental.pallas import tpu as pltpu

D_MODEL = 1024
BATCH = 8
SEQ = 2048
DEPTH = 4
DEC_BATCH = 128
DEC_SEQ = 4
PAST_LEN = 2048
PAGE_SIZE = 128
N_PAGES = PAST_LEN // PAGE_SIZE
CHUNK = 128
D_A = D_MODEL
N_GROUPS_A = 8
GROUP_DIM_A = D_A // N_GROUPS_A
N_HEADS = 8
HEAD_DIM = D_MODEL // N_HEADS
N_KV_HEADS = 2
Q_PER_KV = N_HEADS // N_KV_HEADS
N_IDX_HEADS = 8
IDX_DIM = 64
TOPK_MAX = 256
QUERY_BLOCK = 128
ROPE_THETA = 500000.0
ROT_DIV = 4
IDX_SCALE = float((IDX_DIM * N_IDX_HEADS) ** -0.5)
ATTN_SCALE = float(HEAD_DIM ** -0.5)
Q_W = N_HEADS * HEAD_DIM
KV_W = N_KV_HEADS * HEAD_DIM
QI_W = N_IDX_HEADS * IDX_DIM
B_PROJ = Q_W + 2 * KV_W + QI_W + IDX_DIM + N_IDX_HEADS
D_FF = 4 * D_MODEL
EPS = 1e-6

LANES = 128
SUBLANES = 8
P_ROWS = BATCH * SEQ
S_ROWS = DEC_BATCH * DEC_SEQ
ROWS = P_ROWS + S_ROWS
TM = 512
N_TILES = ROWS // TM
P_TILES = P_ROWS // TM
TILES_PER_SEQ = SEQ // TM
CHUNKS_PER_TILE = TM // CHUNK
FF_TILE = 1024
B_PROJ_PAD = Q_W + 2 * KV_W + QI_W + LANES
KIWI_OFF = Q_W + 2 * KV_W + QI_W
S_KEYS = PAST_LEN + LANES
ATT_CHUNK = 256
KV_SPAN = 512
VMEM_LIMIT = 56 * 1024 * 1024
INT_MIN = -(2 ** 31)
F32_LOWEST = -3.4028234663852886e38
LOG2_E = 1.4426950408889634
SATT_SEQS = 2
SIDX_SEQS = 4
BISECT_ROWS = 64

f32 = jnp.float32
bf16 = jnp.bfloat16

assert S_ROWS == TM and P_ROWS % TM == 0 and SEQ % TM == 0


def _params(*sem):
    return pltpu.CompilerParams(dimension_semantics=sem, vmem_limit_bytes=VMEM_LIMIT)


def _rms(x, g):
    return x * lax.rsqrt(jnp.mean(x * x, axis=-1, keepdims=True) + EPS) * g


def _gelu(x):
    a = -2.0 * LOG2_E * 0.7978845608028654
    return x / (1.0 + jnp.exp2(x * (a + (a * 0.044715) * (x * x))))


def _dot(a, b):
    return jnp.dot(a, b, preferred_element_type=f32)


def _dot_nt(a, b):
    return lax.dot_general(a, b, (((1,), (1,)), ((), ())), preferred_element_type=f32)


def _prompt_tile(i):
    return (jnp.minimum(i, P_TILES - 1), 0)


def _first_tile(i):
    return (0, 0)


def _pair_specs(width):
    return [pl.BlockSpec((TM, width), _prompt_tile), pl.BlockSpec((S_ROWS, width), _first_tile)]


def _layer_spec(shape, layer, **kwargs):
    return pl.BlockSpec((None,) + shape, lambda i: (layer, 0, 0), **kwargs)


def _rows_specs(n_arrays, width):
    return [pl.BlockSpec((TM, width), lambda i: (i, 0))] if n_arrays == 1 else _pair_specs(width)


def _rows_tile(i, refs):
    if len(refs) == 1:
        return refs[0][...]
    p_ref, s_ref = refs
    return jnp.where(i == P_TILES, s_ref[...], p_ref[...])


def _store_rows_tile(i, refs, value):
    if len(refs) == 1:
        refs[0][...] = value
        return
    p_ref, s_ref = refs

    @pl.when(i < P_TILES)
    def _():
        p_ref[...] = value

    @pl.when(i == P_TILES)
    def _():
        s_ref[...] = value


def _mixa_kernel(*refs, n_x):
    x_refs = refs[:n_x]
    g_ref, win_ref, gain_ref, ws_ref, bias_ref, scoef_ref, sbias_ref, m_ref, cvp_ref, cvs_ref = refs[n_x:]
    i = pl.program_id(0)
    h = _rms(_rows_tile(i, x_refs), g_ref[...]).astype(bf16)
    uv = _gelu(_dot(h, win_ref[...]))
    u = uv[:, :D_A]
    v = uv[:, D_A:]
    vc = v - jnp.mean(v, axis=-1, keepdims=True)
    vn = vc * lax.rsqrt(jnp.mean(vc * vc, axis=-1, keepdims=True) + EPS) * gain_ref[...]

    @pl.when(i < P_TILES)
    def _prompt():
        t_idx = lax.broadcasted_iota(jnp.int32, (CHUNK, CHUNK), 0)
        s_idx = lax.broadcasted_iota(jnp.int32, (CHUNK, CHUNK), 1)
        causal = s_idx <= t_idx
        ws = [jnp.where(causal, ws_ref[g], 0.0).astype(bf16) for g in range(N_GROUPS_A)]
        for c in range(CHUNKS_PER_TILE):
            rows = slice(c * CHUNK, (c + 1) * CHUNK)
            v_c = vn[rows].astype(bf16)
            mixed = jnp.concatenate(
                [_dot(ws[g], v_c[:, g * GROUP_DIM_A:(g + 1) * GROUP_DIM_A]) for g in range(N_GROUPS_A)],
                axis=1) + bias_ref[...]
            m_ref[rows, :] = (u[rows] * mixed).astype(bf16)

        @pl.when(i % TILES_PER_SEQ == TILES_PER_SEQ - 1)
        def _():
            cvp_ref[0] = vn[TM - CHUNK:]

    @pl.when(i == P_TILES)
    def _sample():
        grp = TM // SUBLANES
        v3 = vn.reshape(grp, SUBLANES, D_A)
        mixed = sbias_ref[...][None] + scoef_ref[0][None] * v3
        for d in range(1, DEC_SEQ):
            mixed = mixed + scoef_ref[d][None] * pltpu.roll(v3, d, 1)
        m_ref[...] = (u.reshape(grp, SUBLANES, D_A) * mixed).reshape(TM, D_A).astype(bf16)
        cvs_ref[...] = vn


def _mixa(x, g, w_in, layer, gain, ws, bias, scoef, sbias):
    const2 = lambda i: (0, 0)
    const3 = lambda i: (0, 0, 0)
    return pl.pallas_call(
        functools.partial(_mixa_kernel, n_x=len(x)),
        grid=(N_TILES,),
        in_specs=_rows_specs(len(x), D_MODEL) + [
            pl.BlockSpec((1, D_MODEL), const2),
            _layer_spec((D_MODEL, 2 * D_A), layer),
            pl.BlockSpec((1, D_A), const2),
            pl.BlockSpec((N_GROUPS_A, CHUNK, CHUNK), const3),
            pl.BlockSpec((CHUNK, D_A), const2),
            pl.BlockSpec((DEC_SEQ, SUBLANES, D_A), const3),
            pl.BlockSpec((SUBLANES, D_A), const2),
        ],
        out_specs=[
            pl.BlockSpec((TM, D_A), lambda i: (i, 0)),
            pl.BlockSpec((1, CHUNK, D_A), lambda i: (jnp.minimum(i // TILES_PER_SEQ, BATCH - 1), 0, 0)),
            pl.BlockSpec((S_ROWS, D_A), const2),
        ],
        out_shape=[
            jax.ShapeDtypeStruct((ROWS, D_A), bf16),
            jax.ShapeDtypeStruct((BATCH, CHUNK, D_A), f32),
            jax.ShapeDtypeStruct((S_ROWS, D_A), f32),
        ],
        compiler_params=_params("arbitrary"),
        name="mixer_a",
    )(*x, g, w_in, gain, ws, bias, scoef, sbias)


def _post_kernel(*refs, n_x, n_m, final):
    x_refs, m_refs = refs[:n_x], refs[n_x:n_x + n_m]
    wout_ref, g_ref, w1_ref, w2_ref, gf_ref = refs[n_x + n_m:n_x + n_m + 5]
    y_refs = refs[n_x + n_m + 5:]
    i = pl.program_id(0)
    x1 = _rows_tile(i, x_refs) + _dot(_rows_tile(i, m_refs), wout_ref[...])
    h = _rms(x1, g_ref[...]).astype(bf16)
    y = x1
    for c in range(D_FF // FF_TILE):
        a = jnp.maximum(_dot(h, w1_ref[:, c * FF_TILE:(c + 1) * FF_TILE]), 0.0)
        y = y + _dot((a * a).astype(bf16), w2_ref[c * FF_TILE:(c + 1) * FF_TILE, :])
    if final:
        y = _rms(y, gf_ref[...])
    _store_rows_tile(i, y_refs, y)


def _post(x, m, w_out, mixer_layer, g, w1, w2, layer, g_final, final):
    const2 = lambda i: (0, 0)
    resident = dict(pipeline_mode=pl.Buffered(1))
    if final:
        out_specs = _pair_specs(D_MODEL)
        out_shape = [jax.ShapeDtypeStruct((P_ROWS, D_MODEL), f32), jax.ShapeDtypeStruct((S_ROWS, D_MODEL), f32)]
    else:
        out_specs = _rows_specs(1, D_MODEL)
        out_shape = [jax.ShapeDtypeStruct((ROWS, D_MODEL), f32)]
    return pl.pallas_call(
        functools.partial(_post_kernel, n_x=len(x), n_m=len(m), final=final),
        grid=(N_TILES,),
        in_specs=_rows_specs(len(x), D_MODEL) + _rows_specs(len(m), D_MODEL) + [
            _layer_spec((D_MODEL, D_MODEL), mixer_layer, **resident),
            pl.BlockSpec((1, D_MODEL), const2),
            _layer_spec((D_MODEL, D_FF), layer, **resident),
            _layer_spec((D_FF, D_MODEL), layer, **resident),
            pl.BlockSpec((1, D_MODEL), const2),
        ],
        out_specs=out_specs,
        out_shape=out_shape,
        compiler_params=_params("arbitrary"),
        name="post_ffn",
    )(*x, *m, w_out, g, w1, w2, g_final)


def _rope(x, cos, sin_lo, sin_hi, half):
    return x * cos + pltpu.roll(x, LANES - half, 1) * sin_lo + pltpu.roll(x, half, 1) * sin_hi


def _proj_kernel(*refs, n_alias):
    x_ref, g_ref, w_ref, rh_ref, ri_ref, rk_ref = refs[:6]
    (q_ref, kb_ref, vb_ref, qi_ref, kiwi_ref, kib_ref, wit_ref,
     kp_ref, ks_ref, vp_ref, vs_ref) = refs[6 + n_alias:]
    i = pl.program_id(0)
    h = _rms(x_ref[...], g_ref[...]).astype(bf16)
    p = _dot(h, w_ref[...])
    half_h = HEAD_DIM // ROT_DIV // 2
    half_i = IDX_DIM // ROT_DIV // 2
    ch, slh, shh = rh_ref[0], rh_ref[1], rh_ref[2]
    for hd in range(N_HEADS):
        cols = slice(hd * HEAD_DIM, (hd + 1) * HEAD_DIM)
        q_ref[:, cols] = (_rope(p[:, cols], ch, slh, shh, half_h) * (ATTN_SCALE * LOG2_E)).astype(bf16)
    one_hot = jnp.where(lax.broadcasted_iota(jnp.int32, (TM, HEAD_DIM), 1) == 0, 1.0, 0.0).astype(bf16)
    for kh in range(N_KV_HEADS):
        cols = slice(kh * HEAD_DIM, (kh + 1) * HEAD_DIM)
        kr = _rope(p[:, Q_W + kh * HEAD_DIM:Q_W + (kh + 1) * HEAD_DIM], ch, slh, shh, half_h)
        v = p[:, Q_W + KV_W + kh * HEAD_DIM:Q_W + KV_W + (kh + 1) * HEAD_DIM]
        kb_ref[:, cols] = kr.astype(bf16)
        vb_ref[:, 2 * kh * HEAD_DIM:(2 * kh + 1) * HEAD_DIM] = v.astype(bf16)
        vb_ref[:, (2 * kh + 1) * HEAD_DIM:(2 * kh + 2) * HEAD_DIM] = one_hot
        head_rows = pl.ds(kh, TM, stride=N_KV_HEADS)

        for slot in range(kp_ref.shape[0]):
            @pl.when(i < P_TILES)
            def _(slot=slot):
                kp_ref[slot, head_rows, :] = kr
                vp_ref[slot, head_rows, :] = v

            @pl.when(i == P_TILES)
            def _(slot=slot):
                ks_ref[slot, head_rows, :] = kr
                vs_ref[slot, head_rows, :] = v
    ci, sli, shi = ri_ref[0], ri_ref[1], ri_ref[2]
    off = Q_W + 2 * KV_W
    for j in range(QI_W // LANES):
        cols = slice(j * LANES, (j + 1) * LANES)
        qi_ref[:, cols] = _rope(p[:, off + j * LANES:off + (j + 1) * LANES], ci, sli, shi, half_i).astype(bf16)
    kiwi = _rope(p[:, KIWI_OFF:KIWI_OFF + LANES], rk_ref[0], rk_ref[1], rk_ref[2], half_i)
    kiwi_ref[...] = kiwi
    kib_ref[...] = kiwi[:, :IDX_DIM].astype(bf16)
    wit_ref[...] = kiwi.T[IDX_DIM:IDX_DIM + N_IDX_HEADS, :]


def _proj(x, g, w, rope_h, rope_i, rope_k, layer, kv_prev):
    const2 = lambda i: (0, 0)
    rope_map = lambda i: (0, jnp.where(i < P_TILES, i % TILES_PER_SEQ, TILES_PER_SEQ), 0)
    row = lambda i: (i, 0)
    widths = [(Q_W, bf16), (KV_W, bf16), (2 * KV_W, bf16), (QI_W, bf16), (LANES, f32), (IDX_DIM, bf16)]
    n_layers = DEPTH // 2
    kv_rows = TM * N_KV_HEADS
    slots, first = (n_layers, 0) if kv_prev is None else (1, layer)
    kv_specs = [pl.BlockSpec((slots, kv_rows, HEAD_DIM), lambda i: (first, jnp.minimum(i, P_TILES - 1), 0)),
                pl.BlockSpec((slots, kv_rows, HEAD_DIM), lambda i: (first, 0, 0))] * 2
    kv_shapes = [jax.ShapeDtypeStruct((n_layers, P_ROWS * N_KV_HEADS, HEAD_DIM), f32),
                 jax.ShapeDtypeStruct((n_layers, S_ROWS * N_KV_HEADS, HEAD_DIM), f32)] * 2
    n_in = 6
    n_out = len(widths) + 1
    alias_in = [] if kv_prev is None else list(kv_prev)
    return pl.pallas_call(
        functools.partial(_proj_kernel, n_alias=len(alias_in)),
        grid=(N_TILES,),
        in_specs=_rows_specs(1, D_MODEL) + [
            pl.BlockSpec((1, D_MODEL), const2),
            _layer_spec((D_MODEL, B_PROJ_PAD), layer),
            pl.BlockSpec((3, TM, LANES), rope_map),
            pl.BlockSpec((3, TM, LANES), rope_map),
            pl.BlockSpec((3, TM, LANES), rope_map),
        ] + [pl.BlockSpec(memory_space=pl.ANY)] * len(alias_in),
        out_specs=[pl.BlockSpec((TM, w_), row) for w_, _ in widths]
        + [pl.BlockSpec((N_IDX_HEADS, TM), lambda i: (0, i))] + kv_specs,
        out_shape=[jax.ShapeDtypeStruct((ROWS, w_), dt) for w_, dt in widths]
        + [jax.ShapeDtypeStruct((N_IDX_HEADS, ROWS), f32)] + kv_shapes,
        input_output_aliases={n_in + a: n_out + a for a in range(len(alias_in))},
        compiler_params=_params("arbitrary"),
        name="dsa_proj",
    )(x, g, w, rope_h, rope_i, rope_k, *alias_in)


def _order_float(key):
    return pltpu.bitcast(key ^ ((key >> 31) & 0x7FFFFFFF), f32)


def _count0(mask):
    return jnp.sum(jnp.where(mask, 1.0, 0.0), axis=0, keepdims=True)


def _kth_largest(sc_ref, n_keys, k):
    kf = float(k)

    def count_ge(cand_key):
        cand = _order_float(cand_key)
        acc = jnp.zeros((BISECT_ROWS, LANES), f32)
        for c in range(n_keys // BISECT_ROWS):
            acc = jnp.where(sc_ref[c * BISECT_ROWS:(c + 1) * BISECT_ROWS, :] >= cand, acc + 1.0, acc)
        return jnp.sum(acc, axis=0, keepdims=True)

    zero = jnp.zeros((1, LANES), jnp.int32)
    thr = jnp.where(count_ge(zero) >= kf, zero, zero + INT_MIN)

    def bit_step(j, thr):
        cand = thr | lax.shift_left(jnp.int32(1), 30 - j)
        return jnp.where(count_ge(cand) >= kf, cand, thr)

    thr = _order_float(lax.fori_loop(0, 31, bit_step, thr))
    return jnp.where(thr >= F32_LOWEST, thr, F32_LOWEST)


def _topk_bias_t(sc, thr, k, out_ref):
    n_keys = sc.shape[0]
    kf = float(k)
    ge = sc >= thr
    excess = jnp.max(_count0(ge)) - kf

    @pl.when(excess <= 0.0)
    def _():
        out_ref[:n_keys, :] = jnp.where(ge, 0.0, -jnp.inf)

    @pl.when(excess > 0.0)
    def _():
        pos = lax.broadcasted_iota(jnp.int32, sc.shape, 0)
        above = sc > thr
        tie = sc == thr
        need = kf - _count0(above)
        n_bits = max(1, (n_keys - 1).bit_length())

        def step(j, p):
            cand = p + lax.shift_left(jnp.int32(1), n_bits - 1 - j)
            return jnp.where(_count0(tie & (pos < cand)) < need, cand, p)

        last = lax.fori_loop(0, n_bits, step, jnp.zeros(thr.shape, jnp.int32))
        out_ref[:n_keys, :] = jnp.where(above | (tie & (pos <= last)), 0.0, -jnp.inf)


def _dsa_prompt_kernel(q_ref, qi_ref, wt_ref, k_ref, v_ref, ki_ref, o_ref, key_s, thr_s, biast_s, bias_s):
    i = pl.program_id(1)
    k_top = min(TOPK_MAX, SEQ // 4)
    blocks_per_span = KV_SPAN // QUERY_BLOCK

    def run(span):
        n_keys = (span + 1) * KV_SPAN
        ki = ki_ref[:n_keys, :]
        wt = wt_ref[...] * IDX_SCALE
        sc = None
        for hd in range(N_IDX_HEADS):
            d = _dot_nt(ki, qi_ref[:, hd * IDX_DIM:(hd + 1) * IDX_DIM])
            term = jnp.maximum(d, 0.0) * wt[hd:hd + 1, :]
            sc = term if sc is None else sc + term
        pos = lax.broadcasted_iota(jnp.int32, (n_keys, QUERY_BLOCK), 0)
        query = i * QUERY_BLOCK + lax.broadcasted_iota(jnp.int32, (n_keys, QUERY_BLOCK), 1)
        valid = pos <= query
        key_s[:n_keys, :] = jnp.where(valid, sc, -jnp.inf)

        thr_s[...] = jnp.full(thr_s.shape, F32_LOWEST, f32)
        for blk in range(span * blocks_per_span, (span + 1) * blocks_per_span):
            if (blk + 1) * QUERY_BLOCK > k_top:
                @pl.when(i == blk)
                def _(blk=blk):
                    thr_s[0:1, :] = _kth_largest(key_s, (blk + 1) * QUERY_BLOCK, k_top)

        _topk_bias_t(key_s[:n_keys, :], thr_s[0:1, :], k_top, biast_s)
        bias_s[:, :n_keys] = biast_s[:n_keys, :].T
        bias = bias_s[:, :n_keys][None]
        for kh in range(N_KV_HEADS):
            q_rows = jnp.concatenate(
                [q_ref[:, (kh * Q_PER_KV + g) * HEAD_DIM:(kh * Q_PER_KV + g + 1) * HEAD_DIM]
                 for g in range(Q_PER_KV)], axis=0)
            chunks = [slice(c, c + ATT_CHUNK) for c in range(0, n_keys, ATT_CHUNK)]
            s = []
            for ck in chunks:
                s_c = _dot_nt(q_rows, k_ref[ck, kh * HEAD_DIM:(kh + 1) * HEAD_DIM])
                s.append((s_c.reshape(Q_PER_KV, QUERY_BLOCK, ATT_CHUNK) + bias[:, :, ck])
                         .reshape(Q_PER_KV * QUERY_BLOCK, ATT_CHUNK))
            row_max = functools.reduce(jnp.maximum, [jnp.max(s_c, axis=1, keepdims=True) for s_c in s])
            o = None
            for ck, s_c in zip(chunks, s):
                o_c = _dot(jnp.exp2(s_c - row_max).astype(bf16), v_ref[ck, 2 * kh * HEAD_DIM:(2 * kh + 2) * HEAD_DIM])
                o = o_c if o is None else o + o_c
            o = o[:, :HEAD_DIM] / o[:, HEAD_DIM:HEAD_DIM + 1]
            for g in range(Q_PER_KV):
                hd = kh * Q_PER_KV + g
                o_ref[:, hd * HEAD_DIM:(hd + 1) * HEAD_DIM] = o[g * QUERY_BLOCK:(g + 1) * QUERY_BLOCK].astype(bf16)

    for span in range(SEQ // KV_SPAN):
        pl.when(i // blocks_per_span == span)(functools.partial(run, span))


def _dsa_prompt(q, qi, wi_t, kb, vb, kib):
    nq = SEQ // QUERY_BLOCK
    qmap = lambda b, i: (b * nq + i, 0)
    kmap = lambda b, i: (b, 0)
    return pl.pallas_call(
        _dsa_prompt_kernel,
        grid=(BATCH, nq),
        in_specs=[
            pl.BlockSpec((QUERY_BLOCK, Q_W), qmap),
            pl.BlockSpec((QUERY_BLOCK, QI_W), qmap),
            pl.BlockSpec((N_IDX_HEADS, QUERY_BLOCK), lambda b, i: (0, b * nq + i)),
            pl.BlockSpec((SEQ, KV_W), kmap),
            pl.BlockSpec((SEQ, 2 * KV_W), kmap),
            pl.BlockSpec((SEQ, IDX_DIM), kmap),
        ],
        out_specs=pl.BlockSpec((QUERY_BLOCK, Q_W), qmap),
        out_shape=jax.ShapeDtypeStruct((P_ROWS, Q_W), bf16),
        scratch_shapes=[
            pltpu.VMEM((SEQ, QUERY_BLOCK), f32),
            pltpu.VMEM((SUBLANES, QUERY_BLOCK), f32),
            pltpu.VMEM((SEQ, QUERY_BLOCK), f32),
            pltpu.VMEM((QUERY_BLOCK, SEQ), f32),
        ],
        compiler_params=_params("arbitrary", "arbitrary"),
        name="dsa_prompt",
    )(q, qi, wi_t, kb, vb, kib)


def _sidx_kernel(pt_ref, qi_ref, w_ref, kin_ref, cki_hbm, sc_ref, pbuf, sem, *, first_page):
    step = pl.program_id(0)
    slot = step % 2

    def page_copies(st, sl):
        return [pltpu.make_async_copy(cki_hbm.at[first_page + pt_ref[st * SIDX_SEQS + g, j]], pbuf.at[sl, g, j],
                                      sem.at[sl])
                for g in range(SIDX_SEQS) for j in range(N_PAGES)]

    @pl.when(step == 0)
    def _():
        for c in page_copies(0, 0):
            c.start()

    @pl.when(step + 1 < pl.num_programs(0))
    def _():
        for c in page_copies(step + 1, 1 - slot):
            c.start()

    for c in page_copies(step, slot):
        c.wait()

    for g in range(SIDX_SEQS):
        qi = qi_ref[g]
        d = jnp.concatenate([_dot(qi, pbuf[slot, g, j].astype(bf16)) for j in range(N_PAGES)]
                            + [_dot_nt(qi, kin_ref[g])], axis=1)
        d = jnp.maximum(d, 0.0) * w_ref[g]
        sc_ref[g] = jnp.sum(d.reshape(DEC_SEQ, N_IDX_HEADS, S_KEYS), axis=1) * IDX_SCALE


def _sample_index_scores(page_table, qi32, w32, ki_new, cache_kidx, layer):
    n_pool = cache_kidx.shape[0] // (DEPTH // 2)
    seq3 = lambda n, pt: (n, 0, 0)
    grid_spec = pltpu.PrefetchScalarGridSpec(
        num_scalar_prefetch=1,
        grid=(DEC_BATCH // SIDX_SEQS,),
        in_specs=[
            pl.BlockSpec((SIDX_SEQS, DEC_SEQ * N_IDX_HEADS, IDX_DIM), seq3),
            pl.BlockSpec((SIDX_SEQS, DEC_SEQ * N_IDX_HEADS, 1), seq3),
            pl.BlockSpec((SIDX_SEQS, LANES, IDX_DIM), seq3),
            pl.BlockSpec(memory_space=pl.ANY),
        ],
        out_specs=pl.BlockSpec((SIDX_SEQS, DEC_SEQ, S_KEYS), seq3),
        scratch_shapes=[
            pltpu.VMEM((2, SIDX_SEQS, N_PAGES, IDX_DIM, PAGE_SIZE), f32),
            pltpu.SemaphoreType.DMA((2,)),
        ],
    )
    return pl.pallas_call(
        functools.partial(_sidx_kernel, first_page=layer * n_pool),
        grid_spec=grid_spec,
        out_shape=jax.ShapeDtypeStruct((DEC_BATCH, DEC_SEQ, S_KEYS), f32),
        compiler_params=_params("arbitrary"),
        name="dsa_sample_index",
    )(page_table, qi32, w32, ki_new, cache_kidx)


def _stopk_kernel(sc_ref, bias_ref, key_s):
    shape = sc_ref.shape
    pos = lax.broadcasted_iota(jnp.int32, shape, 0)
    t = (pl.program_id(0) * LANES + lax.broadcasted_iota(jnp.int32, shape, 1)) % DEC_SEQ
    valid = pos <= t + PAST_LEN
    k_top = min(TOPK_MAX, (PAST_LEN + DEC_SEQ) // 4)
    key_s[...] = jnp.where(valid, sc_ref[...], -jnp.inf)
    thr = _kth_largest(key_s, S_KEYS, k_top)
    _topk_bias_t(key_s[...], thr, k_top, bias_ref)


def _sample_topk_bias(sc_t):
    spec = pl.BlockSpec((S_KEYS, LANES), lambda i: (0, i))
    return pl.pallas_call(
        _stopk_kernel,
        grid=(S_ROWS // LANES,),
        in_specs=[spec],
        out_specs=spec,
        out_shape=jax.ShapeDtypeStruct((S_KEYS, S_ROWS), f32),
        scratch_shapes=[pltpu.VMEM((S_KEYS, LANES), f32)],
        compiler_params=_params("arbitrary"),
        name="dsa_sample_topk",
    )(sc_t)


def _satt_kernel(pt_ref, q_ref, bias_ref, kn_ref, vn_ref, ck_hbm, cv_hbm, o_ref,
                 kbuf, vbuf, sem, *, first_page):
    step = pl.program_id(0)
    slot = step % 2

    def page_copies(st, sl):
        copies = []
        for g in range(SATT_SEQS):
            for j in range(N_PAGES):
                page = first_page + pt_ref[st * SATT_SEQS + g, j]
                copies.append(pltpu.make_async_copy(ck_hbm.at[page], kbuf.at[sl, g, j], sem.at[sl, 0]))
                copies.append(pltpu.make_async_copy(cv_hbm.at[page], vbuf.at[sl, g, j], sem.at[sl, 1]))
        return copies

    @pl.when(step == 0)
    def _():
        for c in page_copies(0, 0):
            c.start()

    @pl.when(step + 1 < pl.num_programs(0))
    def _():
        for c in page_copies(step + 1, 1 - slot):
            c.start()

    for c in page_copies(step, slot):
        c.wait()

    for g in range(SATT_SEQS):
        for kh in range(N_KV_HEADS):
            cols = slice(kh * HEAD_DIM, (kh + 1) * HEAD_DIM)
            head_rows = pl.ds(kh, PAGE_SIZE, stride=N_KV_HEADS)
            k_all = jnp.concatenate([kbuf[slot, g, j, head_rows, :].astype(bf16) for j in range(N_PAGES)]
                                    + [kn_ref[g, :, cols]], axis=0)
            v_all = jnp.concatenate([vbuf[slot, g, j, head_rows, :].astype(bf16) for j in range(N_PAGES)]
                                    + [vn_ref[g, :, cols]], axis=0)
            s = _dot_nt(q_ref[g, kh], k_all)
            s = (s.reshape(Q_PER_KV, DEC_SEQ, S_KEYS) + bias_ref[g][None]).reshape(Q_PER_KV * DEC_SEQ, S_KEYS)
            p = jnp.exp2(s - jnp.max(s, axis=1, keepdims=True))
            denom = jnp.sum(p, axis=1, keepdims=True)
            o_ref[g, kh] = (_dot(p.astype(bf16), v_all) / denom).astype(bf16)


def _sample_attention(page_table, q16, bias, k_new, v_new, cache_k, cache_v, layer):
    n_pool = cache_k.shape[0] // (DEPTH // 2)
    rows = DEC_SEQ * Q_PER_KV
    seq3 = lambda n, pt: (n, 0, 0)
    seq4 = lambda n, pt: (n, 0, 0, 0)
    page_shape = (PAGE_SIZE * N_KV_HEADS, HEAD_DIM)
    grid_spec = pltpu.PrefetchScalarGridSpec(
        num_scalar_prefetch=1,
        grid=(DEC_BATCH // SATT_SEQS,),
        in_specs=[
            pl.BlockSpec((SATT_SEQS, N_KV_HEADS, rows, HEAD_DIM), seq4),
            pl.BlockSpec((SATT_SEQS, DEC_SEQ, S_KEYS), seq3),
            pl.BlockSpec((SATT_SEQS, LANES, KV_W), seq3),
            pl.BlockSpec((SATT_SEQS, LANES, KV_W), seq3),
            pl.BlockSpec(memory_space=pl.ANY),
            pl.BlockSpec(memory_space=pl.ANY),
        ],
        out_specs=pl.BlockSpec((SATT_SEQS, N_KV_HEADS, rows, HEAD_DIM), seq4),
        scratch_shapes=[
            pltpu.VMEM((2, SATT_SEQS, N_PAGES) + page_shape, f32),
            pltpu.VMEM((2, SATT_SEQS, N_PAGES) + page_shape, f32),
            pltpu.SemaphoreType.DMA((2, 2)),
        ],
    )
    return pl.pallas_call(
        functools.partial(_satt_kernel, first_page=layer * n_pool),
        grid_spec=grid_spec,
        out_shape=jax.ShapeDtypeStruct((DEC_BATCH, N_KV_HEADS, rows, HEAD_DIM), bf16),
        compiler_params=_params("arbitrary"),
        name="dsa_sample_attn",
    )(page_table, q16, bias, k_new, v_new, cache_k, cache_v)


def _rope_tables():
    pos = np.concatenate([np.arange(SEQ), PAST_LEN + np.arange(TM) % DEC_SEQ]).astype(np.float64)

    def pattern(dim, n_rot_groups):
        r = dim // ROT_DIV
        half = r // 2
        inv = ROPE_THETA ** (-np.arange(half, dtype=np.float64) * 2.0 / r)
        ang = pos[:, None] * inv[None, :]
        cos, sin = np.cos(ang), np.sin(ang)
        ones = np.ones((pos.shape[0], dim - r))
        zeros = np.zeros((pos.shape[0], dim - r))
        zh = np.zeros_like(sin)
        c = np.concatenate([cos, cos, ones], axis=1)
        lo = np.concatenate([-sin, zh, zeros], axis=1)
        hi = np.concatenate([zh, sin, zeros], axis=1)
        ident = (np.ones((pos.shape[0], dim)), np.zeros((pos.shape[0], dim)), np.zeros((pos.shape[0], dim)))
        reps = LANES // dim
        parts = [(c, lo, hi) if j < n_rot_groups else ident for j in range(reps)]
        table = np.stack([np.concatenate([p_[t] for p_ in parts], axis=1) for t in range(3)])
        return jnp.asarray(table.astype(np.float32))

    return pattern(HEAD_DIM, 1), pattern(IDX_DIM, 2), pattern(IDX_DIM, 1)


def _sample_mix_tables(w_s, b_s):
    t = jnp.arange(SUBLANES) % DEC_SEQ
    coefs = []
    for d in range(DEC_SEQ):
        c = jnp.where((t - d >= 0)[None, :], w_s[:, t, jnp.maximum(t - d, 0)], 0.0)
        coefs.append(jnp.repeat(c.T, GROUP_DIM_A, axis=1))
    bias = jnp.repeat(b_s[:, t].T, GROUP_DIM_A, axis=1)
    return jnp.stack(coefs), bias


def kernel(x_prompt, x_sample, cache_k, cache_v, cache_kidx, page_table, norm_mix, norm_ffn,
           a_w_in, a_v_gain, a_w_s, a_b_s, a_w_out, b_w_in, b_w_out, ffn_w1, ffn_w2, norm_final):
    x = (x_prompt.reshape(P_ROWS, D_MODEL), x_sample.reshape(S_ROWS, D_MODEL))
    kv_new = None
    rope_h, rope_i, rope_k = _rope_tables()
    n_pool = cache_k.shape[1]
    ck = cache_k.reshape(-1, PAGE_SIZE * N_KV_HEADS, HEAD_DIM)
    cv = cache_v.reshape(-1, PAGE_SIZE * N_KV_HEADS, HEAD_DIM)
    cki = jnp.swapaxes(cache_kidx, 2, 3).reshape(-1, IDX_DIM, PAGE_SIZE)
    g_final = norm_final.reshape(1, D_MODEL)
    a_w_in_b, a_w_out_b, b_w_out_b = a_w_in.astype(bf16), a_w_out.astype(bf16), b_w_out.astype(bf16)
    b_w_in_b = jnp.pad(b_w_in, ((0, 0), (0, 0), (0, B_PROJ_PAD - B_PROJ))).astype(bf16)
    ffn_w1_b, ffn_w2_b = ffn_w1.astype(bf16), ffn_w2.astype(bf16)
    outs = {name: [] for name in ("kip", "kis", "cvp", "cvs")}

    for i in range(DEPTH):
        j = i // 2
        g_mix = norm_mix[i].reshape(1, D_MODEL)
        if i % 2 == 0:
            scoef, sbias = _sample_mix_tables(a_w_s[j], a_b_s[j])
            bias = jnp.repeat(a_b_s[j].T, GROUP_DIM_A, axis=1)
            *m, cvp, cvs = _mixa(x, g_mix, a_w_in_b, j, a_v_gain[j].reshape(1, D_A), a_w_s[j], bias, scoef, sbias)
            outs["cvp"].append(cvp)
            outs["cvs"].append(cvs.reshape(DEC_BATCH, DEC_SEQ, D_A))
            w_out = a_w_out_b
        else:
            q, kb, vb, qi, kiwi, kib, wi_t, *kv_new = _proj(x[0], g_mix, b_w_in_b, rope_h, rope_i, rope_k, j, kv_new)
            m_p = _dsa_prompt(q, qi, wi_t, kb, vb, kib)
            qi32 = qi[P_ROWS:].reshape(DEC_BATCH, DEC_SEQ * N_IDX_HEADS, IDX_DIM)
            w32 = kiwi[P_ROWS:, IDX_DIM:IDX_DIM + N_IDX_HEADS].reshape(DEC_BATCH, DEC_SEQ * N_IDX_HEADS, 1)
            pad_rows = lambda a: jnp.pad(a.reshape(DEC_BATCH, DEC_SEQ, -1), ((0, 0), (0, LANES - DEC_SEQ), (0, 0)))
            sc = _sample_index_scores(page_table, qi32, w32, pad_rows(kib[P_ROWS:]), cki, j)
            bias = _sample_topk_bias(sc.reshape(S_ROWS, S_KEYS).T).T.reshape(DEC_BATCH, DEC_SEQ, S_KEYS)
            q16 = q[P_ROWS:].reshape(DEC_BATCH, DEC_SEQ, N_KV_HEADS, Q_PER_KV, HEAD_DIM)
            q16 = q16.transpose(0, 2, 3, 1, 4).reshape(DEC_BATCH, N_KV_HEADS, Q_PER_KV * DEC_SEQ, HEAD_DIM)
            v_new = vb[P_ROWS:].reshape(S_ROWS, N_KV_HEADS, 2, HEAD_DIM)[:, :, 0]
            o16 = _sample_attention(page_table, q16, bias, pad_rows(kb[P_ROWS:]), pad_rows(v_new), ck, cv, j)
            m_s = o16.reshape(DEC_BATCH, N_KV_HEADS, Q_PER_KV, DEC_SEQ, HEAD_DIM).transpose(0, 3, 1, 2, 4)
            m = (m_p, m_s.reshape(S_ROWS, Q_W))
            outs["kip"].append(kiwi[:P_ROWS, :IDX_DIM].reshape(BATCH, SEQ, IDX_DIM))
            outs["kis"].append(kiwi[P_ROWS:, :IDX_DIM].reshape(DEC_BATCH, DEC_SEQ, IDX_DIM))
            w_out = b_w_out_b
        x = tuple(_post(x, tuple(m), w_out, j, norm_ffn[i].reshape(1, D_MODEL), ffn_w1_b, ffn_w2_b, i, g_final,
                        final=(i == DEPTH - 1)))

    y_prompt = x[0].reshape(BATCH, SEQ, D_MODEL)
    y_sample = x[1].reshape(DEC_BATCH, DEC_SEQ, D_MODEL)
    n_layers = DEPTH // 2
    kp, ks, vp, vs = kv_new
    prompt_kv = lambda a: a.reshape(n_layers, BATCH, SEQ, N_KV_HEADS, HEAD_DIM)
    sample_kv = lambda a: a.reshape(n_layers, DEC_BATCH, DEC_SEQ, N_KV_HEADS, HEAD_DIM)
    st = jnp.stack
    return (y_prompt, y_sample, prompt_kv(kp), prompt_kv(vp), st(outs["kip"]), sample_kv(ks), sample_kv(vs),
            st(outs["kis"]), st(outs["cvp"]), st(outs["cvs"]))
```

```python
import functools

import jax
import jax.numpy as jnp
import numpy as np
from jax import lax
from jax.experimental import pallas as pl
from jax.experimental.pallas import tpu as pltpu

D_MODEL = 1024
BATCH = 8
SEQ = 2048
DEPTH = 4
DEC_BATCH = 128
DEC_SEQ = 4
PAST_LEN = 2048
PAGE_SIZE = 128
N_PAGES = PAST_LEN // PAGE_SIZE
CHUNK = 128
D_A = D_MODEL
N_GROUPS_A = 8
GROUP_DIM_A = D_A // N_GROUPS_A
N_HEADS = 8
HEAD_DIM = D_MODEL // N_HEADS
N_KV_HEADS = 2
Q_PER_KV = N_HEADS // N_KV_HEADS
N_IDX_HEADS = 8
IDX_DIM = 64
TOPK_MAX = 256
QUERY_BLOCK = 128
ROPE_THETA = 500000.0
ROT_DIV = 4
IDX_SCALE = float((IDX_DIM * N_IDX_HEADS) ** -0.5)
ATTN_SCALE = float(HEAD_DIM ** -0.5)
Q_W = N_HEADS * HEAD_DIM
KV_W = N_KV_HEADS * HEAD_DIM
QI_W = N_IDX_HEADS * IDX_DIM
B_PROJ = Q_W + 2 * KV_W + QI_W + IDX_DIM + N_IDX_HEADS
D_FF = 4 * D_MODEL
EPS = 1e-6

LANES = 128
SUBLANES = 8
P_ROWS = BATCH * SEQ
S_ROWS = DEC_BATCH * DEC_SEQ
ROWS = P_ROWS + S_ROWS
TM = 512
N_TILES = ROWS // TM
P_TILES = P_ROWS // TM
TILES_PER_SEQ = SEQ // TM
CHUNKS_PER_TILE = TM // CHUNK
FF_TILE = 1024
B_PROJ_PAD = Q_W + 2 * KV_W + QI_W + LANES
KIWI_OFF = Q_W + 2 * KV_W + QI_W
S_KEYS = PAST_LEN + LANES
ATT_CHUNK = 256
KV_SPAN = 512
VMEM_LIMIT = 56 * 1024 * 1024
INT_MIN = -(2 ** 31)
F32_LOWEST = -3.4028234663852886e38
LOG2_E = 1.4426950408889634
SATT_SEQS = 2
SIDX_SEQS = 4
BISECT_ROWS = 64

f32 = jnp.float32
bf16 = jnp.bfloat16

assert S_ROWS == TM and P_ROWS % TM == 0 and SEQ % TM == 0


def _params(*sem):
    return pltpu.CompilerParams(dimension_semantics=sem, vmem_limit_bytes=VMEM_LIMIT)


def _rms(x, g):
    return x * lax.rsqrt(jnp.mean(x * x, axis=-1, keepdims=True) + EPS) * g


def _gelu(x):
    a = -2.0 * LOG2_E * 0.7978845608028654
    return x / (1.0 + jnp.exp2(x * (a + (a * 0.044715) * (x * x))))


def _dot(a, b):
    return jnp.dot(a, b, preferred_element_type=f32)


def _dot_nt(a, b):
    return lax.dot_general(a, b, (((1,), (1,)), ((), ())), preferred_element_type=f32)


def _prompt_tile(i):
    return (jnp.minimum(i, P_TILES - 1), 0)


def _first_tile(i):
    return (0, 0)


def _pair_specs(width):
    return [pl.BlockSpec((TM, width), _prompt_tile), pl.BlockSpec((S_ROWS, width), _first_tile)]


def _layer_spec(shape, layer, **kwargs):
    return pl.BlockSpec((None,) + shape, lambda i: (layer, 0, 0), **kwargs)


def _rows_specs(n_arrays, width):
    return [pl.BlockSpec((TM, width), lambda i: (i, 0))] if n_arrays == 1 else _pair_specs(width)


def _rows_tile(i, refs):
    if len(refs) == 1:
        return refs[0][...]
    p_ref, s_ref = refs
    return jnp.where(i == P_TILES, s_ref[...], p_ref[...])


def _store_rows_tile(i, refs, value):
    if len(refs) == 1:
        refs[0][...] = value
        return
    p_ref, s_ref = refs

    @pl.when(i < P_TILES)
    def _():
        p_ref[...] = value

    @pl.when(i == P_TILES)
    def _():
        s_ref[...] = value


def _mixa_kernel(*refs, n_x):
    x_refs = refs[:n_x]
    g_ref, win_ref, gain_ref, ws_ref, bias_ref, scoef_ref, sbias_ref, m_ref, cvp_ref, cvs_ref = refs[n_x:]
    i = pl.program_id(0)
    h = _rms(_rows_tile(i, x_refs), g_ref[...]).astype(bf16)
    uv = _gelu(_dot(h, win_ref[...]))
    u = uv[:, :D_A]
    v = uv[:, D_A:]
    vc = v - jnp.mean(v, axis=-1, keepdims=True)
    vn = vc * lax.rsqrt(jnp.mean(vc * vc, axis=-1, keepdims=True) + EPS) * gain_ref[...]

    @pl.when(i < P_TILES)
    def _prompt():
        t_idx = lax.broadcasted_iota(jnp.int32, (CHUNK, CHUNK), 0)
        s_idx = lax.broadcasted_iota(jnp.int32, (CHUNK, CHUNK), 1)
        causal = s_idx <= t_idx
        ws = [jnp.where(causal, ws_ref[g], 0.0).astype(bf16) for g in range(N_GROUPS_A)]
        for c in range(CHUNKS_PER_TILE):
            rows = slice(c * CHUNK, (c + 1) * CHUNK)
            v_c = vn[rows].astype(bf16)
            mixed = jnp.concatenate(
                [_dot(ws[g], v_c[:, g * GROUP_DIM_A:(g + 1) * GROUP_DIM_A]) for g in range(N_GROUPS_A)],
                axis=1) + bias_ref[...]
            m_ref[rows, :] = (u[rows] * mixed).astype(bf16)

        @pl.when(i % TILES_PER_SEQ == TILES_PER_SEQ - 1)
        def _():
            cvp_ref[0] = vn[TM - CHUNK:]

    @pl.when(i == P_TILES)
    def _sample():
        grp = TM // SUBLANES
        v3 = vn.reshape(grp, SUBLANES, D_A)
        mixed = sbias_ref[...][None] + scoef_ref[0][None] * v3
        for d in range(1, DEC_SEQ):
            mixed = mixed + scoef_ref[d][None] * pltpu.roll(v3, d, 1)
        m_ref[...] = (u.reshape(grp, SUBLANES, D_A) * mixed).reshape(TM, D_A).astype(bf16)
        cvs_ref[...] = vn


def _mixa(x, g, w_in, layer, gain, ws, bias, scoef, sbias):
    const2 = lambda i: (0, 0)
    const3 = lambda i: (0, 0, 0)
    return pl.pallas_call(
        functools.partial(_mixa_kernel, n_x=len(x)),
        grid=(N_TILES,),
        in_specs=_rows_specs(len(x), D_MODEL) + [
            pl.BlockSpec((1, D_MODEL), const2),
            _layer_spec((D_MODEL, 2 * D_A), layer),
            pl.BlockSpec((1, D_A), const2),
            pl.BlockSpec((N_GROUPS_A, CHUNK, CHUNK), const3),
            pl.BlockSpec((CHUNK, D_A), const2),
            pl.BlockSpec((DEC_SEQ, SUBLANES, D_A), const3),
            pl.BlockSpec((SUBLANES, D_A), const2),
        ],
        out_specs=[
            pl.BlockSpec((TM, D_A), lambda i: (i, 0)),
            pl.BlockSpec((1, CHUNK, D_A), lambda i: (jnp.minimum(i // TILES_PER_SEQ, BATCH - 1), 0, 0)),
            pl.BlockSpec((S_ROWS, D_A), const2),
        ],
        out_shape=[
            jax.ShapeDtypeStruct((ROWS, D_A), bf16),
            jax.ShapeDtypeStruct((BATCH, CHUNK, D_A), f32),
            jax.ShapeDtypeStruct((S_ROWS, D_A), f32),
        ],
        compiler_params=_params("arbitrary"),
        name="mixer_a",
    )(*x, g, w_in, gain, ws, bias, scoef, sbias)


def _post_kernel(*refs, n_x, n_m, final):
    x_refs, m_refs = refs[:n_x], refs[n_x:n_x + n_m]
    wout_ref, g_ref, w1_ref, w2_ref, gf_ref = refs[n_x + n_m:n_x + n_m + 5]
    y_refs = refs[n_x + n_m + 5:]
    i = pl.program_id(0)
    x1 = _rows_tile(i, x_refs) + _dot(_rows_tile(i, m_refs), wout_ref[...])
    h = _rms(x1, g_ref[...]).astype(bf16)
    y = x1
    for c in range(D_FF // FF_TILE):
        a = jnp.maximum(_dot(h, w1_ref[:, c * FF_TILE:(c + 1) * FF_TILE].astype(bf16)), 0.0)
        y = y + _dot((a * a).astype(bf16), w2_ref[c * FF_TILE:(c + 1) * FF_TILE, :].astype(bf16))
    if final:
        y = _rms(y, gf_ref[...])
    _store_rows_tile(i, y_refs, y)


def _post(x, m, w_out, mixer_layer, g, w1, w2, layer, g_final, final):
    const2 = lambda i: (0, 0)
    resident = dict(pipeline_mode=pl.Buffered(1))
    if final:
        out_specs = _pair_specs(D_MODEL)
        out_shape = [jax.ShapeDtypeStruct((P_ROWS, D_MODEL), f32), jax.ShapeDtypeStruct((S_ROWS, D_MODEL), f32)]
    else:
        out_specs = _rows_specs(1, D_MODEL)
        out_shape = [jax.ShapeDtypeStruct((ROWS, D_MODEL), f32)]
    return pl.pallas_call(
        functools.partial(_post_kernel, n_x=len(x), n_m=len(m), final=final),
        grid=(N_TILES,),
        in_specs=_rows_specs(len(x), D_MODEL) + _rows_specs(len(m), D_MODEL) + [
            _layer_spec((D_MODEL, D_MODEL), mixer_layer, **resident),
            pl.BlockSpec((1, D_MODEL), const2),
            _layer_spec((D_MODEL, D_FF), layer, **resident),
            _layer_spec((D_FF, D_MODEL), layer, **resident),
            pl.BlockSpec((1, D_MODEL), const2),
        ],
        out_specs=out_specs,
        out_shape=out_shape,
        compiler_params=_params("arbitrary"),
        name="post_ffn",
    )(*x, *m, w_out, g, w1, w2, g_final)


def _rope(x, cos, sin_lo, sin_hi, half):
    return x * cos + pltpu.roll(x, LANES - half, 1) * sin_lo + pltpu.roll(x, half, 1) * sin_hi


def _proj_kernel(*refs, n_alias):
    x_ref, g_ref, w_ref, rh_ref, ri_ref, rk_ref = refs[:6]
    (q_ref, kb_ref, vb_ref, qi_ref, kiwi_ref, kib_ref, wit_ref,
     kp_ref, ks_ref, vp_ref, vs_ref) = refs[6 + n_alias:]
    i = pl.program_id(0)
    h = _rms(x_ref[...], g_ref[...]).astype(bf16)
    p = _dot(h, w_ref[...])
    half_h = HEAD_DIM // ROT_DIV // 2
    half_i = IDX_DIM // ROT_DIV // 2
    ch, slh, shh = rh_ref[0], rh_ref[1], rh_ref[2]
    for hd in range(N_HEADS):
        cols = slice(hd * HEAD_DIM, (hd + 1) * HEAD_DIM)
        q_ref[:, cols] = (_rope(p[:, cols], ch, slh, shh, half_h) * (ATTN_SCALE * LOG2_E)).astype(bf16)
    one_hot = jnp.where(lax.broadcasted_iota(jnp.int32, (TM, HEAD_DIM), 1) == 0, 1.0, 0.0).astype(bf16)
    for kh in range(N_KV_HEADS):
        cols = slice(kh * HEAD_DIM, (kh + 1) * HEAD_DIM)
        kr = _rope(p[:, Q_W + kh * HEAD_DIM:Q_W + (kh + 1) * HEAD_DIM], ch, slh, shh, half_h)
        v = p[:, Q_W + KV_W + kh * HEAD_DIM:Q_W + KV_W + (kh + 1) * HEAD_DIM]
        kb_ref[:, cols] = kr.astype(bf16)
        vb_ref[:, 2 * kh * HEAD_DIM:(2 * kh + 1) * HEAD_DIM] = v.astype(bf16)
        vb_ref[:, (2 * kh + 1) * HEAD_DIM:(2 * kh + 2) * HEAD_DIM] = one_hot
        head_rows = pl.ds(kh, TM, stride=N_KV_HEADS)

        for slot in range(kp_ref.shape[0]):
            @pl.when(i < P_TILES)
            def _(slot=slot):
                kp_ref[slot, head_rows, :] = kr
                vp_ref[slot, head_rows, :] = v

            @pl.when(i == P_TILES)
            def _(slot=slot):
                ks_ref[slot, head_rows, :] = kr
                vs_ref[slot, head_rows, :] = v
    ci, sli, shi = ri_ref[0], ri_ref[1], ri_ref[2]
    off = Q_W + 2 * KV_W
    for j in range(QI_W // LANES):
        cols = slice(j * LANES, (j + 1) * LANES)
        qi_ref[:, cols] = _rope(p[:, off + j * LANES:off + (j + 1) * LANES], ci, sli, shi, half_i).astype(bf16)
    kiwi = _rope(p[:, KIWI_OFF:KIWI_OFF + LANES], rk_ref[0], rk_ref[1], rk_ref[2], half_i)
    kiwi_ref[...] = kiwi
    kib_ref[...] = kiwi[:, :IDX_DIM].astype(bf16)
    wit_ref[...] = kiwi.T[IDX_DIM:IDX_DIM + N_IDX_HEADS, :]


def _proj(x, g, w, rope_h, rope_i, rope_k, layer, kv_prev):
    const2 = lambda i: (0, 0)
    rope_map = lambda i: (0, jnp.where(i < P_TILES, i % TILES_PER_SEQ, TILES_PER_SEQ), 0)
    row = lambda i: (i, 0)
    widths = [(Q_W, bf16), (KV_W, bf16), (2 * KV_W, bf16), (QI_W, bf16), (LANES, f32), (IDX_DIM, bf16)]
    n_layers = DEPTH // 2
    kv_rows = TM * N_KV_HEADS
    slots, first = (n_layers, 0) if kv_prev is None else (1, layer)
    kv_specs = [pl.BlockSpec((slots, kv_rows, HEAD_DIM), lambda i: (first, jnp.minimum(i, P_TILES - 1), 0)),
                pl.BlockSpec((slots, kv_rows, HEAD_DIM), lambda i: (first, 0, 0))] * 2
    kv_shapes = [jax.ShapeDtypeStruct((n_layers, P_ROWS * N_KV_HEADS, HEAD_DIM), f32),
                 jax.ShapeDtypeStruct((n_layers, S_ROWS * N_KV_HEADS, HEAD_DIM), f32)] * 2
    n_in = 6
    n_out = len(widths) + 1
    alias_in = [] if kv_prev is None else list(kv_prev)
    return pl.pallas_call(
        functools.partial(_proj_kernel, n_alias=len(alias_in)),
        grid=(N_TILES,),
        in_specs=_rows_specs(1, D_MODEL) + [
            pl.BlockSpec((1, D_MODEL), const2),
            _layer_spec((D_MODEL, B_PROJ_PAD), layer),
            pl.BlockSpec((3, TM, LANES), rope_map),
            pl.BlockSpec((3, TM, LANES), rope_map),
            pl.BlockSpec((3, TM, LANES), rope_map),
        ] + [pl.BlockSpec(memory_space=pl.ANY)] * len(alias_in),
        out_specs=[pl.BlockSpec((TM, w_), row) for w_, _ in widths]
        + [pl.BlockSpec((N_IDX_HEADS, TM), lambda i: (0, i))] + kv_specs,
        out_shape=[jax.ShapeDtypeStruct((ROWS, w_), dt) for w_, dt in widths]
        + [jax.ShapeDtypeStruct((N_IDX_HEADS, ROWS), f32)] + kv_shapes,
        input_output_aliases={n_in + a: n_out + a for a in range(len(alias_in))},
        compiler_params=_params("arbitrary"),
        name="dsa_proj",
    )(x, g, w, rope_h, rope_i, rope_k, *alias_in)


def _order_float(key):
    return pltpu.bitcast(key ^ ((key >> 31) & 0x7FFFFFFF), f32)


def _count0(mask):
    return jnp.sum(jnp.where(mask, 1.0, 0.0), axis=0, keepdims=True)


def _kth_largest(sc_ref, n_keys, k):
    kf = float(k)

    def count_ge(cand_key):
        cand = _order_float(cand_key)
        acc = jnp.zeros((BISECT_ROWS, LANES), f32)
        for c in range(n_keys // BISECT_ROWS):
            acc = jnp.where(sc_ref[c * BISECT_ROWS:(c + 1) * BISECT_ROWS, :] >= cand, acc + 1.0, acc)
        return jnp.sum(acc, axis=0, keepdims=True)

    zero = jnp.zeros((1, LANES), jnp.int32)
    thr = jnp.where(count_ge(zero) >= kf, zero, zero + INT_MIN)

    def bit_step(j, thr):
        cand = thr | lax.shift_left(jnp.int32(1), 30 - j)
        return jnp.where(count_ge(cand) >= kf, cand, thr)

    thr = _order_float(lax.fori_loop(0, 31, bit_step, thr))
    return jnp.where(thr >= F32_LOWEST, thr, F32_LOWEST)


def _topk_bias_t(sc, thr, k, out_ref):
    n_keys = sc.shape[0]
    kf = float(k)
    ge = sc >= thr
    excess = jnp.max(_count0(ge)) - kf

    @pl.when(excess <= 0.0)
    def _():
        out_ref[:n_keys, :] = jnp.where(ge, 0.0, -jnp.inf)

    @pl.when(excess > 0.0)
    def _():
        pos = lax.broadcasted_iota(jnp.int32, sc.shape, 0)
        above = sc > thr
        tie = sc == thr
        need = kf - _count0(above)
        n_bits = max(1, (n_keys - 1).bit_length())

        def step(j, p):
            cand = p + lax.shift_left(jnp.int32(1), n_bits - 1 - j)
            return jnp.where(_count0(tie & (pos < cand)) < need, cand, p)

        last = lax.fori_loop(0, n_bits, step, jnp.zeros(thr.shape, jnp.int32))
        out_ref[:n_keys, :] = jnp.where(above | (tie & (pos <= last)), 0.0, -jnp.inf)


def _dsa_prompt_kernel(q_ref, qi_ref, wt_ref, k_ref, v_ref, ki_ref, o_ref, key_s, thr_s, biast_s, bias_s):
    i = pl.program_id(1)
    k_top = min(TOPK_MAX, SEQ // 4)
    blocks_per_span = KV_SPAN // QUERY_BLOCK

    def run(span):
        n_keys = (span + 1) * KV_SPAN
        ki = ki_ref[:n_keys, :]
        wt = wt_ref[...] * IDX_SCALE
        sc = None
        for hd in range(N_IDX_HEADS):
            d = _dot_nt(ki, qi_ref[:, hd * IDX_DIM:(hd + 1) * IDX_DIM])
            term = jnp.maximum(d, 0.0) * wt[hd:hd + 1, :]
            sc = term if sc is None else sc + term
        pos = lax.broadcasted_iota(jnp.int32, (n_keys, QUERY_BLOCK), 0)
        query = i * QUERY_BLOCK + lax.broadcasted_iota(jnp.int32, (n_keys, QUERY_BLOCK), 1)
        valid = pos <= query
        key_s[:n_keys, :] = jnp.where(valid, sc, -jnp.inf)

        thr_s[...] = jnp.full(thr_s.shape, F32_LOWEST, f32)
        for blk in range(span * blocks_per_span, (span + 1) * blocks_per_span):
            if (blk + 1) * QUERY_BLOCK > k_top:
                @pl.when(i == blk)
                def _(blk=blk):
                    thr_s[0:1, :] = _kth_largest(key_s, (blk + 1) * QUERY_BLOCK, k_top)

        _topk_bias_t(key_s[:n_keys, :], thr_s[0:1, :], k_top, biast_s)
        bias_s[:, :n_keys] = biast_s[:n_keys, :].T
        bias = bias_s[:, :n_keys][None]
        for kh in range(N_KV_HEADS):
            q_rows = jnp.concatenate(
                [q_ref[:, (kh * Q_PER_KV + g) * HEAD_DIM:(kh * Q_PER_KV + g + 1) * HEAD_DIM]
                 for g in range(Q_PER_KV)], axis=0)
            chunks = [slice(c, c + ATT_CHUNK) for c in range(0, n_keys, ATT_CHUNK)]
            s = []
            for ck in chunks:
                s_c = _dot_nt(q_rows, k_ref[ck, kh * HEAD_DIM:(kh + 1) * HEAD_DIM])
                s.append((s_c.reshape(Q_PER_KV, QUERY_BLOCK, ATT_CHUNK) + bias[:, :, ck])
                         .reshape(Q_PER_KV * QUERY_BLOCK, ATT_CHUNK))
            row_max = functools.reduce(jnp.maximum, [jnp.max(s_c, axis=1, keepdims=True) for s_c in s])
            o = None
            for ck, s_c in zip(chunks, s):
                o_c = _dot(jnp.exp2(s_c - row_max).astype(bf16), v_ref[ck, 2 * kh * HEAD_DIM:(2 * kh + 2) * HEAD_DIM])
                o = o_c if o is None else o + o_c
            o = o[:, :HEAD_DIM] / o[:, HEAD_DIM:HEAD_DIM + 1]
            for g in range(Q_PER_KV):
                hd = kh * Q_PER_KV + g
                o_ref[:, hd * HEAD_DIM:(hd + 1) * HEAD_DIM] = o[g * QUERY_BLOCK:(g + 1) * QUERY_BLOCK].astype(bf16)

    for span in range(SEQ // KV_SPAN):
        pl.when(i // blocks_per_span == span)(functools.partial(run, span))


def _dsa_prompt(q, qi, wi_t, kb, vb, kib):
    nq = SEQ // QUERY_BLOCK
    qmap = lambda b, i: (b * nq + i, 0)
    kmap = lambda b, i: (b, 0)
    return pl.pallas_call(
        _dsa_prompt_kernel,
        grid=(BATCH, nq),
        in_specs=[
            pl.BlockSpec((QUERY_BLOCK, Q_W), qmap),
            pl.BlockSpec((QUERY_BLOCK, QI_W), qmap),
            pl.BlockSpec((N_IDX_HEADS, QUERY_BLOCK), lambda b, i: (0, b * nq + i)),
            pl.BlockSpec((SEQ, KV_W), kmap),
            pl.BlockSpec((SEQ, 2 * KV_W), kmap),
            pl.BlockSpec((SEQ, IDX_DIM), kmap),
        ],
        out_specs=pl.BlockSpec((QUERY_BLOCK, Q_W), qmap),
        out_shape=jax.ShapeDtypeStruct((P_ROWS, Q_W), bf16),
        scratch_shapes=[
            pltpu.VMEM((SEQ, QUERY_BLOCK), f32),
            pltpu.VMEM((SUBLANES, QUERY_BLOCK), f32),
            pltpu.VMEM((SEQ, QUERY_BLOCK), f32),
            pltpu.VMEM((QUERY_BLOCK, SEQ), f32),
        ],
        compiler_params=_params("arbitrary", "arbitrary"),
        name="dsa_prompt",
    )(q, qi, wi_t, kb, vb, kib)


def _sidx_kernel(pt_ref, qi_ref, w_ref, kin_ref, cki_hbm, sc_ref, pbuf, sem, *, first_page):
    step = pl.program_id(0)
    slot = step % 2

    def page_copies(st, sl):
        return [pltpu.make_async_copy(cki_hbm.at[first_page + pt_ref[st * SIDX_SEQS + g, j]], pbuf.at[sl, g, j],
                                      sem.at[sl])
                for g in range(SIDX_SEQS) for j in range(N_PAGES)]

    @pl.when(step == 0)
    def _():
        for c in page_copies(0, 0):
            c.start()

    @pl.when(step + 1 < pl.num_programs(0))
    def _():
        for c in page_copies(step + 1, 1 - slot):
            c.start()

    for c in page_copies(step, slot):
        c.wait()

    for g in range(SIDX_SEQS):
        qi = qi_ref[g]
        d = jnp.concatenate([_dot(qi, pbuf[slot, g, j].astype(bf16)) for j in range(N_PAGES)]
                            + [_dot_nt(qi, kin_ref[g])], axis=1)
        d = jnp.maximum(d, 0.0) * w_ref[g]
        sc_ref[g] = jnp.sum(d.reshape(DEC_SEQ, N_IDX_HEADS, S_KEYS), axis=1) * IDX_SCALE


def _sample_index_scores(page_table, qi32, w32, ki_new, cache_kidx, layer):
    n_pool = cache_kidx.shape[0] // (DEPTH // 2)
    seq3 = lambda n, pt: (n, 0, 0)
    grid_spec = pltpu.PrefetchScalarGridSpec(
        num_scalar_prefetch=1,
        grid=(DEC_BATCH // SIDX_SEQS,),
        in_specs=[
            pl.BlockSpec((SIDX_SEQS, DEC_SEQ * N_IDX_HEADS, IDX_DIM), seq3),
            pl.BlockSpec((SIDX_SEQS, DEC_SEQ * N_IDX_HEADS, 1), seq3),
            pl.BlockSpec((SIDX_SEQS, LANES, IDX_DIM), seq3),
            pl.BlockSpec(memory_space=pl.ANY),
        ],
        out_specs=pl.BlockSpec((SIDX_SEQS, DEC_SEQ, S_KEYS), seq3),
        scratch_shapes=[
            pltpu.VMEM((2, SIDX_SEQS, N_PAGES, IDX_DIM, PAGE_SIZE), f32),
            pltpu.SemaphoreType.DMA((2,)),
        ],
    )
    return pl.pallas_call(
        functools.partial(_sidx_kernel, first_page=layer * n_pool),
        grid_spec=grid_spec,
        out_shape=jax.ShapeDtypeStruct((DEC_BATCH, DEC_SEQ, S_KEYS), f32),
        compiler_params=_params("arbitrary"),
        name="dsa_sample_index",
    )(page_table, qi32, w32, ki_new, cache_kidx)


def _stopk_kernel(sc_ref, bias_ref, key_s):
    shape = sc_ref.shape
    pos = lax.broadcasted_iota(jnp.int32, shape, 0)
    t = (pl.program_id(0) * LANES + lax.broadcasted_iota(jnp.int32, shape, 1)) % DEC_SEQ
    valid = pos <= t + PAST_LEN
    k_top = min(TOPK_MAX, (PAST_LEN + DEC_SEQ) // 4)
    key_s[...] = jnp.where(valid, sc_ref[...], -jnp.inf)
    thr = _kth_largest(key_s, S_KEYS, k_top)
    _topk_bias_t(key_s[...], thr, k_top, bias_ref)


def _sample_topk_bias(sc_t):
    spec = pl.BlockSpec((S_KEYS, LANES), lambda i: (0, i))
    return pl.pallas_call(
        _stopk_kernel,
        grid=(S_ROWS // LANES,),
        in_specs=[spec],
        out_specs=spec,
        out_shape=jax.ShapeDtypeStruct((S_KEYS, S_ROWS), f32),
        scratch_shapes=[pltpu.VMEM((S_KEYS, LANES), f32)],
        compiler_params=_params("arbitrary"),
        name="dsa_sample_topk",
    )(sc_t)


def _satt_kernel(pt_ref, q_ref, bias_ref, kn_ref, vn_ref, ck_hbm, cv_hbm, o_ref,
                 kbuf, vbuf, sem, *, first_page):
    step = pl.program_id(0)
    slot = step % 2

    def page_copies(st, sl):
        copies = []
        for g in range(SATT_SEQS):
            for j in range(N_PAGES):
                page = first_page + pt_ref[st * SATT_SEQS + g, j]
                copies.append(pltpu.make_async_copy(ck_hbm.at[page], kbuf.at[sl, g, j], sem.at[sl, 0]))
                copies.append(pltpu.make_async_copy(cv_hbm.at[page], vbuf.at[sl, g, j], sem.at[sl, 1]))
        return copies

    @pl.when(step == 0)
    def _():
        for c in page_copies(0, 0):
            c.start()

    @pl.when(step + 1 < pl.num_programs(0))
    def _():
        for c in page_copies(step + 1, 1 - slot):
            c.start()

    for c in page_copies(step, slot):
        c.wait()

    for g in range(SATT_SEQS):
        for kh in range(N_KV_HEADS):
            cols = slice(kh * HEAD_DIM, (kh + 1) * HEAD_DIM)
            head_rows = pl.ds(kh, PAGE_SIZE, stride=N_KV_HEADS)
            k_all = jnp.concatenate([kbuf[slot, g, j, head_rows, :].astype(bf16) for j in range(N_PAGES)]
                                    + [kn_ref[g, :, cols]], axis=0)
            v_all = jnp.concatenate([vbuf[slot, g, j, head_rows, :].astype(bf16) for j in range(N_PAGES)]
                                    + [vn_ref[g, :, cols]], axis=0)
            s = _dot_nt(q_ref[g, kh], k_all)
            s = (s.reshape(Q_PER_KV, DEC_SEQ, S_KEYS) + bias_ref[g][None]).reshape(Q_PER_KV * DEC_SEQ, S_KEYS)
            p = jnp.exp2(s - jnp.max(s, axis=1, keepdims=True))
            denom = jnp.sum(p, axis=1, keepdims=True)
            o_ref[g, kh] = (_dot(p.astype(bf16), v_all) / denom).astype(bf16)


def _sample_attention(page_table, q16, bias, k_new, v_new, cache_k, cache_v, layer):
    n_pool = cache_k.shape[0] // (DEPTH // 2)
    rows = DEC_SEQ * Q_PER_KV
    seq3 = lambda n, pt: (n, 0, 0)
    seq4 = lambda n, pt: (n, 0, 0, 0)
    page_shape = (PAGE_SIZE * N_KV_HEADS, HEAD_DIM)
    grid_spec = pltpu.PrefetchScalarGridSpec(
        num_scalar_prefetch=1,
        grid=(DEC_BATCH // SATT_SEQS,),
        in_specs=[
            pl.BlockSpec((SATT_SEQS, N_KV_HEADS, rows, HEAD_DIM), seq4),
            pl.BlockSpec((SATT_SEQS, DEC_SEQ, S_KEYS), seq3),
            pl.BlockSpec((SATT_SEQS, LANES, KV_W), seq3),
            pl.BlockSpec((SATT_SEQS, LANES, KV_W), seq3),
            pl.BlockSpec(memory_space=pl.ANY),
            pl.BlockSpec(memory_space=pl.ANY),
        ],
        out_specs=pl.BlockSpec((SATT_SEQS, N_KV_HEADS, rows, HEAD_DIM), seq4),
        scratch_shapes=[
            pltpu.VMEM((2, SATT_SEQS, N_PAGES) + page_shape, f32),
            pltpu.VMEM((2, SATT_SEQS, N_PAGES) + page_shape, f32),
            pltpu.SemaphoreType.DMA((2, 2)),
        ],
    )
    return pl.pallas_call(
        functools.partial(_satt_kernel, first_page=layer * n_pool),
        grid_spec=grid_spec,
        out_shape=jax.ShapeDtypeStruct((DEC_BATCH, N_KV_HEADS, rows, HEAD_DIM), bf16),
        compiler_params=_params("arbitrary"),
        name="dsa_sample_attn",
    )(page_table, q16, bias, k_new, v_new, cache_k, cache_v)


def _rope_tables():
    pos = np.concatenate([np.arange(SEQ), PAST_LEN + np.arange(TM) % DEC_SEQ]).astype(np.float64)

    def pattern(dim, n_rot_groups):
        r = dim // ROT_DIV
        half = r // 2
        inv = ROPE_THETA ** (-np.arange(half, dtype=np.float64) * 2.0 / r)
        ang = pos[:, None] * inv[None, :]
        cos, sin = np.cos(ang), np.sin(ang)
        ones = np.ones((pos.shape[0], dim - r))
        zeros = np.zeros((pos.shape[0], dim - r))
        zh = np.zeros_like(sin)
        c = np.concatenate([cos, cos, ones], axis=1)
        lo = np.concatenate([-sin, zh, zeros], axis=1)
        hi = np.concatenate([zh, sin, zeros], axis=1)
        ident = (np.ones((pos.shape[0], dim)), np.zeros((pos.shape[0], dim)), np.zeros((pos.shape[0], dim)))
        reps = LANES // dim
        parts = [(c, lo, hi) if j < n_rot_groups else ident for j in range(reps)]
        table = np.stack([np.concatenate([p_[t] for p_ in parts], axis=1) for t in range(3)])
        return jnp.asarray(table.astype(np.float32))

    return pattern(HEAD_DIM, 1), pattern(IDX_DIM, 2), pattern(IDX_DIM, 1)


def _sample_mix_tables(w_s, b_s):
    t = jnp.arange(SUBLANES) % DEC_SEQ
    coefs = []
    for d in range(DEC_SEQ):
        c = jnp.where((t - d >= 0)[None, :], w_s[:, t, jnp.maximum(t - d, 0)], 0.0)
        coefs.append(jnp.repeat(c.T, GROUP_DIM_A, axis=1))
    bias = jnp.repeat(b_s[:, t].T, GROUP_DIM_A, axis=1)
    return jnp.stack(coefs), bias


def kernel(x_prompt, x_sample, cache_k, cache_v, cache_kidx, page_table, norm_mix, norm_ffn,
           a_w_in, a_v_gain, a_w_s, a_b_s, a_w_out, b_w_in, b_w_out, ffn_w1, ffn_w2, norm_final):
    x = (x_prompt.reshape(P_ROWS, D_MODEL), x_sample.reshape(S_ROWS, D_MODEL))
    kv_new = None
    rope_h, rope_i, rope_k = _rope_tables()
    n_pool = cache_k.shape[1]
    ck = cache_k.reshape(-1, PAGE_SIZE * N_KV_HEADS, HEAD_DIM)
    cv = cache_v.reshape(-1, PAGE_SIZE * N_KV_HEADS, HEAD_DIM)
    cki = jnp.swapaxes(cache_kidx, 2, 3).reshape(-1, IDX_DIM, PAGE_SIZE)
    g_final = norm_final.reshape(1, D_MODEL)
    a_w_in_b, a_w_out_b, b_w_out_b = a_w_in.astype(bf16), a_w_out.astype(bf16), b_w_out.astype(bf16)
    b_w_in_b = jnp.pad(b_w_in, ((0, 0), (0, 0), (0, B_PROJ_PAD - B_PROJ))).astype(bf16)
    ffn_w1_b, ffn_w2_b = ffn_w1, ffn_w2
    outs = {name: [] for name in ("kip", "kis", "cvp", "cvs")}

    for i in range(DEPTH):
        j = i // 2
        g_mix = norm_mix[i].reshape(1, D_MODEL)
        if i % 2 == 0:
            scoef, sbias = _sample_mix_tables(a_w_s[j], a_b_s[j])
            bias = jnp.repeat(a_b_s[j].T, GROUP_DIM_A, axis=1)
            *m, cvp, cvs = _mixa(x, g_mix, a_w_in_b, j, a_v_gain[j].reshape(1, D_A), a_w_s[j], bias, scoef, sbias)
            outs["cvp"].append(cvp)
            outs["cvs"].append(cvs.reshape(DEC_BATCH, DEC_SEQ, D_A))
            w_out = a_w_out_b
        else:
            q, kb, vb, qi, kiwi, kib, wi_t, *kv_new = _proj(x[0], g_mix, b_w_in_b, rope_h, rope_i, rope_k, j, kv_new)
            m_p = _dsa_prompt(q, qi, wi_t, kb, vb, kib)
            qi32 = qi[P_ROWS:].reshape(DEC_BATCH, DEC_SEQ * N_IDX_HEADS, IDX_DIM)
            w32 = kiwi[P_ROWS:, IDX_DIM:IDX_DIM + N_IDX_HEADS].reshape(DEC_BATCH, DEC_SEQ * N_IDX_HEADS, 1)
            pad_rows = lambda a: jnp.pad(a.reshape(DEC_BATCH, DEC_SEQ, -1), ((0, 0), (0, LANES - DEC_SEQ), (0, 0)))
            sc = _sample_index_scores(page_table, qi32, w32, pad_rows(kib[P_ROWS:]), cki, j)
            bias = _sample_topk_bias(sc.reshape(S_ROWS, S_KEYS).T).T.reshape(DEC_BATCH, DEC_SEQ, S_KEYS)
            q16 = q[P_ROWS:].reshape(DEC_BATCH, DEC_SEQ, N_KV_HEADS, Q_PER_KV, HEAD_DIM)
            q16 = q16.transpose(0, 2, 3, 1, 4).reshape(DEC_BATCH, N_KV_HEADS, Q_PER_KV * DEC_SEQ, HEAD_DIM)
            v_new = vb[P_ROWS:].reshape(S_ROWS, N_KV_HEADS, 2, HEAD_DIM)[:, :, 0]
            o16 = _sample_attention(page_table, q16, bias, pad_rows(kb[P_ROWS:]), pad_rows(v_new), ck, cv, j)
            m_s = o16.reshape(DEC_BATCH, N_KV_HEADS, Q_PER_KV, DEC_SEQ, HEAD_DIM).transpose(0, 3, 1, 2, 4)
            m = (m_p, m_s.reshape(S_ROWS, Q_W))
            outs["kip"].append(kiwi[:P_ROWS, :IDX_DIM].reshape(BATCH, SEQ, IDX_DIM))
            outs["kis"].append(kiwi[P_ROWS:, :IDX_DIM].reshape(DEC_BATCH, DEC_SEQ, IDX_DIM))
            w_out = b_w_out_b
        x = tuple(_post(x, tuple(m), w_out, j, norm_ffn[i].reshape(1, D_MODEL), ffn_w1_b, ffn_w2_b, i, g_final,
                        final=(i == DEPTH - 1)))

    y_prompt = x[0].reshape(BATCH, SEQ, D_MODEL)
    y_sample = x[1].reshape(DEC_BATCH, DEC_SEQ, D_MODEL)
    n_layers = DEPTH // 2
    kp, ks, vp, vs = kv_new
    prompt_kv = lambda a: a.reshape(n_layers, BATCH, SEQ, N_KV_HEADS, HEAD_DIM)
    sample_kv = lambda a: a.reshape(n_layers, DEC_BATCH, DEC_SEQ, N_KV_HEADS, HEAD_DIM)
    st = jnp.stack
    return (y_prompt, y_sample, prompt_kv(kp), prompt_kv(vp), st(outs["kip"]), sample_kv(ks), sample_kv(vs),
            st(outs["kis"]), st(outs["cvp"]), st(outs["cvs"]))
```
